```python
import jax, jax.numpy as jnp
from jax import lax
import numpy as np

D_MODEL = 2048
BATCH = 4
SEQ = 2048
DEPTH = 2
DEC_BATCH = 128
DEC_SEQ = 4
PAST_LEN = 16384
PAGE_SIZE = 128

N_BRANCH = 4
D_BRANCH = D_MODEL // 4
D_CONF = D_BRANCH
CONF_K = 31
D_GMLP = D_BRANCH
GMLP_HEADS = 4
GMLP_HEAD_DIM = D_GMLP // GMLP_HEADS
CHUNK = 128
D_SCONV = D_BRANCH
SCONV_K = 3
D_POOL = D_BRANCH
POOL_WINDOWS = (2, 4, 8, 16)
POOL_GROUPS = 4
POOL_GROUP_DIM = D_POOL // POOL_GROUPS
POOL_PAST = 15
D_FF = 11 * D_MODEL // 4
FFN_K = 3
EPS = 1e-6
IN_SIZES = (D_CONF, D_CONF, D_GMLP, D_GMLP, D_SCONV, D_SCONV, D_SCONV, D_POOL, N_BRANCH * D_MODEL)
IN_WIDTH = 2 * D_CONF + 2 * D_GMLP + 3 * D_SCONV + D_POOL + N_BRANCH * D_MODEL

kernel_name = 'hybrid_conv_gmlp_pool_decoder_step'


def rmsnorm(x, g):
    xf = x.astype(jnp.float32)
    y = xf * lax.rsqrt(jnp.mean(xf * xf, axis=-1, keepdims=True) + EPS)
    return y.astype(x.dtype) * g.astype(x.dtype)


def layernorm(x, g, b):
    xf = x.astype(jnp.float32)
    mu = jnp.mean(xf, axis=-1, keepdims=True)
    var = jnp.mean(jnp.square(xf - mu), axis=-1, keepdims=True)
    y = (xf - mu) * lax.rsqrt(var + EPS)
    return y.astype(x.dtype) * g.astype(x.dtype) + b.astype(x.dtype)


def causal_dwconv(x, past, w):
    K, C = w.shape
    xp = jnp.concatenate([past.astype(x.dtype), x], axis=1)
    y = lax.conv_general_dilated(xp, w[:, None, :].astype(x.dtype), window_strides=(1,),
                                 padding='VALID', dimension_numbers=('NWC', 'WIO', 'NWC'),
                                 feature_group_count=C)
    return y, xp[:, -(K - 1):]


def chunk_spatial_gate(u, v, ws, bias):
    B, T, _ = v.shape
    L = min(T, CHUNK)
    n = T // L
    vh = v.reshape(B, n, L, GMLP_HEADS, GMLP_HEAD_DIM)
    w = jnp.tril(ws[:, :L, :L]).astype(v.dtype)
    mixed = jnp.einsum('hts,bcshd->bcthd', w, vh) + bias[:, :L].T[:, :, None].astype(v.dtype)
    return u * mixed.reshape(B, T, D_GMLP)


def multiscale_pool(p, past, start_pos):
    T = p.shape[1]
    xp = jnp.concatenate([past.astype(p.dtype), p], axis=1)
    xf = xp.astype(jnp.float32)
    cs = jnp.concatenate([jnp.zeros_like(xf[:, :1]), jnp.cumsum(xf, axis=1)], axis=1)
    pos = start_pos + jnp.arange(T, dtype=jnp.int32)
    outs = []
    for g, win in enumerate(POOL_WINDOWS):
        c0, c1 = g * POOL_GROUP_DIM, (g + 1) * POOL_GROUP_DIM
        hi = cs[:, POOL_PAST + 1:POOL_PAST + 1 + T, c0:c1]
        lo = cs[:, POOL_PAST + 1 - win:POOL_PAST + 1 - win + T, c0:c1]
        cnt = jnp.minimum(win, pos + 1).astype(jnp.float32)
        outs.append((hi - lo) / cnt[None, :, None])
    y = jnp.concatenate(outs, axis=-1) - xf[:, POOL_PAST:]
    return y.astype(p.dtype), xp[:, -POOL_PAST:]


def trunk(x, start_pos, past_conf, past_sconv, past_pool, past_ffn,
          norm_mix, w_in, conf_dw, conf_ln_g, conf_ln_b, gmlp_ln_g, gmlp_ln_b, gmlp_ws, gmlp_b,
          sconv_dw, pool_w, pool_scale, w_branch, w_o, norm_ffn, ffn_up, ffn_dw, ffn_down, norm_final):
    B, T, _ = x.shape
    split_points = np.cumsum(IN_SIZES)[:-1].tolist()
    s_conf, s_sconv, s_pool, s_ffn, s_v = [], [], [], [], []
    for l in range(DEPTH):
        xn = rmsnorm(x, norm_mix[l])
        a_in, a_gate, g_u, g_v, c_b, c_c, c_h, p_in, gate_logits = jnp.split(xn @ w_in[l], split_points, axis=-1)
        a, st = causal_dwconv(a_in * jax.nn.sigmoid(a_gate), past_conf[l], conf_dw[l])
        s_conf.append(st)
        br_a = jax.nn.silu(layernorm(a, conf_ln_g[l], conf_ln_b[l])) @ w_branch[l, 0]
        u = jax.nn.gelu(g_u)
        v = layernorm(jax.nn.gelu(g_v), gmlp_ln_g[l], gmlp_ln_b[l])
        s_v.append(v)
        br_b = chunk_spatial_gate(u, v, gmlp_ws[l], gmlp_b[l]) @ w_branch[l, 1]
        z, st = causal_dwconv(c_c * c_h, past_sconv[l], sconv_dw[l])
        s_sconv.append(st)
        br_c = (c_b * z) @ w_branch[l, 2]
        pm, st = multiscale_pool(p_in, past_pool[l], start_pos)
        s_pool.append(st)
        pm = jnp.einsum('btgc,gce->btge', pm.reshape(B, T, POOL_GROUPS, POOL_GROUP_DIM),
                        pool_w[l]).reshape(B, T, D_POOL) * pool_scale[l]
        br_d = pm @ w_branch[l, 3]
        g = jax.nn.sigmoid(gate_logits).reshape(B, T, N_BRANCH, D_MODEL)
        merged = g[:, :, 0] * br_a + g[:, :, 1] * br_b + g[:, :, 2] * br_c + g[:, :, 3] * br_d
        x = x + merged @ w_o[l]
        h, st = causal_dwconv(rmsnorm(x, norm_ffn[l]) @ ffn_up[l], past_ffn[l], ffn_dw[l])
        s_ffn.append(st)
        h_g, h_v = jnp.split(h, 2, axis=-1)
        x = x + (jax.nn.silu(h_g) * h_v) @ ffn_down[l]
    y = rmsnorm(x, norm_final)
    return y, jnp.stack(s_conf), jnp.stack(s_sconv), jnp.stack(s_pool), jnp.stack(s_ffn), jnp.stack(s_v)


def setup_inputs(seed: int = 0) -> dict:
    key = jax.random.key(seed)
    ks = jax.random.split(key, 32)
    nrm = lambda k, shape, s: jax.random.normal(k, shape, jnp.float32) * s
    ones_n = lambda k, shape: 1.0 + 0.1 * jax.random.normal(k, shape, jnp.float32)
    return {
        'x_prompt': nrm(ks[0], (BATCH, SEQ, D_MODEL), 1.0),
        'x_sample': nrm(ks[1], (DEC_BATCH, DEC_SEQ, D_MODEL), 1.0),
        'state_conf_conv': nrm(ks[2], (DEPTH, DEC_BATCH, CONF_K - 1, D_CONF), 0.5),
        'state_sconv': nrm(ks[3], (DEPTH, DEC_BATCH, SCONV_K - 1, D_SCONV), 0.5),
        'state_pool': nrm(ks[4], (DEPTH, DEC_BATCH, POOL_PAST, D_POOL), 0.5),
        'state_ffn_conv': nrm(ks[5], (DEPTH, DEC_BATCH, FFN_K - 1, 2 * D_FF), 0.5),
        'norm_mix': ones_n(ks[6], (DEPTH, D_MODEL)),
        'w_in': nrm(ks[7], (DEPTH, D_MODEL, IN_WIDTH), D_MODEL ** -0.5),
        'conf_dw': nrm(ks[8], (DEPTH, CONF_K, D_CONF), CONF_K ** -0.5),
        'conf_ln_g': ones_n(ks[9], (DEPTH, D_CONF)),
        'conf_ln_b': nrm(ks[10], (DEPTH, D_CONF), 0.02),
        'gmlp_ln_g': ones_n(ks[11], (DEPTH, D_GMLP)),
        'gmlp_ln_b': nrm(ks[12], (DEPTH, D_GMLP), 0.02),
        'gmlp_ws': nrm(ks[13], (DEPTH, GMLP_HEADS, CHUNK, CHUNK), CHUNK ** -0.5),
        'gmlp_b': ones_n(ks[14], (DEPTH, GMLP_HEADS, CHUNK)),
        'sconv_dw': nrm(ks[15], (DEPTH, SCONV_K, D_SCONV), SCONV_K ** -0.5),
        'pool_w': nrm(ks[16], (DEPTH, POOL_GROUPS, POOL_GROUP_DIM, POOL_GROUP_DIM), POOL_GROUP_DIM ** -0.5),
        'pool_scale': ones_n(ks[17], (DEPTH, D_POOL)),
        'w_branch': nrm(ks[18], (DEPTH, N_BRANCH, D_BRANCH, D_MODEL), D_BRANCH ** -0.5),
        'w_o': nrm(ks[19], (DEPTH, D_MODEL, D_MODEL), D_MODEL ** -0.5),
        'norm_ffn': ones_n(ks[20], (DEPTH, D_MODEL)),
        'ffn_up': nrm(ks[21], (DEPTH, D_MODEL, 2 * D_FF), D_MODEL ** -0.5),
        'ffn_dw': nrm(ks[22], (DEPTH, FFN_K, 2 * D_FF), FFN_K ** -0.5),
        'ffn_down': nrm(ks[23], (DEPTH, D_FF, D_MODEL), D_FF ** -0.5),
        'norm_final': ones_n(ks[24], (D_MODEL,)),
    }


def reference(x_prompt, x_sample, state_conf_conv, state_sconv, state_pool, state_ffn_conv,
              norm_mix, w_in, conf_dw, conf_ln_g, conf_ln_b, gmlp_ln_g, gmlp_ln_b, gmlp_ws, gmlp_b,
              sconv_dw, pool_w, pool_scale, w_branch, w_o, norm_ffn, ffn_up, ffn_dw, ffn_down, norm_final):
    weights = (norm_mix, w_in, conf_dw, conf_ln_g, conf_ln_b, gmlp_ln_g, gmlp_ln_b, gmlp_ws, gmlp_b,
               sconv_dw, pool_w, pool_scale, w_branch, w_o, norm_ffn, ffn_up, ffn_dw, ffn_down, norm_final)
    dt = x_prompt.dtype
    zc = jnp.zeros((DEPTH, BATCH, CONF_K - 1, D_CONF), dt)
    zs = jnp.zeros((DEPTH, BATCH, SCONV_K - 1, D_SCONV), dt)
    zp = jnp.zeros((DEPTH, BATCH, POOL_PAST, D_POOL), dt)
    zf = jnp.zeros((DEPTH, BATCH, FFN_K - 1, 2 * D_FF), dt)
    y_prompt, conf_p, sconv_p, pool_p, ffn_p, _ = trunk(x_prompt, 0, zc, zs, zp, zf, *weights)
    y_sample, conf_s, sconv_s, pool_s, ffn_s, v_s = trunk(
        x_sample, PAST_LEN, state_conf_conv, state_sconv, state_pool, state_ffn_conv, *weights)
    return (y_prompt, y_sample, conf_p, conf_s, sconv_p, sconv_s, pool_p, pool_s, ffn_p, ffn_s, v_s)
```

```python
import functools

import jax
import jax.numpy as jnp
from jax import lax
from jax.experimental import pallas as pl
from jax.experimental.pallas import tpu as pltpu

D_MODEL = 2048
BATCH = 4
SEQ = 2048
DEPTH = 2
DEC_BATCH = 128
DEC_SEQ = 4
PAST_LEN = 16384
D_BR = 512
N_BRANCH = 4
CONF_K = 31
GMLP_HEADS = 4
HEAD_DIM = D_BR // GMLP_HEADS
CHUNK = 128
SCONV_K = 3
POOL_WINDOWS = (2, 4, 8, 16)
POOL_PAST = 15
D_FF = 5632
FFN_K = 3
EPS = 1e-6
GATE_COL0 = 8 * D_BR

TM = 1024
M_P = BATCH * SEQ
M_S = DEC_BATCH * DEC_SEQ
M_ALL = M_P + M_S
N_PT = M_P // TM
TILES_PER_SEQ = SEQ // TM
N_TILES = N_PT + 1

HALO_A = 32
HALO_P = 16
HALO_S = 8
CONV_R = 64

TN_G = 256
TN_O = 1024
TK_U = 512
TN_D = 512

VMEM_LIMIT = 60000 * 1024

f32 = jnp.float32
bf16 = jnp.bfloat16


def _cparams(n_axes):
    return pltpu.CompilerParams(
        dimension_semantics=("arbitrary",) * n_axes, vmem_limit_bytes=VMEM_LIMIT)


def _once(block_shape, index_map):
    return pl.BlockSpec(block_shape, index_map, pipeline_mode=pl.Buffered(1))


def _rms(x, g):
    return x * lax.rsqrt(jnp.mean(x * x, axis=-1, keepdims=True) + EPS) * g


def _layernorm(x, g, b):
    mu = jnp.mean(x, axis=-1, keepdims=True)
    d = x - mu
    var = jnp.mean(d * d, axis=-1, keepdims=True)
    return d * lax.rsqrt(var + EPS) * g + b


def _sigmoid(x):
    return 1.0 / (1.0 + jnp.exp(-x))


def _silu(x):
    return x * _sigmoid(x)


def _gelu_tanh(x):
    c = 0.7978845608028654
    return 0.5 * x * (1.0 + jnp.tanh(c * (x + 0.044715 * (x * x * x))))


def _cast_rows(w_ref, wb_ref, col0=0, rows_per_step=256):
    k, n = w_ref.shape

    def body(c, carry):
        r = pl.multiple_of(c * rows_per_step, rows_per_step)
        wb_ref[pl.ds(r, rows_per_step), col0:col0 + n] = (
            w_ref[pl.ds(r, rows_per_step), :].astype(bf16))
        return carry

    lax.fori_loop(0, k // rows_per_step, body, 0)


def _dot(a, b):
    return jnp.dot(a, b, preferred_element_type=f32)


def _norm_rows(src_ref, g, dst_refs_fn, rows, chunk=256):
    def body(c, carry):
        r = pl.multiple_of(c * chunk, chunk)
        dst_refs_fn(r, chunk, src_ref[pl.ds(r, chunk), :])
        return carry

    lax.fori_loop(0, rows // chunk, body, 0)


def _norm0_kernel(xp_ref, xs_ref, g_ref, x_ref, xn_ref):
    i = pl.program_id(0)
    g = g_ref[...]

    @pl.when(i < N_PT)
    def _():
        def put(r, n, x):
            x_ref[pl.ds(r, n), :] = x
            xn_ref[pl.ds(r, n), :] = _rms(x, g).astype(bf16)
        _norm_rows(xp_ref, g, put, TM)

    @pl.when(i == N_PT)
    def _():
        for t in range(DEC_SEQ):
            x = xs_ref[:, t * D_MODEL:(t + 1) * D_MODEL]
            x_ref[t * DEC_BATCH:(t + 1) * DEC_BATCH, :] = x
            xn_ref[t * DEC_BATCH:(t + 1) * DEC_BATCH, :] = _rms(x, g).astype(bf16)


def _norm0(x_prompt, x_sample, norm_w, l):
    xp = x_prompt.reshape(M_P, D_MODEL)
    xs = x_sample.reshape(DEC_BATCH, DEC_SEQ * D_MODEL)
    return pl.pallas_call(
        _norm0_kernel,
        grid=(N_TILES,),
        in_specs=[
            pl.BlockSpec((TM, D_MODEL), lambda i: (jnp.minimum(i, N_PT - 1), 0)),
            pl.BlockSpec((DEC_BATCH, DEC_SEQ * D_MODEL), lambda i: (0, 0)),
            pl.BlockSpec((None, 1, D_MODEL), lambda i: (l, 0, 0)),
        ],
        out_specs=[
            pl.BlockSpec((TM, D_MODEL), lambda i: (i, 0)),
            pl.BlockSpec((TM, D_MODEL), lambda i: (i, 0)),
        ],
        out_shape=[
            jax.ShapeDtypeStruct((M_ALL, D_MODEL), f32),
            jax.ShapeDtypeStruct((M_ALL, D_MODEL), bf16),
        ],
        compiler_params=_cparams(1),
        name="norm0",
    )(xp, xs, norm_w.reshape(DEPTH, 1, D_MODEL))


def _norm_kernel(x_ref, g_ref, xn_ref):
    i = pl.program_id(0)
    g = g_ref[...]

    def put(r, n, x):
        xn_ref[pl.ds(r, n), :] = _rms(x, g).astype(bf16)

    @pl.when(i < N_PT)
    def _():
        _norm_rows(x_ref, g, put, TM)

    @pl.when(i == N_PT)
    def _():
        _norm_rows(x_ref, g, put, M_S)


def _norm(x, norm_w, l):
    return pl.pallas_call(
        _norm_kernel,
        grid=(N_TILES,),
        in_specs=[
            pl.BlockSpec((TM, D_MODEL), lambda i: (i, 0)),
            pl.BlockSpec((None, 1, D_MODEL), lambda i: (l, 0, 0)),
        ],
        out_specs=pl.BlockSpec((TM, D_MODEL), lambda i: (i, 0)),
        out_shape=jax.ShapeDtypeStruct((M_ALL, D_MODEL), bf16),
        compiler_params=_cparams(1),
        name="norm",
    )(x, norm_w.reshape(DEPTH, 1, D_MODEL))


def _final_norm_kernel(x_ref, g_ref, yp_ref, ys_ref):
    i = pl.program_id(0)
    g = g_ref[...]

    @pl.when(i < N_PT)
    def _():
        def put(r, n, x):
            yp_ref[pl.ds(r, n), :] = _rms(x, g)
        _norm_rows(x_ref, g, put, TM)

    @pl.when(i == N_PT)
    def _():
        for t in range(DEC_SEQ):
            x = x_ref[t * DEC_BATCH:(t + 1) * DEC_BATCH, :]
            ys_ref[:, t * D_MODEL:(t + 1) * D_MODEL] = _rms(x, g)


def _final_norm(x, norm_w):
    yp, ys = pl.pallas_call(
        _final_norm_kernel,
        grid=(N_TILES,),
        in_specs=[
            pl.BlockSpec((TM, D_MODEL), lambda i: (i, 0)),
            pl.BlockSpec((1, D_MODEL), lambda i: (0, 0)),
        ],
        out_specs=[
            pl.BlockSpec((TM, D_MODEL), lambda i: (jnp.minimum(i, N_PT - 1), 0)),
            pl.BlockSpec((DEC_BATCH, DEC_SEQ * D_MODEL), lambda i: (0, 0)),
        ],
        out_shape=[
            jax.ShapeDtypeStruct((M_P, D_MODEL), f32),
            jax.ShapeDtypeStruct((DEC_BATCH, DEC_SEQ * D_MODEL), f32),
        ],
        compiler_params=_cparams(1),
        name="final_norm",
    )(x, norm_w.reshape(1, D_MODEL))
    return (yp.reshape(BATCH, SEQ, D_MODEL),
            ys.reshape(DEC_BATCH, DEC_SEQ, D_MODEL))


def _conv31_chunk(ext_ref, base, dw_ref):
    y = None
    for r in range(8):
        rows = CONV_R if r == 0 else CONV_R + 8
        z = None
        for a in range(5):
            j = 8 * a + r - 2
            if 0 <= j < CONF_K:
                term = dw_ref[pl.ds(j, 1), :] * ext_ref[pl.ds(base + 8 * a, rows), :]
                z = term if z is None else z + term
        zr = z[r:r + CONV_R]
        y = zr if y is None else y + zr
    return y


def _stage_a_kernel(xn_ref, w_ref, dw_ref, lng_ref, lnb_ref, hist_ref,
                    act_ref, stp_ref, sts_ref, wb_ref, ext_ref):
    i = pl.program_id(0)

    @pl.when(i == 0)
    def _():
        _cast_rows(w_ref, wb_ref)

    lng = lng_ref[...]
    lnb = lnb_ref[...]

    @pl.when(i < N_PT)
    def _():
        tis = i % TILES_PER_SEQ

        @pl.when(tis == 0)
        def _():
            ext_ref[0:HALO_A, :] = jnp.zeros((HALO_A, D_BR), f32)

        p = _dot(xn_ref[...], wb_ref[...])
        ext_ref[HALO_A:HALO_A + TM, :] = p[:, :D_BR] * _sigmoid(p[:, D_BR:])

        def chunk(c, carry):
            base = pl.multiple_of(c * CONV_R, CONV_R)
            y = _conv31_chunk(ext_ref, base, dw_ref)
            act_ref[pl.ds(base, CONV_R), :] = _silu(_layernorm(y, lng, lnb)).astype(bf16)
            return carry

        lax.fori_loop(0, TM // CONV_R, chunk, 0)

        @pl.when(tis == TILES_PER_SEQ - 1)
        def _():
            stp_ref[i // TILES_PER_SEQ] = ext_ref[HALO_A + TM - (CONF_K - 1):HALO_A + TM, :]

        ext_ref[0:HALO_A, :] = ext_ref[TM:TM + HALO_A, :]

    @pl.when(i == N_PT)
    def _():
        p = _dot(xn_ref[0:M_S, :], wb_ref[...])
        glu = p[:, :D_BR] * _sigmoid(p[:, D_BR:])
        npast = CONF_K - 1
        rb = 32
        for t in range(DEC_SEQ):
            for q in range(DEC_BATCH // rb):
                y = None
                for j in range(CONF_K):
                    s = t + j
                    if s < npast:
                        slab = hist_ref[q * rb:(q + 1) * rb, s * D_BR:(s + 1) * D_BR]
                    else:
                        r0 = (s - npast) * DEC_BATCH + q * rb
                        slab = glu[r0:r0 + rb]
                    term = dw_ref[pl.ds(j, 1), :] * slab
                    y = term if y is None else y + term
                r0 = t * DEC_BATCH + q * rb
                act_ref[r0:r0 + rb, :] = _silu(_layernorm(y, lng, lnb)).astype(bf16)
        keep = npast - DEC_SEQ
        sts_ref[:, 0:keep * D_BR] = hist_ref[:, DEC_SEQ * D_BR:npast * D_BR]
        for t in range(DEC_SEQ):
            sts_ref[:, (keep + t) * D_BR:(keep + t + 1) * D_BR] = (
                glu[t * DEC_BATCH:(t + 1) * DEC_BATCH])


def _stage_a(xn, w_in, conf_dw, ln_g, ln_b, state, l):
    npast = CONF_K - 1
    hist = state.reshape(DEPTH, DEC_BATCH, npast * D_BR)
    act, stp, sts = pl.pallas_call(
        _stage_a_kernel,
        grid=(N_TILES,),
        in_specs=[
            pl.BlockSpec((TM, D_MODEL), lambda i: (i, 0)),
            _once((None, D_MODEL, 2 * D_BR), lambda i: (l, 0, 0)),
            pl.BlockSpec((None, CONF_K, D_BR), lambda i: (l, 0, 0)),
            pl.BlockSpec((None, 1, D_BR), lambda i: (l, 0, 0)),
            pl.BlockSpec((None, 1, D_BR), lambda i: (l, 0, 0)),
            _once((None, DEC_BATCH, npast * D_BR), lambda i: (l, 0, 0)),
        ],
        out_specs=[
            pl.BlockSpec((TM, D_BR), lambda i: (i, 0)),
            pl.BlockSpec((BATCH, npast, D_BR), lambda i: (0, 0, 0)),
            pl.BlockSpec((DEC_BATCH, npast * D_BR), lambda i: (0, 0)),
        ],
        out_shape=[
            jax.ShapeDtypeStruct((M_ALL, D_BR), bf16),
            jax.ShapeDtypeStruct((BATCH, npast, D_BR), f32),
            jax.ShapeDtypeStruct((DEC_BATCH, npast * D_BR), f32),
        ],
        scratch_shapes=[
            pltpu.VMEM((D_MODEL, 2 * D_BR), bf16),
            pltpu.VMEM((HALO_A + TM, D_BR), f32),
        ],
        compiler_params=_cparams(1),
        name="branch_a",
    )(xn, w_in, conf_dw, ln_g.reshape(DEPTH, 1, D_BR), ln_b.reshape(DEPTH, 1, D_BR), hist)
    return act, stp, sts.reshape(DEC_BATCH, npast, D_BR)


def _stage_b_kernel(xn_ref, w_ref, lng_ref, lnb_ref, ws_ref, bt_ref, wss_ref, bss_ref,
                    act_ref, v_ref, wb_ref, tril_ref, bias_ref):
    i = pl.program_id(0)

    @pl.when(i == 0)
    def _():
        _cast_rows(w_ref, wb_ref)
        row = lax.broadcasted_iota(jnp.int32, (CHUNK, CHUNK), 0)
        col = lax.broadcasted_iota(jnp.int32, (CHUNK, CHUNK), 1)
        for h in range(GMLP_HEADS):
            tril_ref[h] = jnp.where(row >= col, ws_ref[h], 0.0).astype(bf16)
            bias_ref[h] = jnp.broadcast_to(bt_ref[:, h:h + 1], (CHUNK, HEAD_DIM))

    lng = lng_ref[...]
    lnb = lnb_ref[...]

    @pl.when(i < N_PT)
    def _():
        p = _dot(xn_ref[...], wb_ref[...])
        for c in range(TM // CHUNK):
            rows = slice(c * CHUNK, (c + 1) * CHUNK)
            u = _gelu_tanh(p[rows, :D_BR])
            v = _layernorm(_gelu_tanh(p[rows, D_BR:]), lng, lnb).astype(bf16)
            for h in range(GMLP_HEADS):
                lanes = slice(h * HEAD_DIM, (h + 1) * HEAD_DIM)
                mixed = _dot(tril_ref[h], v[:, lanes]) + bias_ref[h]
                act_ref[rows, lanes] = (u[:, lanes] * mixed).astype(bf16)

    @pl.when(i == N_PT)
    def _():
        p = _dot(xn_ref[0:M_S, :], wb_ref[...])
        u = _gelu_tanh(p[:, :D_BR])
        v = _layernorm(_gelu_tanh(p[:, D_BR:]), lng, lnb)
        for t in range(DEC_SEQ):
            v_ref[:, t * D_BR:(t + 1) * D_BR] = v[t * DEC_BATCH:(t + 1) * DEC_BATCH]
        for t in range(DEC_SEQ):
            rows = slice(t * DEC_BATCH, (t + 1) * DEC_BATCH)
            for h in range(GMLP_HEADS):
                lanes = slice(h * HEAD_DIM, (h + 1) * HEAD_DIM)
                mixed = jnp.full((DEC_BATCH, HEAD_DIM), bss_ref[h * DEC_SEQ + t], f32)
                for s in range(t + 1):
                    coef = wss_ref[(h * DEC_SEQ + t) * DEC_SEQ + s]
                    mixed = mixed + coef * v[s * DEC_BATCH:(s + 1) * DEC_BATCH, lanes]
                act_ref[rows, lanes] = (u[rows, lanes] * mixed).astype(bf16)


def _stage_b(xn, w_in, ln_g, ln_b, gmlp_ws, gmlp_b, l):
    bias_t = jnp.swapaxes(gmlp_b, 1, 2)
    ws_small = gmlp_ws[l, :, :DEC_SEQ, :DEC_SEQ].reshape(-1)
    b_small = gmlp_b[l, :, :DEC_SEQ].reshape(-1)
    act, v = pl.pallas_call(
        _stage_b_kernel,
        grid=(N_TILES,),
        in_specs=[
            pl.BlockSpec((TM, D_MODEL), lambda i: (i, 0)),
            _once((None, D_MODEL, 2 * D_BR), lambda i: (l, 0, 1)),
            pl.BlockSpec((None, 1, D_BR), lambda i: (l, 0, 0)),
            pl.BlockSpec((None, 1, D_BR), lambda i: (l, 0, 0)),
            pl.BlockSpec((None, GMLP_HEADS, CHUNK, CHUNK), lambda i: (l, 0, 0, 0)),
            pl.BlockSpec((None, CHUNK, GMLP_HEADS), lambda i: (l, 0, 0)),
            pl.BlockSpec(memory_space=pltpu.SMEM),
            pl.BlockSpec(memory_space=pltpu.SMEM),
        ],
        out_specs=[
            pl.BlockSpec((TM, D_BR), lambda i: (i, 0)),
            pl.BlockSpec((DEC_BATCH, DEC_SEQ * D_BR), lambda i: (0, 0)),
        ],
        out_shape=[
            jax.ShapeDtypeStruct((M_ALL, D_BR), bf16),
            jax.ShapeDtypeStruct((DEC_BATCH, DEC_SEQ * D_BR), f32),
        ],
        scratch_shapes=[
            pltpu.VMEM((D_MODEL, 2 * D_BR), bf16),
            pltpu.VMEM((GMLP_HEADS, CHUNK, CHUNK), bf16),
            pltpu.VMEM((GMLP_HEADS, CHUNK, HEAD_DIM), f32),
        ],
        compiler_params=_cparams(1),
        name="branch_b",
    )(xn, w_in, ln_g.reshape(DEPTH, 1, D_BR), ln_b.reshape(DEPTH, 1, D_BR),
      gmlp_ws, bias_t, ws_small, b_small)
    return act, v.reshape(DEC_BATCH, DEC_SEQ, D_BR)


def _stage_c_kernel(xn_ref, w0_ref, w1_ref, w2_ref, dw_ref, hist_ref,
                    act_ref, stp_ref, sts_ref, wb_ref, ext_ref):
    i = pl.program_id(0)

    @pl.when(i == 0)
    def _():
        _cast_rows(w0_ref, wb_ref, 0)
        _cast_rows(w1_ref, wb_ref, D_BR)
        _cast_rows(w2_ref, wb_ref, 2 * D_BR)

    w0 = dw_ref[0:1, :]
    w1 = dw_ref[1:2, :]
    w2 = dw_ref[2:3, :]

    @pl.when(i < N_PT)
    def _():
        tis = i % TILES_PER_SEQ

        @pl.when(tis == 0)
        def _():
            ext_ref[0:HALO_S, :] = jnp.zeros((HALO_S, D_BR), f32)

        p = _dot(xn_ref[...], wb_ref[...])
        ext_ref[HALO_S:HALO_S + TM, :] = p[:, D_BR:2 * D_BR] * p[:, 2 * D_BR:]
        z = (w0 * ext_ref[HALO_S - 2:HALO_S - 2 + TM, :]
             + w1 * ext_ref[HALO_S - 1:HALO_S - 1 + TM, :]
             + w2 * ext_ref[HALO_S:HALO_S + TM, :])
        act_ref[...] = (p[:, :D_BR] * z).astype(bf16)

        @pl.when(tis == TILES_PER_SEQ - 1)
        def _():
            stp_ref[i // TILES_PER_SEQ] = ext_ref[HALO_S + TM - (SCONV_K - 1):HALO_S + TM, :]

        ext_ref[0:HALO_S, :] = ext_ref[TM:TM + HALO_S, :]

    @pl.when(i == N_PT)
    def _():
        p = _dot(xn_ref[0:M_S, :], wb_ref[...])
        s = p[:, D_BR:2 * D_BR] * p[:, 2 * D_BR:]
        xp = [hist_ref[:, 0:D_BR], hist_ref[:, D_BR:2 * D_BR]]
        xp += [s[t * DEC_BATCH:(t + 1) * DEC_BATCH] for t in range(DEC_SEQ)]
        for t in range(DEC_SEQ):
            z = w0 * xp[t] + w1 * xp[t + 1] + w2 * xp[t + 2]
            rows = slice(t * DEC_BATCH, (t + 1) * DEC_BATCH)
            act_ref[rows, :] = (p[rows, :D_BR] * z).astype(bf16)
        sts_ref[:, 0:D_BR] = xp[DEC_SEQ]
        sts_ref[:, D_BR:2 * D_BR] = xp[DEC_SEQ + 1]


def _stage_c(xn, w_in, sconv_dw, state, l):
    npast = SCONV_K - 1
    hist = state.reshape(DEPTH, DEC_BATCH, npast * D_BR)
    wspec = lambda cb: _once((None, D_MODEL, D_BR), lambda i: (l, 0, cb))
    act, stp, sts = pl.pallas_call(
        _stage_c_kernel,
        grid=(N_TILES,),
        in_specs=[
            pl.BlockSpec((TM, D_MODEL), lambda i: (i, 0)),
            wspec(4), wspec(5), wspec(6),
            pl.BlockSpec((None, SCONV_K, D_BR), lambda i: (l, 0, 0)),
            pl.BlockSpec((None, DEC_BATCH, npast * D_BR), lambda i: (l, 0, 0)),
        ],
        out_specs=[
            pl.BlockSpec((TM, D_BR), lambda i: (i, 0)),
            pl.BlockSpec((BATCH, npast, D_BR), lambda i: (0, 0, 0)),
            pl.BlockSpec((DEC_BATCH, npast * D_BR), lambda i: (0, 0)),
        ],
        out_shape=[
            jax.ShapeDtypeStruct((M_ALL, D_BR), bf16),
            jax.ShapeDtypeStruct((BATCH, npast, D_BR), f32),
            jax.ShapeDtypeStruct((DEC_BATCH, npast * D_BR), f32),
        ],
        scratch_shapes=[
            pltpu.VMEM((D_MODEL, 3 * D_BR), bf16),
            pltpu.VMEM((HALO_S + TM, D_BR), f32),
        ],
        compiler_params=_cparams(1),
        name="branch_c",
    )(xn, w_in, w_in, w_in, sconv_dw, hist)
    return act, stp, sts.reshape(DEC_BATCH, npast, D_BR)


def _stage_d_kernel(xn_ref, w_ref, pw_ref, psc_ref, hist_ref,
                    act_ref, stp_ref, sts_ref, wb_ref, pwb_ref, ext_ref):
    i = pl.program_id(0)
    gdim = D_BR // len(POOL_WINDOWS)

    @pl.when(i == 0)
    def _():
        _cast_rows(w_ref, wb_ref)
        pwb_ref[...] = pw_ref[...].astype(bf16)

    @pl.when(i < N_PT)
    def _():
        tis = i % TILES_PER_SEQ

        @pl.when(tis == 0)
        def _():
            ext_ref[0:HALO_P, :] = jnp.zeros((HALO_P, D_BR), f32)

        ext_ref[HALO_P:HALO_P + TM, :] = _dot(xn_ref[...], wb_ref[...])
        pos1 = tis * TM + 1 + lax.broadcasted_iota(jnp.int32, (TM, gdim), 0)
        for g, win in enumerate(POOL_WINDOWS):
            lanes = slice(g * gdim, (g + 1) * gdim)
            cur = ext_ref[HALO_P:HALO_P + TM, lanes]
            tot = cur
            for k in range(1, win):
                tot = tot + ext_ref[HALO_P - k:HALO_P - k + TM, lanes]
            cnt = jnp.minimum(pos1, win).astype(f32)
            pm = (tot / cnt - cur).astype(bf16)
            act_ref[:, lanes] = (_dot(pm, pwb_ref[g]) * psc_ref[:, lanes]).astype(bf16)

        @pl.when(tis == TILES_PER_SEQ - 1)
        def _():
            stp_ref[i // TILES_PER_SEQ] = ext_ref[HALO_P + TM - POOL_PAST:HALO_P + TM, :]

        ext_ref[0:HALO_P, :] = ext_ref[TM:TM + HALO_P, :]

    @pl.when(i == N_PT)
    def _():
        p = _dot(xn_ref[0:M_S, :], wb_ref[...])

        def slab(s, lanes):
            if s < POOL_PAST:
                return hist_ref[:, s * D_BR + lanes.start:s * D_BR + lanes.stop]
            r0 = (s - POOL_PAST) * DEC_BATCH
            return p[r0:r0 + DEC_BATCH, lanes]

        for t in range(DEC_SEQ):
            rows = slice(t * DEC_BATCH, (t + 1) * DEC_BATCH)
            for g, win in enumerate(POOL_WINDOWS):
                lanes = slice(g * gdim, (g + 1) * gdim)
                cur = slab(POOL_PAST + t, lanes)
                tot = cur
                for k in range(1, win):
                    tot = tot + slab(POOL_PAST + t - k, lanes)
                cnt = float(min(win, PAST_LEN + t + 1))
                pm = (tot / cnt - cur).astype(bf16)
                act_ref[rows, lanes] = (_dot(pm, pwb_ref[g]) * psc_ref[:, lanes]).astype(bf16)
        keep = POOL_PAST - DEC_SEQ
        sts_ref[:, 0:keep * D_BR] = hist_ref[:, DEC_SEQ * D_BR:POOL_PAST * D_BR]
        for t in range(DEC_SEQ):
            sts_ref[:, (keep + t) * D_BR:(keep + t + 1) * D_BR] = (
                p[t * DEC_BATCH:(t + 1) * DEC_BATCH])


def _stage_d(xn, w_in, pool_w, pool_scale, state, l):
    ngroup = len(POOL_WINDOWS)
    gdim = D_BR // ngroup
    hist = state.reshape(DEPTH, DEC_BATCH, POOL_PAST * D_BR)
    act, stp, sts = pl.pallas_call(
        _stage_d_kernel,
        grid=(N_TILES,),
        in_specs=[
            pl.BlockSpec((TM, D_MODEL), lambda i: (i, 0)),
            _once((None, D_MODEL, D_BR), lambda i: (l, 0, 7)),
            pl.BlockSpec((None, ngroup, gdim, gdim), lambda i: (l, 0, 0, 0)),
            pl.BlockSpec((None, 1, D_BR), lambda i: (l, 0, 0)),
            _once((None, DEC_BATCH, POOL_PAST * D_BR), lambda i: (l, 0, 0)),
        ],
        out_specs=[
            pl.BlockSpec((TM, D_BR), lambda i: (i, 0)),
            pl.BlockSpec((BATCH, POOL_PAST, D_BR), lambda i: (0, 0, 0)),
            pl.BlockSpec((DEC_BATCH, POOL_PAST * D_BR), lambda i: (0, 0)),
        ],
        out_shape=[
            jax.ShapeDtypeStruct((M_ALL, D_BR), bf16),
            jax.ShapeDtypeStruct((BATCH, POOL_PAST, D_BR), f32),
            jax.ShapeDtypeStruct((DEC_BATCH, POOL_PAST * D_BR), f32),
        ],
        scratch_shapes=[
            pltpu.VMEM((D_MODEL, D_BR), bf16),
            pltpu.VMEM((ngroup, gdim, gdim), bf16),
            pltpu.VMEM((HALO_P + TM, D_BR), f32),
        ],
        compiler_params=_cparams(1),
        name="branch_d",
    )(xn, w_in, pool_w, pool_scale.reshape(DEPTH, 1, D_BR), hist)
    return act, stp, sts.reshape(DEC_BATCH, POOL_PAST, D_BR)


def _stage_g_kernel(xn_ref, a0_ref, a1_ref, a2_ref, a3_ref,
                    g0_ref, g1_ref, g2_ref, g3_ref, wbr_ref,
                    out_ref, wgb_ref, wbb_ref):
    i = pl.program_id(1)

    @pl.when(i == 0)
    def _():
        for b, g_ref in enumerate((g0_ref, g1_ref, g2_ref, g3_ref)):
            _cast_rows(g_ref, wgb_ref, b * TN_G)
        wbb_ref[...] = wbr_ref[...].astype(bf16)

    def compute(rows):
        gate = _sigmoid(_dot(xn_ref[rows, :], wgb_ref[...]))
        merged = None
        for b, a_ref in enumerate((a0_ref, a1_ref, a2_ref, a3_ref)):
            term = gate[:, b * TN_G:(b + 1) * TN_G] * _dot(a_ref[rows, :], wbb_ref[b])
            merged = term if merged is None else merged + term
        out_ref[rows, :] = merged.astype(bf16)

    @pl.when(i < N_PT)
    def _():
        compute(slice(0, TM))

    @pl.when(i == N_PT)
    def _():
        compute(slice(0, M_S))


def _stage_g(xn, acts, w_in, w_branch, l):
    gate_blk0 = GATE_COL0 // TN_G
    per_branch = D_MODEL // TN_G
    gspec = lambda b: pl.BlockSpec(
        (None, D_MODEL, TN_G), lambda c, i: (l, 0, gate_blk0 + b * per_branch + c))
    aspec = pl.BlockSpec((TM, D_BR), lambda c, i: (i, 0))
    return pl.pallas_call(
        _stage_g_kernel,
        grid=(D_MODEL // TN_G, N_TILES),
        in_specs=[
            pl.BlockSpec((TM, D_MODEL), lambda c, i: (i, 0)),
            aspec, aspec, aspec, aspec,
            gspec(0), gspec(1), gspec(2), gspec(3),
            pl.BlockSpec((None, N_BRANCH, D_BR, TN_G), lambda c, i: (l, 0, 0, c)),
        ],
        out_specs=pl.BlockSpec((TM, TN_G), lambda c, i: (i, c)),
        out_shape=jax.ShapeDtypeStruct((M_ALL, D_MODEL), bf16),
        scratch_shapes=[
            pltpu.VMEM((D_MODEL, N_BRANCH * TN_G), bf16),
            pltpu.VMEM((N_BRANCH, D_BR, TN_G), bf16),
        ],
        compiler_params=_cparams(2),
        name="gate_merge",
    )(xn, *acts, w_in, w_in, w_in, w_in, w_branch)


def _stage_res_kernel(lhs_ref, w_ref, x_ref, o_ref, wb_ref):
    i = pl.program_id(1)

    @pl.when(i == 0)
    def _():
        _cast_rows(w_ref, wb_ref)

    @pl.when(i < N_PT)
    def _():
        o_ref[...] = x_ref[...] + _dot(lhs_ref[...], wb_ref[...])

    @pl.when(i == N_PT)
    def _():
        o_ref[0:M_S, :] = x_ref[0:M_S, :] + _dot(lhs_ref[0:M_S, :], wb_ref[...])


def _stage_res(lhs, w, x, l, tn, name, single_buffer_w):
    k = lhs.shape[1]
    wshape = (None, k, tn)
    wmap = lambda c, i: (l, 0, c)
    wspec = _once(wshape, wmap) if single_buffer_w else pl.BlockSpec(wshape, wmap)
    return pl.pallas_call(
        _stage_res_kernel,
        grid=(D_MODEL // tn, N_TILES),
        in_specs=[
            pl.BlockSpec((TM, k), lambda c, i: (i, 0)),
            wspec,
            pl.BlockSpec((TM, tn), lambda c, i: (i, c)),
        ],
        out_specs=pl.BlockSpec((TM, tn), lambda c, i: (i, c)),
        out_shape=jax.ShapeDtypeStruct((M_ALL, D_MODEL), f32),
        scratch_shapes=[pltpu.VMEM((k, tn), bf16)],
        compiler_params=_cparams(2),
        name=name,
    )(lhs, w, x)


def _stage_u_kernel(xn_ref, wg_ref, wv_ref, dwg_ref, dwv_ref,
                    hg0_ref, hg1_ref, hv0_ref, hv1_ref,
                    a_ref, stp_ref, sts_ref, wb_ref, ext_ref):
    i = pl.program_id(1)

    @pl.when(i == 0)
    def _():
        _cast_rows(wg_ref, wb_ref, 0)
        _cast_rows(wv_ref, wb_ref, TK_U)

    def taps(dw_ref):
        return dw_ref[0:1, :], dw_ref[1:2, :], dw_ref[2:3, :]

    @pl.when(i < N_PT)
    def _():
        tis = i % TILES_PER_SEQ

        @pl.when(tis == 0)
        def _():
            ext_ref[0:HALO_S, :] = jnp.zeros((HALO_S, 2 * TK_U), f32)

        ext_ref[HALO_S:HALO_S + TM, :] = _dot(xn_ref[...], wb_ref[...])

        def conv(dw_ref, lanes):
            w0, w1, w2 = taps(dw_ref)
            return (w0 * ext_ref[HALO_S - 2:HALO_S - 2 + TM, lanes]
                    + w1 * ext_ref[HALO_S - 1:HALO_S - 1 + TM, lanes]
                    + w2 * ext_ref[HALO_S:HALO_S + TM, lanes])

        hg = conv(dwg_ref, slice(0, TK_U))
        hv = conv(dwv_ref, slice(TK_U, 2 * TK_U))
        a_ref[...] = (_silu(hg) * hv).astype(bf16)

        @pl.when(tis == TILES_PER_SEQ - 1)
        def _():
            b = i // TILES_PER_SEQ
            last = slice(HALO_S + TM - (FFN_K - 1), HALO_S + TM)
            stp_ref[b, 0] = ext_ref[last, 0:TK_U]
            stp_ref[b, 1] = ext_ref[last, TK_U:2 * TK_U]

        ext_ref[0:HALO_S, :] = ext_ref[TM:TM + HALO_S, :]

    @pl.when(i == N_PT)
    def _():
        h = _dot(xn_ref[0:M_S, :], wb_ref[...])

        def gated_half(dw_ref, h0_ref, h1_ref, lanes):
            w0, w1, w2 = taps(dw_ref)
            xp = [h0_ref[...], h1_ref[...]]
            xp += [h[t * DEC_BATCH:(t + 1) * DEC_BATCH, lanes] for t in range(DEC_SEQ)]
            return [w0 * xp[t] + w1 * xp[t + 1] + w2 * xp[t + 2] for t in range(DEC_SEQ)]

        hg = gated_half(dwg_ref, hg0_ref, hg1_ref, slice(0, TK_U))
        hv = gated_half(dwv_ref, hv0_ref, hv1_ref, slice(TK_U, 2 * TK_U))
        for t in range(DEC_SEQ):
            a_ref[t * DEC_BATCH:(t + 1) * DEC_BATCH, :] = (_silu(hg[t]) * hv[t]).astype(bf16)
        for r in range(FFN_K - 1):
            t = DEC_SEQ - (FFN_K - 1) + r
            rows = slice(t * DEC_BATCH, (t + 1) * DEC_BATCH)
            sts_ref[r, 0] = h[rows, 0:TK_U]
            sts_ref[r, 1] = h[rows, TK_U:2 * TK_U]


def _stage_u(xn, ffn_up, ffn_dw, state, l):
    nk = D_FF // TK_U
    npast = FFN_K - 1
    hist = state.reshape(DEPTH, DEC_BATCH, npast * 2 * D_FF)
    wspec = lambda half: pl.BlockSpec(
        (None, D_MODEL, TK_U), lambda k, i: (l, 0, half * nk + k))
    dspec = lambda half: pl.BlockSpec(
        (None, FFN_K, TK_U), lambda k, i: (l, 0, half * nk + k))
    hspec = lambda row, half: pl.BlockSpec(
        (None, DEC_BATCH, TK_U), lambda k, i: (l, 0, (row * 2 + half) * nk + k))
    a, stp, sts = pl.pallas_call(
        _stage_u_kernel,
        grid=(nk, N_TILES),
        in_specs=[
            pl.BlockSpec((TM, D_MODEL), lambda k, i: (i, 0)),
            wspec(0), wspec(1), dspec(0), dspec(1),
            hspec(0, 0), hspec(1, 0), hspec(0, 1), hspec(1, 1),
        ],
        out_specs=[
            pl.BlockSpec((TM, TK_U), lambda k, i: (i, k)),
            pl.BlockSpec((BATCH, 2, npast, TK_U), lambda k, i: (0, 0, 0, k)),
            pl.BlockSpec((npast, 2, DEC_BATCH, TK_U), lambda k, i: (0, 0, 0, k)),
        ],
        out_shape=[
            jax.ShapeDtypeStruct((M_ALL, D_FF), bf16),
            jax.ShapeDtypeStruct((BATCH, 2, npast, D_FF), f32),
            jax.ShapeDtypeStruct((npast, 2, DEC_BATCH, D_FF), f32),
        ],
        scratch_shapes=[
            pltpu.VMEM((D_MODEL, 2 * TK_U), bf16),
            pltpu.VMEM((HALO_S + TM, 2 * TK_U), f32),
        ],
        compiler_params=_cparams(2),
        name="ffn_up",
    )(xn, ffn_up, ffn_up, ffn_dw, ffn_dw, hist, hist, hist, hist)
    stp = jnp.transpose(stp, (0, 2, 1, 3)).reshape(BATCH, npast, 2 * D_FF)
    sts = jnp.transpose(sts, (2, 0, 1, 3)).reshape(DEC_BATCH, npast, 2 * D_FF)
    return a, stp, sts


def kernel(x_prompt, x_sample, state_conf_conv, state_sconv, state_pool, state_ffn_conv,
           norm_mix, w_in, conf_dw, conf_ln_g, conf_ln_b, gmlp_ln_g, gmlp_ln_b, gmlp_ws,
           gmlp_b, sconv_dw, pool_w, pool_scale, w_branch, w_o, norm_ffn, ffn_up, ffn_dw,
           ffn_down, norm_final):
    conf_p, conf_s, sconv_p, sconv_s, pool_p, pool_s, ffn_p, ffn_s, v_s = ([] for _ in range(9))
    x = None
    for l in range(DEPTH):
        if l == 0:
            x, xn = _norm0(x_prompt, x_sample, norm_mix, l)
        else:
            xn = _norm(x, norm_mix, l)
        act_a, st_p, st_s = _stage_a(xn, w_in, conf_dw, conf_ln_g, conf_ln_b, state_conf_conv, l)
        conf_p.append(st_p)
        conf_s.append(st_s)
        act_b, v = _stage_b(xn, w_in, gmlp_ln_g, gmlp_ln_b, gmlp_ws, gmlp_b, l)
        v_s.append(v)
        act_c, st_p, st_s = _stage_c(xn, w_in, sconv_dw, state_sconv, l)
        sconv_p.append(st_p)
        sconv_s.append(st_s)
        act_d, st_p, st_s = _stage_d(xn, w_in, pool_w, pool_scale, state_pool, l)
        pool_p.append(st_p)
        pool_s.append(st_s)
        merged = _stage_g(xn, (act_a, act_b, act_c, act_d), w_in, w_branch, l)
        x = _stage_res(merged, w_o, x, l, TN_O, "out_proj", False)
        xn = _norm(x, norm_ffn, l)
        a, st_p, st_s = _stage_u(xn, ffn_up, ffn_dw, state_ffn_conv, l)
        ffn_p.append(st_p)
        ffn_s.append(st_s)
        x = _stage_res(a, ffn_down, x, l, TN_D, "ffn_down", True)
    y_prompt, y_sample = _final_norm(x, norm_final)
    st = jnp.stack
    return (y_prompt, y_sample, st(conf_p), st(conf_s), st(sconv_p), st(sconv_s),
            st(pool_p), st(pool_s), st(ffn_p), st(ffn_s), st(v_s))
```

```python
import jax
import jax.numpy as jnp
from jax import lax
from jax.experimental import pallas as pl
from jax.experimental.pallas import tpu as pltpu

D_MODEL = 2048
BATCH = 4
SEQ = 2048
DEPTH = 2
DEC_BATCH = 128
DEC_SEQ = 4
PAST_LEN = 16384
D_BR = 512
N_BRANCH = 4
CONF_K = 31
GMLP_HEADS = 4
HEAD_DIM = D_BR // GMLP_HEADS
CHUNK = 128
SCONV_K = 3
POOL_WINDOWS = (2, 4, 8, 16)
POOL_PAST = 15
D_FF = 5632
FFN_K = 3
EPS = 1e-6
GATE_COL0 = 8 * D_BR

TM = 1024
M_P = BATCH * SEQ
M_S = DEC_BATCH * DEC_SEQ
M_ALL = M_P + M_S
N_PT = M_P // TM
TILES_PER_SEQ = SEQ // TM
N_TILES = N_PT + 1

LANES = 128
HALO_A = 32
HALO_P = 16
HALO_S = 8
CONV_R = 128
EP_R = 128
LN_R = 64
U_R = 1024
U_BLK = 512

TN_G = 256
TN_O = 1024
TK_U = 512
TN_D = 512

VMEM_LIMIT = 60000 * 1024

f32 = jnp.float32
bf16 = jnp.bfloat16


def _cparams(n_axes):
    return pltpu.CompilerParams(
        dimension_semantics=("arbitrary",) * n_axes, vmem_limit_bytes=VMEM_LIMIT)


def _once(block_shape, index_map):
    return pl.BlockSpec(block_shape, index_map, pipeline_mode=pl.Buffered(1))


def _rms(x, g):
    return x * lax.rsqrt(jnp.mean(x * x, axis=-1, keepdims=True) + EPS) * g


def _layernorm(x, g, b):
    mu = jnp.mean(x, axis=-1, keepdims=True)
    d = x - mu
    var = jnp.mean(d * d, axis=-1, keepdims=True)
    return d * lax.rsqrt(var + EPS) * g + b


def _sigmoid(x):
    return 1.0 / (1.0 + jnp.exp(-x))


def _silu(x):
    return x * _sigmoid(x)


def _gelu_tanh(x):
    c = 0.7978845608028654
    return 0.5 * x * (1.0 + jnp.tanh(c * (x + 0.044715 * (x * x * x))))


def _cast_blocks(moves, wb_ref, rows_per_step=256):
    k = wb_ref.shape[0]

    def body(c, carry):
        r = pl.multiple_of(c * rows_per_step, rows_per_step)
        for w_ref, src, dst, n in moves:
            wb_ref[pl.ds(r, rows_per_step), dst:dst + n] = (
                w_ref[pl.ds(r, rows_per_step), src:src + n].astype(bf16))
        return carry

    lax.fori_loop(0, k // rows_per_step, body, 0)


def _cast_rows(w_ref, wb_ref, col0=0):
    _cast_blocks([(w_ref, 0, col0, w_ref.shape[1])], wb_ref)


def _dot(a, b):
    return jnp.dot(a, b, preferred_element_type=f32)


def _blk(k):
    return slice(k * LANES, (k + 1) * LANES)


def _norm_rows(src_ref, put, rows, chunk=256):
    def body(c, carry):
        r = pl.multiple_of(c * chunk, chunk)
        put(r, chunk, src_ref[pl.ds(r, chunk), :])
        return carry

    lax.fori_loop(0, rows // chunk, body, 0)


def _norm0_kernel(xp_ref, xs_ref, g_ref, x_ref, xn_ref):
    i = pl.program_id(0)
    g = g_ref[...]

    @pl.when(i < N_PT)
    def _():
        def put(r, n, x):
            x_ref[pl.ds(r, n), :] = x
            xn_ref[pl.ds(r, n), :] = _rms(x, g).astype(bf16)
        _norm_rows(xp_ref, put, TM)

    @pl.when(i == N_PT)
    def _():
        for t in range(DEC_SEQ):
            x = xs_ref[:, t * D_MODEL:(t + 1) * D_MODEL]
            x_ref[t * DEC_BATCH:(t + 1) * DEC_BATCH, :] = x
            xn_ref[t * DEC_BATCH:(t + 1) * DEC_BATCH, :] = _rms(x, g).astype(bf16)


def _norm0(x_prompt, x_sample, norm_w, l):
    xp = x_prompt.reshape(M_P, D_MODEL)
    xs = x_sample.reshape(DEC_BATCH, DEC_SEQ * D_MODEL)
    return pl.pallas_call(
        _norm0_kernel,
        grid=(N_TILES,),
        in_specs=[
            pl.BlockSpec((TM, D_MODEL), lambda i: (jnp.minimum(i, N_PT - 1), 0)),
            pl.BlockSpec((DEC_BATCH, DEC_SEQ * D_MODEL), lambda i: (0, 0)),
            pl.BlockSpec((None, 1, D_MODEL), lambda i: (l, 0, 0)),
        ],
        out_specs=[
            pl.BlockSpec((TM, D_MODEL), lambda i: (i, 0)),
            pl.BlockSpec((TM, D_MODEL), lambda i: (i, 0)),
        ],
        out_shape=[
            jax.ShapeDtypeStruct((M_ALL, D_MODEL), f32),
            jax.ShapeDtypeStruct((M_ALL, D_MODEL), bf16),
        ],
        compiler_params=_cparams(1),
        name="norm0",
    )(xp, xs, norm_w.reshape(DEPTH, 1, D_MODEL))


def _norm_kernel(x_ref, g_ref, xn_ref):
    i = pl.program_id(0)
    g = g_ref[...]

    def put(r, n, x):
        xn_ref[pl.ds(r, n), :] = _rms(x, g).astype(bf16)

    @pl.when(i < N_PT)
    def _():
        _norm_rows(x_ref, put, TM)

    @pl.when(i == N_PT)
    def _():
        _norm_rows(x_ref, put, M_S)


def _norm(x, norm_w, l):
    return pl.pallas_call(
        _norm_kernel,
        grid=(N_TILES,),
        in_specs=[
            pl.BlockSpec((TM, D_MODEL), lambda i: (i, 0)),
            pl.BlockSpec((None, 1, D_MODEL), lambda i: (l, 0, 0)),
        ],
        out_specs=pl.BlockSpec((TM, D_MODEL), lambda i: (i, 0)),
        out_shape=jax.ShapeDtypeStruct((M_ALL, D_MODEL), bf16),
        compiler_params=_cparams(1),
        name="norm",
    )(x, norm_w.reshape(DEPTH, 1, D_MODEL))


def _final_norm_kernel(x_ref, g_ref, yp_ref, ys_ref):
    i = pl.program_id(0)
    g = g_ref[...]

    @pl.when(i < N_PT)
    def _():
        def put(r, n, x):
            yp_ref[pl.ds(r, n), :] = _rms(x, g)
        _norm_rows(x_ref, put, TM)

    @pl.when(i == N_PT)
    def _():
        for t in range(DEC_SEQ):
            x = x_ref[t * DEC_BATCH:(t + 1) * DEC_BATCH, :]
            ys_ref[:, t * D_MODEL:(t + 1) * D_MODEL] = _rms(x, g)


def _final_norm(x, norm_w):
    yp, ys = pl.pallas_call(
        _final_norm_kernel,
        grid=(N_TILES,),
        in_specs=[
            pl.BlockSpec((TM, D_MODEL), lambda i: (i, 0)),
            pl.BlockSpec((1, D_MODEL), lambda i: (0, 0)),
        ],
        out_specs=[
            pl.BlockSpec((TM, D_MODEL), lambda i: (jnp.minimum(i, N_PT - 1), 0)),
            pl.BlockSpec((DEC_BATCH, DEC_SEQ * D_MODEL), lambda i: (0, 0)),
        ],
        out_shape=[
            jax.ShapeDtypeStruct((M_P, D_MODEL), f32),
            jax.ShapeDtypeStruct((DEC_BATCH, DEC_SEQ * D_MODEL), f32),
        ],
        compiler_params=_cparams(1),
        name="final_norm",
    )(x, norm_w.reshape(1, D_MODEL))
    return (yp.reshape(BATCH, SEQ, D_MODEL),
            ys.reshape(DEC_BATCH, DEC_SEQ, D_MODEL))


def _conv31_chunk(ext_ref, base, dw_ref, lanes):
    y = None
    for r in range(8):
        rows = CONV_R if r == 0 else CONV_R + 8
        z = None
        for a in range(5):
            j = 8 * a + r - 2
            if 0 <= j < CONF_K:
                term = dw_ref[pl.ds(j, 1), lanes] * ext_ref[pl.ds(base + 8 * a, rows), lanes]
                z = term if z is None else z + term
        zr = z[r:r + CONV_R]
        y = zr if y is None else y + zr
    return y


def _stage_a_kernel(xn_ref, w_ref, dw_ref, lng_ref, lnb_ref, hist_ref,
                    act_ref, stp_ref, sts_ref, wb_ref, ext_ref, y_ref):
    i = pl.program_id(0)
    nblk = D_BR // LANES
    npast = CONF_K - 1

    @pl.when(i == 0)
    def _():
        moves = []
        for k in range(nblk):
            moves.append((w_ref, k * LANES, 2 * k * LANES, LANES))
            moves.append((w_ref, D_BR + k * LANES, (2 * k + 1) * LANES, LANES))
        _cast_blocks(moves, wb_ref)
        ext_ref[0:HALO_A, :] = jnp.zeros((HALO_A, D_BR), f32)

    lng = lng_ref[...]
    lnb = lnb_ref[...]

    def glu_block(p, k, rows):
        return p[rows, _blk(2 * k)] * _sigmoid(p[rows, _blk(2 * k + 1)])

    @pl.when(i < N_PT)
    def _():
        first = lax.rem(i, TILES_PER_SEQ) == 0
        ext_ref[0:HALO_A, :] = jnp.where(first, 0.0, ext_ref[0:HALO_A, :])
        p = _dot(xn_ref[...], wb_ref[...])
        for k in range(nblk):
            for c in range(TM // CONV_R):
                rows = slice(c * CONV_R, (c + 1) * CONV_R)
                ext_ref[HALO_A + c * CONV_R:HALO_A + (c + 1) * CONV_R, _blk(k)] = (
                    glu_block(p, k, rows))
            for c in range(TM // CONV_R):
                y_ref[c * CONV_R:(c + 1) * CONV_R, _blk(k)] = (
                    _conv31_chunk(ext_ref, c * CONV_R, dw_ref, _blk(k)))
        for c in range(TM // LN_R):
            rows = slice(c * LN_R, (c + 1) * LN_R)
            act_ref[rows, :] = _silu(_layernorm(y_ref[rows, :], lng, lnb)).astype(bf16)
        stp_ref[i // TILES_PER_SEQ] = ext_ref[HALO_A + TM - npast:HALO_A + TM, :]
        ext_ref[0:HALO_A, :] = ext_ref[TM:TM + HALO_A, :]

    @pl.when(i == N_PT)
    def _():
        p = _dot(xn_ref[0:M_S, :], wb_ref[...])
        glu = jnp.concatenate([glu_block(p, k, slice(0, M_S)) for k in range(nblk)], axis=1)
        rb = 32
        for t in range(DEC_SEQ):
            for q in range(DEC_BATCH // rb):
                y = None
                for j in range(CONF_K):
                    s = t + j
                    if s < npast:
                        slab = hist_ref[q * rb:(q + 1) * rb, s * D_BR:(s + 1) * D_BR]
                    else:
                        r0 = (s - npast) * DEC_BATCH + q * rb
                        slab = glu[r0:r0 + rb]
                    term = dw_ref[pl.ds(j, 1), :] * slab
                    y = term if y is None else y + term
                r0 = t * DEC_BATCH + q * rb
                act_ref[r0:r0 + rb, :] = _silu(_layernorm(y, lng, lnb)).astype(bf16)
        keep = npast - DEC_SEQ
        sts_ref[:, 0:keep * D_BR] = hist_ref[:, DEC_SEQ * D_BR:npast * D_BR]
        for t in range(DEC_SEQ):
            sts_ref[:, (keep + t) * D_BR:(keep + t + 1) * D_BR] = (
                glu[t * DEC_BATCH:(t + 1) * DEC_BATCH])


def _stage_a(xn, w_in, conf_dw, ln_g, ln_b, state, l):
    npast = CONF_K - 1
    hist = state.reshape(DEPTH, DEC_BATCH, npast * D_BR)
    act, stp, sts = pl.pallas_call(
        _stage_a_kernel,
        grid=(N_TILES,),
        in_specs=[
            pl.BlockSpec((TM, D_MODEL), lambda i: (i, 0)),
            _once((None, D_MODEL, 2 * D_BR), lambda i: (l, 0, 0)),
            pl.BlockSpec((None, CONF_K, D_BR), lambda i: (l, 0, 0)),
            pl.BlockSpec((None, 1, D_BR), lambda i: (l, 0, 0)),
            pl.BlockSpec((None, 1, D_BR), lambda i: (l, 0, 0)),
            _once((None, DEC_BATCH, npast * D_BR), lambda i: (l, 0, 0)),
        ],
        out_specs=[
            pl.BlockSpec((TM, D_BR), lambda i: (i, 0)),
            pl.BlockSpec((BATCH, npast, D_BR), lambda i: (0, 0, 0)),
            pl.BlockSpec((DEC_BATCH, npast * D_BR), lambda i: (0, 0)),
        ],
        out_shape=[
            jax.ShapeDtypeStruct((M_ALL, D_BR), bf16),
            jax.ShapeDtypeStruct((BATCH, npast, D_BR), f32),
            jax.ShapeDtypeStruct((DEC_BATCH, npast * D_BR), f32),
        ],
        scratch_shapes=[
            pltpu.VMEM((D_MODEL, 2 * D_BR), bf16),
            pltpu.VMEM((HALO_A + TM, D_BR), f32),
            pltpu.VMEM((TM, D_BR), f32),
        ],
        compiler_params=_cparams(1),
        name="branch_a",
    )(xn, w_in, conf_dw, ln_g.reshape(DEPTH, 1, D_BR), ln_b.reshape(DEPTH, 1, D_BR), hist)
    return act, stp, sts.reshape(DEC_BATCH, npast, D_BR)


def _stage_b_kernel(xn_ref, w_ref, lng_ref, lnb_ref, ws_ref, bt_ref, wss_ref, bss_ref,
                    act_ref, v_ref, wb_ref, tril_ref, bias_ref):
    i = pl.program_id(0)

    @pl.when(i == 0)
    def _():
        _cast_rows(w_ref, wb_ref)
        row = lax.broadcasted_iota(jnp.int32, (CHUNK, CHUNK), 0)
        col = lax.broadcasted_iota(jnp.int32, (CHUNK, CHUNK), 1)
        for h in range(GMLP_HEADS):
            tril_ref[h] = jnp.where(row >= col, ws_ref[h], 0.0).astype(bf16)
            bias_ref[h] = jnp.broadcast_to(bt_ref[:, h:h + 1], (CHUNK, HEAD_DIM))

    lng = lng_ref[...]
    lnb = lnb_ref[...]

    @pl.when(i < N_PT)
    def _():
        p = _dot(xn_ref[...], wb_ref[...])
        for c in range(TM // CHUNK):
            rows = slice(c * CHUNK, (c + 1) * CHUNK)
            u = _gelu_tanh(p[rows, :D_BR])
            v = _layernorm(_gelu_tanh(p[rows, D_BR:]), lng, lnb).astype(bf16)
            for h in range(GMLP_HEADS):
                lanes = slice(h * HEAD_DIM, (h + 1) * HEAD_DIM)
                mixed = _dot(tril_ref[h], v[:, lanes]) + bias_ref[h]
                act_ref[rows, lanes] = (u[:, lanes] * mixed).astype(bf16)

    @pl.when(i == N_PT)
    def _():
        p = _dot(xn_ref[0:M_S, :], wb_ref[...])
        u = _gelu_tanh(p[:, :D_BR])
        v = _layernorm(_gelu_tanh(p[:, D_BR:]), lng, lnb)
        for t in range(DEC_SEQ):
            v_ref[:, t * D_BR:(t + 1) * D_BR] = v[t * DEC_BATCH:(t + 1) * DEC_BATCH]
        for t in range(DEC_SEQ):
            rows = slice(t * DEC_BATCH, (t + 1) * DEC_BATCH)
            for h in range(GMLP_HEADS):
                lanes = slice(h * HEAD_DIM, (h + 1) * HEAD_DIM)
                mixed = jnp.full((DEC_BATCH, HEAD_DIM), bss_ref[h * DEC_SEQ + t], f32)
                for s in range(t + 1):
                    coef = wss_ref[(h * DEC_SEQ + t) * DEC_SEQ + s]
                    mixed = mixed + coef * v[s * DEC_BATCH:(s + 1) * DEC_BATCH, lanes]
                act_ref[rows, lanes] = (u[rows, lanes] * mixed).astype(bf16)


def _stage_b(xn, w_in, ln_g, ln_b, gmlp_ws, gmlp_b, l):
    bias_t = jnp.swapaxes(gmlp_b, 1, 2)
    ws_small = gmlp_ws[l, :, :DEC_SEQ, :DEC_SEQ].reshape(-1)
    b_small = gmlp_b[l, :, :DEC_SEQ].reshape(-1)
    act, v = pl.pallas_call(
        _stage_b_kernel,
        grid=(N_TILES,),
        in_specs=[
            pl.BlockSpec((TM, D_MODEL), lambda i: (i, 0)),
            _once((None, D_MODEL, 2 * D_BR), lambda i: (l, 0, 1)),
            pl.BlockSpec((None, 1, D_BR), lambda i: (l, 0, 0)),
            pl.BlockSpec((None, 1, D_BR), lambda i: (l, 0, 0)),
            pl.BlockSpec((None, GMLP_HEADS, CHUNK, CHUNK), lambda i: (l, 0, 0, 0)),
            pl.BlockSpec((None, CHUNK, GMLP_HEADS), lambda i: (l, 0, 0)),
            pl.BlockSpec(memory_space=pltpu.SMEM),
            pl.BlockSpec(memory_space=pltpu.SMEM),
        ],
        out_specs=[
            pl.BlockSpec((TM, D_BR), lambda i: (i, 0)),
            pl.BlockSpec((DEC_BATCH, DEC_SEQ * D_BR), lambda i: (0, 0)),
        ],
        out_shape=[
            jax.ShapeDtypeStruct((M_ALL, D_BR), bf16),
            jax.ShapeDtypeStruct((DEC_BATCH, DEC_SEQ * D_BR), f32),
        ],
        scratch_shapes=[
            pltpu.VMEM((D_MODEL, 2 * D_BR), bf16),
            pltpu.VMEM((GMLP_HEADS, CHUNK, CHUNK), bf16),
            pltpu.VMEM((GMLP_HEADS, CHUNK, HEAD_DIM), f32),
        ],
        compiler_params=_cparams(1),
        name="branch_b",
    )(xn, w_in, ln_g.reshape(DEPTH, 1, D_BR), ln_b.reshape(DEPTH, 1, D_BR),
      gmlp_ws, bias_t, ws_small, b_small)
    return act, v.reshape(DEC_BATCH, DEC_SEQ, D_BR)


def _stage_c_kernel(xn_ref, w0_ref, w1_ref, w2_ref, dw_ref, hist_ref,
                    act_ref, stp_ref, sts_ref, wb_ref, ext_ref):
    i = pl.program_id(0)
    nblk = D_BR // LANES

    @pl.when(i == 0)
    def _():
        moves = []
        for k in range(nblk):
            for part, w_ref in enumerate((w0_ref, w1_ref, w2_ref)):
                moves.append((w_ref, k * LANES, (3 * k + part) * LANES, LANES))
        _cast_blocks(moves, wb_ref)
        ext_ref[0:HALO_S, :] = jnp.zeros((HALO_S, D_BR), f32)

    def taps(k):
        return dw_ref[0:1, _blk(k)], dw_ref[1:2, _blk(k)], dw_ref[2:3, _blk(k)]

    @pl.when(i < N_PT)
    def _():
        first = lax.rem(i, TILES_PER_SEQ) == 0
        ext_ref[0:HALO_S, :] = jnp.where(first, 0.0, ext_ref[0:HALO_S, :])
        p = _dot(xn_ref[...], wb_ref[...])
        for k in range(nblk):
            w0, w1, w2 = taps(k)
            for c in range(TM // EP_R):
                rows = slice(c * EP_R, (c + 1) * EP_R)
                r0 = HALO_S + c * EP_R
                ext_ref[r0:r0 + EP_R, _blk(k)] = p[rows, _blk(3 * k + 1)] * p[rows, _blk(3 * k + 2)]
                z = (w0 * ext_ref[r0 - 2:r0 - 2 + EP_R, _blk(k)]
                     + w1 * ext_ref[r0 - 1:r0 - 1 + EP_R, _blk(k)]
                     + w2 * ext_ref[r0:r0 + EP_R, _blk(k)])
                act_ref[rows, _blk(k)] = (p[rows, _blk(3 * k)] * z).astype(bf16)
        stp_ref[i // TILES_PER_SEQ] = ext_ref[HALO_S + TM - (SCONV_K - 1):HALO_S + TM, :]
        ext_ref[0:HALO_S, :] = ext_ref[TM:TM + HALO_S, :]

    @pl.when(i == N_PT)
    def _():
        p = _dot(xn_ref[0:M_S, :], wb_ref[...])
        for k in range(nblk):
            w0, w1, w2 = taps(k)
            s = p[:, _blk(3 * k + 1)] * p[:, _blk(3 * k + 2)]
            xp = [hist_ref[:, _blk(k)], hist_ref[:, D_BR + k * LANES:D_BR + (k + 1) * LANES]]
            xp += [s[t * DEC_BATCH:(t + 1) * DEC_BATCH] for t in range(DEC_SEQ)]
            for t in range(DEC_SEQ):
                z = w0 * xp[t] + w1 * xp[t + 1] + w2 * xp[t + 2]
                rows = slice(t * DEC_BATCH, (t + 1) * DEC_BATCH)
                act_ref[rows, _blk(k)] = (p[rows, _blk(3 * k)] * z).astype(bf16)
            sts_ref[:, _blk(k)] = xp[DEC_SEQ]
            sts_ref[:, D_BR + k * LANES:D_BR + (k + 1) * LANES] = xp[DEC_SEQ + 1]


def _stage_c(xn, w_in, sconv_dw, state, l):
    npast = SCONV_K - 1
    hist = state.reshape(DEPTH, DEC_BATCH, npast * D_BR)
    wspec = lambda cb: _once((None, D_MODEL, D_BR), lambda i: (l, 0, cb))
    act, stp, sts = pl.pallas_call(
        _stage_c_kernel,
        grid=(N_TILES,),
        in_specs=[
            pl.BlockSpec((TM, D_MODEL), lambda i: (i, 0)),
            wspec(4), wspec(5), wspec(6),
            pl.BlockSpec((None, SCONV_K, D_BR), lambda i: (l, 0, 0)),
            pl.BlockSpec((None, DEC_BATCH, npast * D_BR), lambda i: (l, 0, 0)),
        ],
        out_specs=[
            pl.BlockSpec((TM, D_BR), lambda i: (i, 0)),
            pl.BlockSpec((BATCH, npast, D_BR), lambda i: (0, 0, 0)),
            pl.BlockSpec((DEC_BATCH, npast * D_BR), lambda i: (0, 0)),
        ],
        out_shape=[
            jax.ShapeDtypeStruct((M_ALL, D_BR), bf16),
            jax.ShapeDtypeStruct((BATCH, npast, D_BR), f32),
            jax.ShapeDtypeStruct((DEC_BATCH, npast * D_BR), f32),
        ],
        scratch_shapes=[
            pltpu.VMEM((D_MODEL, 3 * D_BR), bf16),
            pltpu.VMEM((HALO_S + TM, D_BR), f32),
        ],
        compiler_params=_cparams(1),
        name="branch_c",
    )(xn, w_in, w_in, w_in, sconv_dw, hist)
    return act, stp, sts.reshape(DEC_BATCH, npast, D_BR)


def _stage_d_kernel(xn_ref, w_ref, pw_ref, psc_ref, hist_ref,
                    act_ref, stp_ref, sts_ref, wb_ref, pwb_ref, ext_ref):
    i = pl.program_id(0)
    gdim = D_BR // len(POOL_WINDOWS)

    @pl.when(i == 0)
    def _():
        _cast_rows(w_ref, wb_ref)
        pwb_ref[...] = pw_ref[...].astype(bf16)

    @pl.when(i < N_PT)
    def _():
        tis = i % TILES_PER_SEQ

        @pl.when(tis == 0)
        def _():
            ext_ref[0:HALO_P, :] = jnp.zeros((HALO_P, D_BR), f32)

        ext_ref[HALO_P:HALO_P + TM, :] = _dot(xn_ref[...], wb_ref[...])
        pos1 = tis * TM + 1 + lax.broadcasted_iota(jnp.int32, (TM, gdim), 0)
        for g, win in enumerate(POOL_WINDOWS):
            lanes = slice(g * gdim, (g + 1) * gdim)
            cur = ext_ref[HALO_P:HALO_P + TM, lanes]
            tot = cur
            for k in range(1, win):
                tot = tot + ext_ref[HALO_P - k:HALO_P - k + TM, lanes]
            cnt = jnp.minimum(pos1, win).astype(f32)
            pm = (tot / cnt - cur).astype(bf16)
            act_ref[:, lanes] = (_dot(pm, pwb_ref[g]) * psc_ref[:, lanes]).astype(bf16)

        @pl.when(tis == TILES_PER_SEQ - 1)
        def _():
            stp_ref[i // TILES_PER_SEQ] = ext_ref[HALO_P + TM - POOL_PAST:HALO_P + TM, :]

        ext_ref[0:HALO_P, :] = ext_ref[TM:TM + HALO_P, :]

    @pl.when(i == N_PT)
    def _():
        p = _dot(xn_ref[0:M_S, :], wb_ref[...])

        def slab(s, lanes):
            if s < POOL_PAST:
                return hist_ref[:, s * D_BR + lanes.start:s * D_BR + lanes.stop]
            r0 = (s - POOL_PAST) * DEC_BATCH
            return p[r0:r0 + DEC_BATCH, lanes]

        for t in range(DEC_SEQ):
            rows = slice(t * DEC_BATCH, (t + 1) * DEC_BATCH)
            for g, win in enumerate(POOL_WINDOWS):
                lanes = slice(g * gdim, (g + 1) * gdim)
                cur = slab(POOL_PAST + t, lanes)
                tot = cur
                for k in range(1, win):
                    tot = tot + slab(POOL_PAST + t - k, lanes)
                cnt = float(min(win, PAST_LEN + t + 1))
                pm = (tot / cnt - cur).astype(bf16)
                act_ref[rows, lanes] = (_dot(pm, pwb_ref[g]) * psc_ref[:, lanes]).astype(bf16)
        keep = POOL_PAST - DEC_SEQ
        sts_ref[:, 0:keep * D_BR] = hist_ref[:, DEC_SEQ * D_BR:POOL_PAST * D_BR]
        for t in range(DEC_SEQ):
            sts_ref[:, (keep + t) * D_BR:(keep + t + 1) * D_BR] = (
                p[t * DEC_BATCH:(t + 1) * DEC_BATCH])


def _stage_d(xn, w_in, pool_w, pool_scale, state, l):
    ngroup = len(POOL_WINDOWS)
    gdim = D_BR // ngroup
    hist = state.reshape(DEPTH, DEC_BATCH, POOL_PAST * D_BR)
    act, stp, sts = pl.pallas_call(
        _stage_d_kernel,
        grid=(N_TILES,),
        in_specs=[
            pl.BlockSpec((TM, D_MODEL), lambda i: (i, 0)),
            _once((None, D_MODEL, D_BR), lambda i: (l, 0, 7)),
            pl.BlockSpec((None, ngroup, gdim, gdim), lambda i: (l, 0, 0, 0)),
            pl.BlockSpec((None, 1, D_BR), lambda i: (l, 0, 0)),
            _once((None, DEC_BATCH, POOL_PAST * D_BR), lambda i: (l, 0, 0)),
        ],
        out_specs=[
            pl.BlockSpec((TM, D_BR), lambda i: (i, 0)),
            pl.BlockSpec((BATCH, POOL_PAST, D_BR), lambda i: (0, 0, 0)),
            pl.BlockSpec((DEC_BATCH, POOL_PAST * D_BR), lambda i: (0, 0)),
        ],
        out_shape=[
            jax.ShapeDtypeStruct((M_ALL, D_BR), bf16),
            jax.ShapeDtypeStruct((BATCH, POOL_PAST, D_BR), f32),
            jax.ShapeDtypeStruct((DEC_BATCH, POOL_PAST * D_BR), f32),
        ],
        scratch_shapes=[
            pltpu.VMEM((D_MODEL, D_BR), bf16),
            pltpu.VMEM((ngroup, gdim, gdim), bf16),
            pltpu.VMEM((HALO_P + TM, D_BR), f32),
        ],
        compiler_params=_cparams(1),
        name="branch_d",
    )(xn, w_in, pool_w, pool_scale.reshape(DEPTH, 1, D_BR), hist)
    return act, stp, sts.reshape(DEC_BATCH, POOL_PAST, D_BR)


def _stage_g_kernel(xn_ref, a0_ref, a1_ref, a2_ref, a3_ref,
                    g0_ref, g1_ref, g2_ref, g3_ref, wbr_ref,
                    out_ref, wgb_ref, wbb_ref):
    i = pl.program_id(1)

    @pl.when(i == 0)
    def _():
        for b, g_ref in enumerate((g0_ref, g1_ref, g2_ref, g3_ref)):
            _cast_rows(g_ref, wgb_ref, b * TN_G)
        wbb_ref[...] = wbr_ref[...].astype(bf16)

    def compute(rows):
        gate = _sigmoid(_dot(xn_ref[rows, :], wgb_ref[...]))
        merged = None
        for b, a_ref in enumerate((a0_ref, a1_ref, a2_ref, a3_ref)):
            term = gate[:, b * TN_G:(b + 1) * TN_G] * _dot(a_ref[rows, :], wbb_ref[b])
            merged = term if merged is None else merged + term
        out_ref[rows, :] = merged.astype(bf16)

    @pl.when(i < N_PT)
    def _():
        compute(slice(0, TM))

    @pl.when(i == N_PT)
    def _():
        compute(slice(0, M_S))


def _stage_g(xn, acts, w_in, w_branch, l):
    gate_blk0 = GATE_COL0 // TN_G
    per_branch = D_MODEL // TN_G
    gspec = lambda b: pl.BlockSpec(
        (None, D_MODEL, TN_G), lambda c, i: (l, 0, gate_blk0 + b * per_branch + c))
    aspec = pl.BlockSpec((TM, D_BR), lambda c, i: (i, 0))
    return pl.pallas_call(
        _stage_g_kernel,
        grid=(D_MODEL // TN_G, N_TILES),
        in_specs=[
            pl.BlockSpec((TM, D_MODEL), lambda c, i: (i, 0)),
            aspec, aspec, aspec, aspec,
            gspec(0), gspec(1), gspec(2), gspec(3),
            pl.BlockSpec((None, N_BRANCH, D_BR, TN_G), lambda c, i: (l, 0, 0, c)),
        ],
        out_specs=pl.BlockSpec((TM, TN_G), lambda c, i: (i, c)),
        out_shape=jax.ShapeDtypeStruct((M_ALL, D_MODEL), bf16),
        scratch_shapes=[
            pltpu.VMEM((D_MODEL, N_BRANCH * TN_G), bf16),
            pltpu.VMEM((N_BRANCH, D_BR, TN_G), bf16),
        ],
        compiler_params=_cparams(2),
        name="gate_merge",
    )(xn, *acts, w_in, w_in, w_in, w_in, w_branch)


def _stage_res_kernel(lhs_ref, w_ref, x_ref, o_ref, wb_ref):
    i = pl.program_id(1)

    @pl.when(i == 0)
    def _():
        _cast_rows(w_ref, wb_ref)

    @pl.when(i < N_PT)
    def _():
        o_ref[...] = x_ref[...] + _dot(lhs_ref[...], wb_ref[...])

    @pl.when(i == N_PT)
    def _():
        o_ref[0:M_S, :] = x_ref[0:M_S, :] + _dot(lhs_ref[0:M_S, :], wb_ref[...])


def _stage_res(lhs, w, x, l, tn, name, single_buffer_w):
    k = lhs.shape[1]
    wshape = (None, k, tn)
    wmap = lambda c, i: (l, 0, c)
    wspec = _once(wshape, wmap) if single_buffer_w else pl.BlockSpec(wshape, wmap)
    return pl.pallas_call(
        _stage_res_kernel,
        grid=(D_MODEL // tn, N_TILES),
        in_specs=[
            pl.BlockSpec((TM, k), lambda c, i: (i, 0)),
            wspec,
            pl.BlockSpec((TM, tn), lambda c, i: (i, c)),
        ],
        out_specs=pl.BlockSpec((TM, tn), lambda c, i: (i, c)),
        out_shape=jax.ShapeDtypeStruct((M_ALL, D_MODEL), f32),
        scratch_shapes=[pltpu.VMEM((k, tn), bf16)],
        compiler_params=_cparams(2),
        name=name,
    )(lhs, w, x)


def _stage_u_kernel(xn_ref, wg_ref, wv_ref, dwg_ref, dwv_ref,
                    hg0_ref, hg1_ref, hv0_ref, hv1_ref,
                    a_ref, stp_ref, sts_ref, wb_ref, ext_ref):
    i = pl.program_id(1)
    nblk = TK_U // U_BLK
    npast = FFN_K - 1

    def _blk(k):
        return slice(k * U_BLK, (k + 1) * U_BLK)

    @pl.when(i == 0)
    def _():
        moves = []
        for k in range(nblk):
            moves.append((wg_ref, k * U_BLK, 2 * k * U_BLK, U_BLK))
            moves.append((wv_ref, k * U_BLK, (2 * k + 1) * U_BLK, U_BLK))
        _cast_blocks(moves, wb_ref)
        ext_ref[0:HALO_S, :] = jnp.zeros((HALO_S, 2 * TK_U), f32)

    def taps(dw_ref, k):
        return dw_ref[0:1, _blk(k)], dw_ref[1:2, _blk(k)], dw_ref[2:3, _blk(k)]

    @pl.when(i < N_PT)
    def _():
        first = lax.rem(i, TILES_PER_SEQ) == 0
        ext_ref[0:HALO_S, :] = jnp.where(first, 0.0, ext_ref[0:HALO_S, :])
        ext_ref[HALO_S:HALO_S + TM, :] = _dot(xn_ref[...], wb_ref[...])
        for k in range(nblk):
            for c in range(TM // U_R):
                r0 = HALO_S + c * U_R

                def conv(dw_ref, lanes):
                    w0, w1, w2 = taps(dw_ref, k)
                    return (w0 * ext_ref[r0 - 2:r0 - 2 + U_R, lanes]
                            + w1 * ext_ref[r0 - 1:r0 - 1 + U_R, lanes]
                            + w2 * ext_ref[r0:r0 + U_R, lanes])

                hg = conv(dwg_ref, _blk(2 * k))
                hv = conv(dwv_ref, _blk(2 * k + 1))
                a_ref[c * U_R:(c + 1) * U_R, _blk(k)] = (_silu(hg) * hv).astype(bf16)
        b = i // TILES_PER_SEQ
        last = slice(HALO_S + TM - npast, HALO_S + TM)
        for k in range(nblk):
            stp_ref[b, 0, :, _blk(k)] = ext_ref[last, _blk(2 * k)]
            stp_ref[b, 1, :, _blk(k)] = ext_ref[last, _blk(2 * k + 1)]
        ext_ref[0:HALO_S, :] = ext_ref[TM:TM + HALO_S, :]

    @pl.when(i == N_PT)
    def _():
        h = _dot(xn_ref[0:M_S, :], wb_ref[...])

        def conv_slabs(dw_ref, h0_ref, h1_ref, k, lanes):
            w0, w1, w2 = taps(dw_ref, k)
            xp = [h0_ref[:, _blk(k)], h1_ref[:, _blk(k)]]
            xp += [h[t * DEC_BATCH:(t + 1) * DEC_BATCH, lanes] for t in range(DEC_SEQ)]
            return [w0 * xp[t] + w1 * xp[t + 1] + w2 * xp[t + 2] for t in range(DEC_SEQ)]

        for k in range(nblk):
            hg = conv_slabs(dwg_ref, hg0_ref, hg1_ref, k, _blk(2 * k))
            hv = conv_slabs(dwv_ref, hv0_ref, hv1_ref, k, _blk(2 * k + 1))
            for t in range(DEC_SEQ):
                a_ref[t * DEC_BATCH:(t + 1) * DEC_BATCH, _blk(k)] = (
                    _silu(hg[t]) * hv[t]).astype(bf16)
            for r in range(npast):
                t = DEC_SEQ - npast + r
                rows = slice(t * DEC_BATCH, (t + 1) * DEC_BATCH)
                sts_ref[r, 0, :, _blk(k)] = h[rows, _blk(2 * k)]
                sts_ref[r, 1, :, _blk(k)] = h[rows, _blk(2 * k + 1)]


def _stage_u(xn, ffn_up, ffn_dw, state, l):
    nk = D_FF // TK_U
    npast = FFN_K - 1
    hist = state.reshape(DEPTH, DEC_BATCH, npast * 2 * D_FF)
    wspec = lambda half: pl.BlockSpec(
        (None, D_MODEL, TK_U), lambda k, i: (l, 0, half * nk + k))
    dspec = lambda half: pl.BlockSpec(
        (None, FFN_K, TK_U), lambda k, i: (l, 0, half * nk + k))
    hspec = lambda row, half: pl.BlockSpec(
        (None, DEC_BATCH, TK_U), lambda k, i: (l, 0, (row * 2 + half) * nk + k))
    a, stp, sts = pl.pallas_call(
        _stage_u_kernel,
        grid=(nk, N_TILES),
        in_specs=[
            pl.BlockSpec((TM, D_MODEL), lambda k, i: (i, 0)),
            wspec(0), wspec(1), dspec(0), dspec(1),
            hspec(0, 0), hspec(1, 0), hspec(0, 1), hspec(1, 1),
        ],
        out_specs=[
            pl.BlockSpec((TM, TK_U), lambda k, i: (i, k)),
            pl.BlockSpec((BATCH, 2, npast, TK_U), lambda k, i: (0, 0, 0, k)),
            pl.BlockSpec((npast, 2, DEC_BATCH, TK_U), lambda k, i: (0, 0, 0, k)),
        ],
        out_shape=[
            jax.ShapeDtypeStruct((M_ALL, D_FF), bf16),
            jax.ShapeDtypeStruct((BATCH, 2, npast, D_FF), f32),
            jax.ShapeDtypeStruct((npast, 2, DEC_BATCH, D_FF), f32),
        ],
        scratch_shapes=[
            pltpu.VMEM((D_MODEL, 2 * TK_U), bf16),
            pltpu.VMEM((HALO_S + TM, 2 * TK_U), f32),
        ],
        compiler_params=_cparams(2),
        name="ffn_up",
    )(xn, ffn_up, ffn_up, ffn_dw, ffn_dw, hist, hist, hist, hist)
    stp = jnp.transpose(stp, (0, 2, 1, 3)).reshape(BATCH, npast, 2 * D_FF)
    sts = jnp.transpose(sts, (2, 0, 1, 3)).reshape(DEC_BATCH, npast, 2 * D_FF)
    return a, stp, sts


def kernel(x_prompt, x_sample, state_conf_conv, state_sconv, state_pool, state_ffn_conv,
           norm_mix, w_in, conf_dw, conf_ln_g, conf_ln_b, gmlp_ln_g, gmlp_ln_b, gmlp_ws,
           gmlp_b, sconv_dw, pool_w, pool_scale, w_branch, w_o, norm_ffn, ffn_up, ffn_dw,
           ffn_down, norm_final):
    conf_p, conf_s, sconv_p, sconv_s, pool_p, pool_s, ffn_p, ffn_s, v_s = ([] for _ in range(9))
    x = None
    for l in range(DEPTH):
        if l == 0:
            x, xn = _norm0(x_prompt, x_sample, norm_mix, l)
        else:
            xn = _norm(x, norm_mix, l)
        act_a, st_p, st_s = _stage_a(xn, w_in, conf_dw, conf_ln_g, conf_ln_b, state_conf_conv, l)
        conf_p.append(st_p)
        conf_s.append(st_s)
        act_b, v = _stage_b(xn, w_in, gmlp_ln_g, gmlp_ln_b, gmlp_ws, gmlp_b, l)
        v_s.append(v)
        act_c, st_p, st_s = _stage_c(xn, w_in, sconv_dw, state_sconv, l)
        sconv_p.append(st_p)
        sconv_s.append(st_s)
        act_d, st_p, st_s = _stage_d(xn, w_in, pool_w, pool_scale, state_pool, l)
        pool_p.append(st_p)
        pool_s.append(st_s)
        merged = _stage_g(xn, (act_a, act_b, act_c, act_d), w_in, w_branch, l)
        x = _stage_res(merged, w_o, x, l, TN_O, "out_proj", False)
        xn = _norm(x, norm_ffn, l)
        a, st_p, st_s = _stage_u(xn, ffn_up, ffn_dw, state_ffn_conv, l)
        ffn_p.append(st_p)
        ffn_s.append(st_s)
        x = _stage_res(a, ffn_down, x, l, TN_D, "ffn_down", True)
    y_prompt, y_sample = _final_norm(x, norm_final)
    st = jnp.stack
    return (y_prompt, y_sample, st(conf_p), st(conf_s), st(sconv_p), st(sconv_s),
            st(pool_p), st(pool_s), st(ffn_p), st(ffn_s), st(v_s))
```

```python
import functools

import jax
import jax.numpy as jnp
from jax import lax
from jax.experimental import pallas as pl
from jax.experimental.pallas import tpu as pltpu

D_MODEL = 2048
BATCH = 4
SEQ = 2048
DEPTH = 2
DEC_BATCH = 128
DEC_SEQ = 4
PAST_LEN = 16384
D_BR = 512
N_BRANCH = 4
CONF_K = 31
GMLP_HEADS = 4
HEAD_DIM = D_BR // GMLP_HEADS
CHUNK = 128
SCONV_K = 3
POOL_WINDOWS = (2, 4, 8, 16)
POOL_PAST = 15
D_FF = 5632
FFN_K = 3
EPS = 1e-6
GATE_COL0 = 8 * D_BR

TM = 1024
M_P = BATCH * SEQ
M_S = DEC_BATCH * DEC_SEQ
M_ALL = M_P + M_S
N_PT = M_P // TM
TILES_PER_SEQ = SEQ // TM
N_TILES = N_PT + 1

LANES = 128
HALO_A = 32
HALO_P = 16
HALO_S = 8
CONV_R = 128
EP_R = 128
LN_R = 64
U_R = 1024
U_BLK = 512

TN_G = 256
TM_O = 512
TK_U = 512
TN_D = 512

VMEM_LIMIT = 60000 * 1024

f32 = jnp.float32
bf16 = jnp.bfloat16


def _cparams(n_axes):
    return pltpu.CompilerParams(
        dimension_semantics=("arbitrary",) * n_axes, vmem_limit_bytes=VMEM_LIMIT)


def _once(block_shape, index_map):
    return pl.BlockSpec(block_shape, index_map, pipeline_mode=pl.Buffered(1))


def _rms(x, g):
    return x * lax.rsqrt(jnp.mean(x * x, axis=-1, keepdims=True) + EPS) * g


def _layernorm(x, g, b):
    mu = jnp.mean(x, axis=-1, keepdims=True)
    d = x - mu
    var = jnp.mean(d * d, axis=-1, keepdims=True)
    return d * lax.rsqrt(var + EPS) * g + b


def _sigmoid(x):
    return 1.0 / (1.0 + jnp.exp(-x))


def _silu(x):
    return x * _sigmoid(x)


def _gelu_tanh(x):
    c = 0.7978845608028654
    return 0.5 * x * (1.0 + jnp.tanh(c * (x + 0.044715 * (x * x * x))))


def _cast_blocks(moves, wb_ref, rows_per_step=256):
    k = wb_ref.shape[0]

    def body(c, carry):
        r = pl.multiple_of(c * rows_per_step, rows_per_step)
        for w_ref, src, dst, n in moves:
            wb_ref[pl.ds(r, rows_per_step), dst:dst + n] = (
                w_ref[pl.ds(r, rows_per_step), src:src + n].astype(bf16))
        return carry

    lax.fori_loop(0, k // rows_per_step, body, 0)


def _cast_rows(w_ref, wb_ref, col0=0):
    _cast_blocks([(w_ref, 0, col0, w_ref.shape[1])], wb_ref)


def _dot(a, b):
    return jnp.dot(a, b, preferred_element_type=f32)


def _blk(k):
    return slice(k * LANES, (k + 1) * LANES)


def _norm_rows(src_ref, put, rows, chunk=256):
    def body(c, carry):
        r = pl.multiple_of(c * chunk, chunk)
        put(r, chunk, src_ref[pl.ds(r, chunk), :])
        return carry

    lax.fori_loop(0, rows // chunk, body, 0)


def _norm0_kernel(xp_ref, xs_ref, g_ref, xn_ref):
    i = pl.program_id(0)
    g = g_ref[...]

    @pl.when(i < N_PT)
    def _():
        def put(r, n, x):
            xn_ref[pl.ds(r, n), :] = _rms(x, g).astype(bf16)
        _norm_rows(xp_ref, put, TM)

    @pl.when(i == N_PT)
    def _():
        for t in range(DEC_SEQ):
            x = xs_ref[:, t * D_MODEL:(t + 1) * D_MODEL]
            xn_ref[t * DEC_BATCH:(t + 1) * DEC_BATCH, :] = _rms(x, g).astype(bf16)


def _norm0(x_prompt, x_sample, norm_w, l):
    xp = x_prompt.reshape(M_P, D_MODEL)
    xs = x_sample.reshape(DEC_BATCH, DEC_SEQ * D_MODEL)
    return pl.pallas_call(
        _norm0_kernel,
        grid=(N_TILES,),
        in_specs=[
            pl.BlockSpec((TM, D_MODEL), lambda i: (jnp.minimum(i, N_PT - 1), 0)),
            pl.BlockSpec((DEC_BATCH, DEC_SEQ * D_MODEL), lambda i: (0, 0)),
            pl.BlockSpec((None, 1, D_MODEL), lambda i: (l, 0, 0)),
        ],
        out_specs=pl.BlockSpec((TM, D_MODEL), lambda i: (i, 0)),
        out_shape=jax.ShapeDtypeStruct((M_ALL, D_MODEL), bf16),
        compiler_params=_cparams(1),
        name="norm0",
    )(xp, xs, norm_w.reshape(DEPTH, 1, D_MODEL))


def _norm_kernel(x_ref, g_ref, xn_ref):
    i = pl.program_id(0)
    g = g_ref[...]

    def put(r, n, x):
        xn_ref[pl.ds(r, n), :] = _rms(x, g).astype(bf16)

    @pl.when(i < N_PT)
    def _():
        _norm_rows(x_ref, put, TM)

    @pl.when(i == N_PT)
    def _():
        _norm_rows(x_ref, put, M_S)


def _norm(x, norm_w, l):
    return pl.pallas_call(
        _norm_kernel,
        grid=(N_TILES,),
        in_specs=[
            pl.BlockSpec((TM, D_MODEL), lambda i: (i, 0)),
            pl.BlockSpec((None, 1, D_MODEL), lambda i: (l, 0, 0)),
        ],
        out_specs=pl.BlockSpec((TM, D_MODEL), lambda i: (i, 0)),
        out_shape=jax.ShapeDtypeStruct((M_ALL, D_MODEL), bf16),
        compiler_params=_cparams(1),
        name="norm",
    )(x, norm_w.reshape(DEPTH, 1, D_MODEL))


def _final_norm_kernel(x_ref, g_ref, yp_ref, ys_ref):
    i = pl.program_id(0)
    g = g_ref[...]

    @pl.when(i < N_PT)
    def _():
        def put(r, n, x):
            yp_ref[pl.ds(r, n), :] = _rms(x, g)
        _norm_rows(x_ref, put, TM)

    @pl.when(i == N_PT)
    def _():
        for t in range(DEC_SEQ):
            x = x_ref[t * DEC_BATCH:(t + 1) * DEC_BATCH, :]
            ys_ref[:, t * D_MODEL:(t + 1) * D_MODEL] = _rms(x, g)


def _final_norm(x, norm_w):
    yp, ys = pl.pallas_call(
        _final_norm_kernel,
        grid=(N_TILES,),
        in_specs=[
            pl.BlockSpec((TM, D_MODEL), lambda i: (i, 0)),
            pl.BlockSpec((1, D_MODEL), lambda i: (0, 0)),
        ],
        out_specs=[
            pl.BlockSpec((TM, D_MODEL), lambda i: (jnp.minimum(i, N_PT - 1), 0)),
            pl.BlockSpec((DEC_BATCH, DEC_SEQ * D_MODEL), lambda i: (0, 0)),
        ],
        out_shape=[
            jax.ShapeDtypeStruct((M_P, D_MODEL), f32),
            jax.ShapeDtypeStruct((DEC_BATCH, DEC_SEQ * D_MODEL), f32),
        ],
        compiler_params=_cparams(1),
        name="final_norm",
    )(x, norm_w.reshape(1, D_MODEL))
    return (yp.reshape(BATCH, SEQ, D_MODEL),
            ys.reshape(DEC_BATCH, DEC_SEQ, D_MODEL))


def _conv31_chunk(ext_ref, base, dw_ref, lanes):
    y = None
    for r in range(8):
        rows = CONV_R if r == 0 else CONV_R + 8
        z = None
        for a in range(5):
            j = 8 * a + r - 2
            if 0 <= j < CONF_K:
                term = dw_ref[pl.ds(j, 1), lanes] * ext_ref[pl.ds(base + 8 * a, rows), lanes]
                z = term if z is None else z + term
        zr = z[r:r + CONV_R]
        y = zr if y is None else y + zr
    return y


def _stage_a_kernel(xn_ref, w_ref, dw_ref, lng_ref, lnb_ref, hist_ref,
                    act_ref, stp_ref, sts_ref, wb_ref, ext_ref, y_ref):
    i = pl.program_id(0)
    nblk = D_BR // LANES
    npast = CONF_K - 1

    @pl.when(i == 0)
    def _():
        moves = []
        for k in range(nblk):
            moves.append((w_ref, k * LANES, 2 * k * LANES, LANES))
            moves.append((w_ref, D_BR + k * LANES, (2 * k + 1) * LANES, LANES))
        _cast_blocks(moves, wb_ref)
        ext_ref[0:HALO_A, :] = jnp.zeros((HALO_A, D_BR), f32)

    lng = lng_ref[...]
    lnb = lnb_ref[...]

    def glu_block(p, k, rows):
        return p[rows, _blk(2 * k)] * _sigmoid(p[rows, _blk(2 * k + 1)])

    @pl.when(i < N_PT)
    def _():
        first = lax.rem(i, TILES_PER_SEQ) == 0
        ext_ref[0:HALO_A, :] = jnp.where(first, 0.0, ext_ref[0:HALO_A, :])
        p = _dot(xn_ref[...], wb_ref[...])
        for k in range(nblk):
            for c in range(TM // CONV_R):
                rows = slice(c * CONV_R, (c + 1) * CONV_R)
                ext_ref[HALO_A + c * CONV_R:HALO_A + (c + 1) * CONV_R, _blk(k)] = (
                    glu_block(p, k, rows))
            for c in range(TM // CONV_R):
                y_ref[c * CONV_R:(c + 1) * CONV_R, _blk(k)] = (
                    _conv31_chunk(ext_ref, c * CONV_R, dw_ref, _blk(k)))
        for c in range(TM // LN_R):
            rows = slice(c * LN_R, (c + 1) * LN_R)
            act_ref[rows, :] = _silu(_layernorm(y_ref[rows, :], lng, lnb)).astype(bf16)
        stp_ref[i // TILES_PER_SEQ] = ext_ref[HALO_A + TM - npast:HALO_A + TM, :]
        ext_ref[0:HALO_A, :] = ext_ref[TM:TM + HALO_A, :]

    @pl.when(i == N_PT)
    def _():
        p = _dot(xn_ref[0:M_S, :], wb_ref[...])
        glu = jnp.concatenate([glu_block(p, k, slice(0, M_S)) for k in range(nblk)], axis=1)
        rb = 32
        for t in range(DEC_SEQ):
            for q in range(DEC_BATCH // rb):
                y = None
                for j in range(CONF_K):
                    s = t + j
                    if s < npast:
                        slab = hist_ref[q * rb:(q + 1) * rb, s * D_BR:(s + 1) * D_BR]
                    else:
                        r0 = (s - npast) * DEC_BATCH + q * rb
                        slab = glu[r0:r0 + rb]
                    term = dw_ref[pl.ds(j, 1), :] * slab
                    y = term if y is None else y + term
                r0 = t * DEC_BATCH + q * rb
                act_ref[r0:r0 + rb, :] = _silu(_layernorm(y, lng, lnb)).astype(bf16)
        keep = npast - DEC_SEQ
        sts_ref[:, 0:keep * D_BR] = hist_ref[:, DEC_SEQ * D_BR:npast * D_BR]
        for t in range(DEC_SEQ):
            sts_ref[:, (keep + t) * D_BR:(keep + t + 1) * D_BR] = (
                glu[t * DEC_BATCH:(t + 1) * DEC_BATCH])


def _stage_a(xn, w_in, conf_dw, ln_g, ln_b, state, l):
    npast = CONF_K - 1
    hist = state.reshape(DEPTH, DEC_BATCH, npast * D_BR)
    act, stp, sts = pl.pallas_call(
        _stage_a_kernel,
        grid=(N_TILES,),
        in_specs=[
            pl.BlockSpec((TM, D_MODEL), lambda i: (i, 0)),
            _once((None, D_MODEL, 2 * D_BR), lambda i: (l, 0, 0)),
            pl.BlockSpec((None, CONF_K, D_BR), lambda i: (l, 0, 0)),
            pl.BlockSpec((None, 1, D_BR), lambda i: (l, 0, 0)),
            pl.BlockSpec((None, 1, D_BR), lambda i: (l, 0, 0)),
            _once((None, DEC_BATCH, npast * D_BR), lambda i: (l, 0, 0)),
        ],
        out_specs=[
            pl.BlockSpec((TM, D_BR), lambda i: (i, 0)),
            pl.BlockSpec((BATCH, npast, D_BR), lambda i: (0, 0, 0)),
            pl.BlockSpec((DEC_BATCH, npast * D_BR), lambda i: (0, 0)),
        ],
        out_shape=[
            jax.ShapeDtypeStruct((M_ALL, D_BR), bf16),
            jax.ShapeDtypeStruct((BATCH, npast, D_BR), f32),
            jax.ShapeDtypeStruct((DEC_BATCH, npast * D_BR), f32),
        ],
        scratch_shapes=[
            pltpu.VMEM((D_MODEL, 2 * D_BR), bf16),
            pltpu.VMEM((HALO_A + TM, D_BR), f32),
            pltpu.VMEM((TM, D_BR), f32),
        ],
        compiler_params=_cparams(1),
        name="branch_a",
    )(xn, w_in, conf_dw, ln_g.reshape(DEPTH, 1, D_BR), ln_b.reshape(DEPTH, 1, D_BR), hist)
    return act, stp, sts.reshape(DEC_BATCH, npast, D_BR)


def _stage_b_kernel(xn_ref, w_ref, lng_ref, lnb_ref, ws_ref, bt_ref, wss_ref, bss_ref,
                    act_ref, v_ref, wb_ref, tril_ref, bias_ref):
    i = pl.program_id(0)

    @pl.when(i == 0)
    def _():
        _cast_rows(w_ref, wb_ref)
        row = lax.broadcasted_iota(jnp.int32, (CHUNK, CHUNK), 0)
        col = lax.broadcasted_iota(jnp.int32, (CHUNK, CHUNK), 1)
        for h in range(GMLP_HEADS):
            tril_ref[h] = jnp.where(row >= col, ws_ref[h], 0.0).astype(bf16)
            bias_ref[h] = jnp.broadcast_to(bt_ref[:, h:h + 1], (CHUNK, HEAD_DIM))

    lng = lng_ref[...]
    lnb = lnb_ref[...]

    @pl.when(i < N_PT)
    def _():
        p = _dot(xn_ref[...], wb_ref[...])
        for c in range(TM // CHUNK):
            rows = slice(c * CHUNK, (c + 1) * CHUNK)
            u = _gelu_tanh(p[rows, :D_BR])
            v = _layernorm(_gelu_tanh(p[rows, D_BR:]), lng, lnb).astype(bf16)
            for h in range(GMLP_HEADS):
                lanes = slice(h * HEAD_DIM, (h + 1) * HEAD_DIM)
                mixed = _dot(tril_ref[h], v[:, lanes]) + bias_ref[h]
                act_ref[rows, lanes] = (u[:, lanes] * mixed).astype(bf16)

    @pl.when(i == N_PT)
    def _():
        p = _dot(xn_ref[0:M_S, :], wb_ref[...])
        u = _gelu_tanh(p[:, :D_BR])
        v = _layernorm(_gelu_tanh(p[:, D_BR:]), lng, lnb)
        for t in range(DEC_SEQ):
            v_ref[:, t * D_BR:(t + 1) * D_BR] = v[t * DEC_BATCH:(t + 1) * DEC_BATCH]
        for t in range(DEC_SEQ):
            rows = slice(t * DEC_BATCH, (t + 1) * DEC_BATCH)
            for h in range(GMLP_HEADS):
                lanes = slice(h * HEAD_DIM, (h + 1) * HEAD_DIM)
                mixed = jnp.full((DEC_BATCH, HEAD_DIM), bss_ref[h * DEC_SEQ + t], f32)
                for s in range(t + 1):
                    coef = wss_ref[(h * DEC_SEQ + t) * DEC_SEQ + s]
                    mixed = mixed + coef * v[s * DEC_BATCH:(s + 1) * DEC_BATCH, lanes]
                act_ref[rows, lanes] = (u[rows, lanes] * mixed).astype(bf16)


def _stage_b(xn, w_in, ln_g, ln_b, gmlp_ws, gmlp_b, l):
    bias_t = jnp.swapaxes(gmlp_b, 1, 2)
    ws_small = gmlp_ws[l, :, :DEC_SEQ, :DEC_SEQ].reshape(-1)
    b_small = gmlp_b[l, :, :DEC_SEQ].reshape(-1)
    act, v = pl.pallas_call(
        _stage_b_kernel,
        grid=(N_TILES,),
        in_specs=[
            pl.BlockSpec((TM, D_MODEL), lambda i: (i, 0)),
            _once((None, D_MODEL, 2 * D_BR), lambda i: (l, 0, 1)),
            pl.BlockSpec((None, 1, D_BR), lambda i: (l, 0, 0)),
            pl.BlockSpec((None, 1, D_BR), lambda i: (l, 0, 0)),
            pl.BlockSpec((None, GMLP_HEADS, CHUNK, CHUNK), lambda i: (l, 0, 0, 0)),
            pl.BlockSpec((None, CHUNK, GMLP_HEADS), lambda i: (l, 0, 0)),
            pl.BlockSpec(memory_space=pltpu.SMEM),
            pl.BlockSpec(memory_space=pltpu.SMEM),
        ],
        out_specs=[
            pl.BlockSpec((TM, D_BR), lambda i: (i, 0)),
            pl.BlockSpec((DEC_BATCH, DEC_SEQ * D_BR), lambda i: (0, 0)),
        ],
        out_shape=[
            jax.ShapeDtypeStruct((M_ALL, D_BR), bf16),
            jax.ShapeDtypeStruct((DEC_BATCH, DEC_SEQ * D_BR), f32),
        ],
        scratch_shapes=[
            pltpu.VMEM((D_MODEL, 2 * D_BR), bf16),
            pltpu.VMEM((GMLP_HEADS, CHUNK, CHUNK), bf16),
            pltpu.VMEM((GMLP_HEADS, CHUNK, HEAD_DIM), f32),
        ],
        compiler_params=_cparams(1),
        name="branch_b",
    )(xn, w_in, ln_g.reshape(DEPTH, 1, D_BR), ln_b.reshape(DEPTH, 1, D_BR),
      gmlp_ws, bias_t, ws_small, b_small)
    return act, v.reshape(DEC_BATCH, DEC_SEQ, D_BR)


def _stage_c_kernel(xn_ref, w0_ref, w1_ref, w2_ref, dw_ref, hist_ref,
                    act_ref, stp_ref, sts_ref, wb_ref, ext_ref):
    i = pl.program_id(0)
    nblk = D_BR // LANES

    @pl.when(i == 0)
    def _():
        moves = []
        for k in range(nblk):
            for part, w_ref in enumerate((w0_ref, w1_ref, w2_ref)):
                moves.append((w_ref, k * LANES, (3 * k + part) * LANES, LANES))
        _cast_blocks(moves, wb_ref)
        ext_ref[0:HALO_S, :] = jnp.zeros((HALO_S, D_BR), f32)

    def taps(k):
        return dw_ref[0:1, _blk(k)], dw_ref[1:2, _blk(k)], dw_ref[2:3, _blk(k)]

    @pl.when(i < N_PT)
    def _():
        first = lax.rem(i, TILES_PER_SEQ) == 0
        ext_ref[0:HALO_S, :] = jnp.where(first, 0.0, ext_ref[0:HALO_S, :])
        p = _dot(xn_ref[...], wb_ref[...])
        for k in range(nblk):
            w0, w1, w2 = taps(k)
            for c in range(TM // EP_R):
                rows = slice(c * EP_R, (c + 1) * EP_R)
                r0 = HALO_S + c * EP_R
                ext_ref[r0:r0 + EP_R, _blk(k)] = p[rows, _blk(3 * k + 1)] * p[rows, _blk(3 * k + 2)]
                z = (w0 * ext_ref[r0 - 2:r0 - 2 + EP_R, _blk(k)]
                     + w1 * ext_ref[r0 - 1:r0 - 1 + EP_R, _blk(k)]
                     + w2 * ext_ref[r0:r0 + EP_R, _blk(k)])
                act_ref[rows, _blk(k)] = (p[rows, _blk(3 * k)] * z).astype(bf16)
        stp_ref[i // TILES_PER_SEQ] = ext_ref[HALO_S + TM - (SCONV_K - 1):HALO_S + TM, :]
        ext_ref[0:HALO_S, :] = ext_ref[TM:TM + HALO_S, :]

    @pl.when(i == N_PT)
    def _():
        p = _dot(xn_ref[0:M_S, :], wb_ref[...])
        for k in range(nblk):
            w0, w1, w2 = taps(k)
            s = p[:, _blk(3 * k + 1)] * p[:, _blk(3 * k + 2)]
            xp = [hist_ref[:, _blk(k)], hist_ref[:, D_BR + k * LANES:D_BR + (k + 1) * LANES]]
            xp += [s[t * DEC_BATCH:(t + 1) * DEC_BATCH] for t in range(DEC_SEQ)]
            for t in range(DEC_SEQ):
                z = w0 * xp[t] + w1 * xp[t + 1] + w2 * xp[t + 2]
                rows = slice(t * DEC_BATCH, (t + 1) * DEC_BATCH)
                act_ref[rows, _blk(k)] = (p[rows, _blk(3 * k)] * z).astype(bf16)
            sts_ref[:, _blk(k)] = xp[DEC_SEQ]
            sts_ref[:, D_BR + k * LANES:D_BR + (k + 1) * LANES] = xp[DEC_SEQ + 1]


def _stage_c(xn, w_in, sconv_dw, state, l):
    npast = SCONV_K - 1
    hist = state.reshape(DEPTH, DEC_BATCH, npast * D_BR)
    wspec = lambda cb: _once((None, D_MODEL, D_BR), lambda i: (l, 0, cb))
    act, stp, sts = pl.pallas_call(
        _stage_c_kernel,
        grid=(N_TILES,),
        in_specs=[
            pl.BlockSpec((TM, D_MODEL), lambda i: (i, 0)),
            wspec(4), wspec(5), wspec(6),
            pl.BlockSpec((None, SCONV_K, D_BR), lambda i: (l, 0, 0)),
            pl.BlockSpec((None, DEC_BATCH, npast * D_BR), lambda i: (l, 0, 0)),
        ],
        out_specs=[
            pl.BlockSpec((TM, D_BR), lambda i: (i, 0)),
            pl.BlockSpec((BATCH, npast, D_BR), lambda i: (0, 0, 0)),
            pl.BlockSpec((DEC_BATCH, npast * D_BR), lambda i: (0, 0)),
        ],
        out_shape=[
            jax.ShapeDtypeStruct((M_ALL, D_BR), bf16),
            jax.ShapeDtypeStruct((BATCH, npast, D_BR), f32),
            jax.ShapeDtypeStruct((DEC_BATCH, npast * D_BR), f32),
        ],
        scratch_shapes=[
            pltpu.VMEM((D_MODEL, 3 * D_BR), bf16),
            pltpu.VMEM((HALO_S + TM, D_BR), f32),
        ],
        compiler_params=_cparams(1),
        name="branch_c",
    )(xn, w_in, w_in, w_in, sconv_dw, hist)
    return act, stp, sts.reshape(DEC_BATCH, npast, D_BR)


def _stage_d_kernel(xn_ref, w_ref, pw_ref, psc_ref, hist_ref,
                    act_ref, stp_ref, sts_ref, wb_ref, pwb_ref, ext_ref):
    i = pl.program_id(0)
    gdim = D_BR // len(POOL_WINDOWS)

    @pl.when(i == 0)
    def _():
        _cast_rows(w_ref, wb_ref)
        pwb_ref[...] = pw_ref[...].astype(bf16)

    @pl.when(i < N_PT)
    def _():
        tis = i % TILES_PER_SEQ

        @pl.when(tis == 0)
        def _():
            ext_ref[0:HALO_P, :] = jnp.zeros((HALO_P, D_BR), f32)

        ext_ref[HALO_P:HALO_P + TM, :] = _dot(xn_ref[...], wb_ref[...])
        pos1 = tis * TM + 1 + lax.broadcasted_iota(jnp.int32, (TM, gdim), 0)
        for g, win in enumerate(POOL_WINDOWS):
            lanes = slice(g * gdim, (g + 1) * gdim)
            cur = ext_ref[HALO_P:HALO_P + TM, lanes]
            tot = cur
            for k in range(1, win):
                tot = tot + ext_ref[HALO_P - k:HALO_P - k + TM, lanes]
            cnt = jnp.minimum(pos1, win).astype(f32)
            pm = (tot / cnt - cur).astype(bf16)
            act_ref[:, lanes] = (_dot(pm, pwb_ref[g]) * psc_ref[:, lanes]).astype(bf16)

        @pl.when(tis == TILES_PER_SEQ - 1)
        def _():
            stp_ref[i // TILES_PER_SEQ] = ext_ref[HALO_P + TM - POOL_PAST:HALO_P + TM, :]

        ext_ref[0:HALO_P, :] = ext_ref[TM:TM + HALO_P, :]

    @pl.when(i == N_PT)
    def _():
        p = _dot(xn_ref[0:M_S, :], wb_ref[...])

        def slab(s, lanes):
            if s < POOL_PAST:
                return hist_ref[:, s * D_BR + lanes.start:s * D_BR + lanes.stop]
            r0 = (s - POOL_PAST) * DEC_BATCH
            return p[r0:r0 + DEC_BATCH, lanes]

        for t in range(DEC_SEQ):
            rows = slice(t * DEC_BATCH, (t + 1) * DEC_BATCH)
            for g, win in enumerate(POOL_WINDOWS):
                lanes = slice(g * gdim, (g + 1) * gdim)
                cur = slab(POOL_PAST + t, lanes)
                tot = cur
                for k in range(1, win):
                    tot = tot + slab(POOL_PAST + t - k, lanes)
                cnt = float(min(win, PAST_LEN + t + 1))
                pm = (tot / cnt - cur).astype(bf16)
                act_ref[rows, lanes] = (_dot(pm, pwb_ref[g]) * psc_ref[:, lanes]).astype(bf16)
        keep = POOL_PAST - DEC_SEQ
        sts_ref[:, 0:keep * D_BR] = hist_ref[:, DEC_SEQ * D_BR:POOL_PAST * D_BR]
        for t in range(DEC_SEQ):
            sts_ref[:, (keep + t) * D_BR:(keep + t + 1) * D_BR] = (
                p[t * DEC_BATCH:(t + 1) * DEC_BATCH])


def _stage_d(xn, w_in, pool_w, pool_scale, state, l):
    ngroup = len(POOL_WINDOWS)
    gdim = D_BR // ngroup
    hist = state.reshape(DEPTH, DEC_BATCH, POOL_PAST * D_BR)
    act, stp, sts = pl.pallas_call(
        _stage_d_kernel,
        grid=(N_TILES,),
        in_specs=[
            pl.BlockSpec((TM, D_MODEL), lambda i: (i, 0)),
            _once((None, D_MODEL, D_BR), lambda i: (l, 0, 7)),
            pl.BlockSpec((None, ngroup, gdim, gdim), lambda i: (l, 0, 0, 0)),
            pl.BlockSpec((None, 1, D_BR), lambda i: (l, 0, 0)),
            _once((None, DEC_BATCH, POOL_PAST * D_BR), lambda i: (l, 0, 0)),
        ],
        out_specs=[
            pl.BlockSpec((TM, D_BR), lambda i: (i, 0)),
            pl.BlockSpec((BATCH, POOL_PAST, D_BR), lambda i: (0, 0, 0)),
            pl.BlockSpec((DEC_BATCH, POOL_PAST * D_BR), lambda i: (0, 0)),
        ],
        out_shape=[
            jax.ShapeDtypeStruct((M_ALL, D_BR), bf16),
            jax.ShapeDtypeStruct((BATCH, POOL_PAST, D_BR), f32),
            jax.ShapeDtypeStruct((DEC_BATCH, POOL_PAST * D_BR), f32),
        ],
        scratch_shapes=[
            pltpu.VMEM((D_MODEL, D_BR), bf16),
            pltpu.VMEM((ngroup, gdim, gdim), bf16),
            pltpu.VMEM((HALO_P + TM, D_BR), f32),
        ],
        compiler_params=_cparams(1),
        name="branch_d",
    )(xn, w_in, pool_w, pool_scale.reshape(DEPTH, 1, D_BR), hist)
    return act, stp, sts.reshape(DEC_BATCH, POOL_PAST, D_BR)


def _stage_g_kernel(xn_ref, a0_ref, a1_ref, a2_ref, a3_ref,
                    g0_ref, g1_ref, g2_ref, g3_ref, wbr_ref,
                    out_ref, wgb_ref, wbb_ref):
    i = pl.program_id(1)

    @pl.when(i == 0)
    def _():
        for b, g_ref in enumerate((g0_ref, g1_ref, g2_ref, g3_ref)):
            _cast_rows(g_ref, wgb_ref, b * TN_G)
        wbb_ref[...] = wbr_ref[...].astype(bf16)

    def compute(rows):
        gate = _sigmoid(_dot(xn_ref[rows, :], wgb_ref[...]))
        merged = None
        for b, a_ref in enumerate((a0_ref, a1_ref, a2_ref, a3_ref)):
            term = gate[:, b * TN_G:(b + 1) * TN_G] * _dot(a_ref[rows, :], wbb_ref[b])
            merged = term if merged is None else merged + term
        out_ref[rows, :] = merged.astype(bf16)

    @pl.when(i < N_PT)
    def _():
        compute(slice(0, TM))

    @pl.when(i == N_PT)
    def _():
        compute(slice(0, M_S))


def _stage_g(xn, acts, w_in, w_branch, l):
    gate_blk0 = GATE_COL0 // TN_G
    per_branch = D_MODEL // TN_G
    gspec = lambda b: pl.BlockSpec(
        (None, D_MODEL, TN_G), lambda c, i: (l, 0, gate_blk0 + b * per_branch + c))
    aspec = pl.BlockSpec((TM, D_BR), lambda c, i: (i, 0))
    return pl.pallas_call(
        _stage_g_kernel,
        grid=(D_MODEL // TN_G, N_TILES),
        in_specs=[
            pl.BlockSpec((TM, D_MODEL), lambda c, i: (i, 0)),
            aspec, aspec, aspec, aspec,
            gspec(0), gspec(1), gspec(2), gspec(3),
            pl.BlockSpec((None, N_BRANCH, D_BR, TN_G), lambda c, i: (l, 0, 0, c)),
        ],
        out_specs=pl.BlockSpec((TM, TN_G), lambda c, i: (i, c)),
        out_shape=jax.ShapeDtypeStruct((M_ALL, D_MODEL), bf16),
        scratch_shapes=[
            pltpu.VMEM((D_MODEL, N_BRANCH * TN_G), bf16),
            pltpu.VMEM((N_BRANCH, D_BR, TN_G), bf16),
        ],
        compiler_params=_cparams(2),
        name="gate_merge",
    )(xn, *acts, w_in, w_in, w_in, w_in, w_branch)


def _stage_res_kernel(lhs_ref, w_ref, x_ref, o_ref, wb_ref):
    i = pl.program_id(1)

    @pl.when(i == 0)
    def _():
        _cast_rows(w_ref, wb_ref)

    @pl.when(i < N_PT)
    def _():
        o_ref[...] = x_ref[...] + _dot(lhs_ref[...], wb_ref[...])

    @pl.when(i == N_PT)
    def _():
        o_ref[0:M_S, :] = x_ref[0:M_S, :] + _dot(lhs_ref[0:M_S, :], wb_ref[...])


def _stage_res(lhs, w, x, l, tn, name, single_buffer_w):
    k = lhs.shape[1]
    wshape = (None, k, tn)
    wmap = lambda c, i: (l, 0, c)
    wspec = _once(wshape, wmap) if single_buffer_w else pl.BlockSpec(wshape, wmap)
    return pl.pallas_call(
        _stage_res_kernel,
        grid=(D_MODEL // tn, N_TILES),
        in_specs=[
            pl.BlockSpec((TM, k), lambda c, i: (i, 0)),
            wspec,
            pl.BlockSpec((TM, tn), lambda c, i: (i, c)),
        ],
        out_specs=pl.BlockSpec((TM, tn), lambda c, i: (i, c)),
        out_shape=jax.ShapeDtypeStruct((M_ALL, D_MODEL), f32),
        scratch_shapes=[pltpu.VMEM((k, tn), bf16)],
        compiler_params=_cparams(2),
        name=name,
    )(lhs, w, x)


def _stage_o_kernel(first_layer, *refs):
    if first_layer:
        m_ref, w_ref, xp_ref, xs_ref, g_ref, x_ref, xn_ref, wb_ref = refs
    else:
        m_ref, w_ref, xin_ref, g_ref, x_ref, xn_ref, wb_ref = refs
    i = pl.program_id(0)

    @pl.when(i == 0)
    def _():
        _cast_rows(w_ref, wb_ref)

    g = g_ref[...]

    def finish(r0, nrows, resid_rows, proj):
        for c in range(nrows // LN_R):
            rows = slice(r0 + c * LN_R, r0 + (c + 1) * LN_R)
            x = resid_rows(slice(c * LN_R, (c + 1) * LN_R)) + proj[rows]
            x_ref[rows, :] = x
            xn_ref[rows, :] = _rms(x, g).astype(bf16)

    if first_layer:
        @pl.when(i < M_P // TM_O)
        def _():
            finish(0, TM_O, lambda r: xp_ref[r, :], _dot(m_ref[...], wb_ref[...]))

        @pl.when(i >= M_P // TM_O)
        def _():
            proj = _dot(m_ref[...], wb_ref[...])
            for t in range(DEC_SEQ):
                finish(t * DEC_BATCH, DEC_BATCH,
                       lambda r, t=t: xs_ref[r, t * D_MODEL:(t + 1) * D_MODEL], proj)
    else:
        finish(0, TM_O, lambda r: xin_ref[r, :], _dot(m_ref[...], wb_ref[...]))


def _stage_o(merged, w_o, norm_w, l, x=None, x_prompt=None, x_sample=None):
    first_layer = x is None
    n_steps = M_ALL // TM_O
    row_blk = pl.BlockSpec((TM_O, D_MODEL), lambda i: (i, 0))
    if first_layer:
        n_p = M_P // TM_O
        resid = [x_prompt.reshape(M_P, D_MODEL), x_sample.reshape(DEC_BATCH, DEC_SEQ * D_MODEL)]
        resid_specs = [
            pl.BlockSpec((TM_O, D_MODEL), lambda i: (jnp.minimum(i, n_p - 1), 0)),
            pl.BlockSpec((DEC_BATCH, DEC_SEQ * D_MODEL), lambda i: (0, 0)),
        ]
    else:
        resid = [x]
        resid_specs = [row_blk]
    return pl.pallas_call(
        functools.partial(_stage_o_kernel, first_layer),
        grid=(n_steps,),
        in_specs=[row_blk, _once((None, D_MODEL, D_MODEL), lambda i: (l, 0, 0))]
        + resid_specs + [pl.BlockSpec((None, 1, D_MODEL), lambda i: (l, 0, 0))],
        out_specs=[row_blk, row_blk],
        out_shape=[
            jax.ShapeDtypeStruct((M_ALL, D_MODEL), f32),
            jax.ShapeDtypeStruct((M_ALL, D_MODEL), bf16),
        ],
        scratch_shapes=[pltpu.VMEM((D_MODEL, D_MODEL), bf16)],
        compiler_params=_cparams(1),
        name="out_proj_norm",
    )(merged, w_o, *resid, norm_w.reshape(DEPTH, 1, D_MODEL))


def _stage_u_kernel(xn_ref, wg_ref, wv_ref, dwg_ref, dwv_ref,
                    hgh_ref, hvh_ref,
                    a_ref, stp_ref, sts_ref, wb_ref, ext_ref):
    i = pl.program_id(1)
    nblk = TK_U // U_BLK
    npast = FFN_K - 1

    def _blk(k):
        return slice(k * U_BLK, (k + 1) * U_BLK)

    @pl.when(i == 0)
    def _():
        moves = []
        for k in range(nblk):
            moves.append((wg_ref, k * U_BLK, 2 * k * U_BLK, U_BLK))
            moves.append((wv_ref, k * U_BLK, (2 * k + 1) * U_BLK, U_BLK))
        _cast_blocks(moves, wb_ref)
        ext_ref[0:HALO_S, :] = jnp.zeros((HALO_S, 2 * TK_U), f32)

    def taps(dw_ref, k):
        return dw_ref[0:1, _blk(k)], dw_ref[1:2, _blk(k)], dw_ref[2:3, _blk(k)]

    @pl.when(i < N_PT)
    def _():
        first = lax.rem(i, TILES_PER_SEQ) == 0
        ext_ref[0:HALO_S, :] = jnp.where(first, 0.0, ext_ref[0:HALO_S, :])
        ext_ref[HALO_S:HALO_S + TM, :] = _dot(xn_ref[...], wb_ref[...])
        for k in range(nblk):
            for c in range(TM // U_R):
                r0 = HALO_S + c * U_R

                def conv(dw_ref, lanes):
                    w0, w1, w2 = taps(dw_ref, k)
                    return (w0 * ext_ref[r0 - 2:r0 - 2 + U_R, lanes]
                            + w1 * ext_ref[r0 - 1:r0 - 1 + U_R, lanes]
                            + w2 * ext_ref[r0:r0 + U_R, lanes])

                hg = conv(dwg_ref, _blk(2 * k))
                hv = conv(dwv_ref, _blk(2 * k + 1))
                a_ref[c * U_R:(c + 1) * U_R, _blk(k)] = (_silu(hg) * hv).astype(bf16)
        b = i // TILES_PER_SEQ
        last = slice(HALO_S + TM - npast, HALO_S + TM)
        for k in range(nblk):
            stp_ref[b, 0, :, _blk(k)] = ext_ref[last, _blk(2 * k)]
            stp_ref[b, 1, :, _blk(k)] = ext_ref[last, _blk(2 * k + 1)]
        ext_ref[0:HALO_S, :] = ext_ref[TM:TM + HALO_S, :]

    @pl.when(i == N_PT)
    def _():
        h = _dot(xn_ref[0:M_S, :], wb_ref[...])

        def conv_slabs(dw_ref, hist_ref, k, lanes):
            w0, w1, w2 = taps(dw_ref, k)
            xp = [hist_ref[:, r, _blk(k)] for r in range(npast)]
            xp += [h[t * DEC_BATCH:(t + 1) * DEC_BATCH, lanes] for t in range(DEC_SEQ)]
            return [w0 * xp[t] + w1 * xp[t + 1] + w2 * xp[t + 2] for t in range(DEC_SEQ)]

        for k in range(nblk):
            hg = conv_slabs(dwg_ref, hgh_ref, k, _blk(2 * k))
            hv = conv_slabs(dwv_ref, hvh_ref, k, _blk(2 * k + 1))
            for t in range(DEC_SEQ):
                a_ref[t * DEC_BATCH:(t + 1) * DEC_BATCH, _blk(k)] = (
                    _silu(hg[t]) * hv[t]).astype(bf16)
            for r in range(npast):
                t = DEC_SEQ - npast + r
                rows = slice(t * DEC_BATCH, (t + 1) * DEC_BATCH)
                sts_ref[r, 0, :, _blk(k)] = h[rows, _blk(2 * k)]
                sts_ref[r, 1, :, _blk(k)] = h[rows, _blk(2 * k + 1)]


def _stage_u(xn, ffn_up, ffn_dw, state, l):
    nk = D_FF // TK_U
    npast = FFN_K - 1
    wspec = lambda half: pl.BlockSpec(
        (None, D_MODEL, TK_U), lambda k, i: (l, 0, half * nk + k))
    dspec = lambda half: pl.BlockSpec(
        (None, FFN_K, TK_U), lambda k, i: (l, 0, half * nk + k))
    hspec = lambda half: pl.BlockSpec(
        (None, DEC_BATCH, npast, TK_U), lambda k, i: (l, 0, 0, half * nk + k))
    a, stp, sts = pl.pallas_call(
        _stage_u_kernel,
        grid=(nk, N_TILES),
        in_specs=[
            pl.BlockSpec((TM, D_MODEL), lambda k, i: (i, 0)),
            wspec(0), wspec(1), dspec(0), dspec(1),
            hspec(0), hspec(1),
        ],
        out_specs=[
            pl.BlockSpec((TM, TK_U), lambda k, i: (i, k)),
            pl.BlockSpec((BATCH, 2, npast, TK_U), lambda k, i: (0, 0, 0, k)),
            pl.BlockSpec((npast, 2, DEC_BATCH, TK_U), lambda k, i: (0, 0, 0, k)),
        ],
        out_shape=[
            jax.ShapeDtypeStruct((M_ALL, D_FF), bf16),
            jax.ShapeDtypeStruct((BATCH, 2, npast, D_FF), f32),
            jax.ShapeDtypeStruct((npast, 2, DEC_BATCH, D_FF), f32),
        ],
        scratch_shapes=[
            pltpu.VMEM((D_MODEL, 2 * TK_U), bf16),
            pltpu.VMEM((HALO_S + TM, 2 * TK_U), f32),
        ],
        compiler_params=_cparams(2),
        name="ffn_up",
    )(xn, ffn_up, ffn_up, ffn_dw, ffn_dw, state, state)
    stp = jnp.transpose(stp, (0, 2, 1, 3)).reshape(BATCH, npast, 2 * D_FF)
    sts = jnp.transpose(sts, (2, 0, 1, 3)).reshape(DEC_BATCH, npast, 2 * D_FF)
    return a, stp, sts


def kernel(x_prompt, x_sample, state_conf_conv, state_sconv, state_pool, state_ffn_conv,
           norm_mix, w_in, conf_dw, conf_ln_g, conf_ln_b, gmlp_ln_g, gmlp_ln_b, gmlp_ws,
           gmlp_b, sconv_dw, pool_w, pool_scale, w_branch, w_o, norm_ffn, ffn_up, ffn_dw,
           ffn_down, norm_final):
    conf_p, conf_s, sconv_p, sconv_s, pool_p, pool_s, ffn_p, ffn_s, v_s = ([] for _ in range(9))
    x = None
    for l in range(DEPTH):
        if l == 0:
            xn = _norm0(x_prompt, x_sample, norm_mix, l)
        else:
            xn = _norm(x, norm_mix, l)
        act_a, st_p, st_s = _stage_a(xn, w_in, conf_dw, conf_ln_g, conf_ln_b, state_conf_conv, l)
        conf_p.append(st_p)
        conf_s.append(st_s)
        act_b, v = _stage_b(xn, w_in, gmlp_ln_g, gmlp_ln_b, gmlp_ws, gmlp_b, l)
        v_s.append(v)
        act_c, st_p, st_s = _stage_c(xn, w_in, sconv_dw, state_sconv, l)
        sconv_p.append(st_p)
        sconv_s.append(st_s)
        act_d, st_p, st_s = _stage_d(xn, w_in, pool_w, pool_scale, state_pool, l)
        pool_p.append(st_p)
        pool_s.append(st_s)
        merged = _stage_g(xn, (act_a, act_b, act_c, act_d), w_in, w_branch, l)
        if l == 0:
            x, xn = _stage_o(merged, w_o, norm_ffn, l, x_prompt=x_prompt, x_sample=x_sample)
        else:
            x, xn = _stage_o(merged, w_o, norm_ffn, l, x=x)
        a, st_p, st_s = _stage_u(xn, ffn_up, ffn_dw, state_ffn_conv, l)
        ffn_p.append(st_p)
        ffn_s.append(st_s)
        x = _stage_res(a, ffn_down, x, l, TN_D, "ffn_down", True)
    y_prompt, y_sample = _final_norm(x, norm_final)
    st = jnp.stack
    return (y_prompt, y_sample, st(conf_p), st(conf_s), st(sconv_p), st(sconv_s),
            st(pool_p), st(pool_s), st(ffn_p), st(ffn_s), st(v_s))
```

```python
import functools

import jax
import jax.numpy as jnp
from jax import lax
from jax.experimental import pallas as pl
from jax.experimental.pallas import tpu as pltpu

D_MODEL = 2048
BATCH = 4
SEQ = 2048
DEPTH = 2
DEC_BATCH = 128
DEC_SEQ = 4
PAST_LEN = 16384
D_BR = 512
N_BRANCH = 4
CONF_K = 31
GMLP_HEADS = 4
HEAD_DIM = D_BR // GMLP_HEADS
CHUNK = 128
SCONV_K = 3
POOL_WINDOWS = (2, 4, 8, 16)
POOL_PAST = 15
D_FF = 5632
FFN_K = 3
EPS = 1e-6
GATE_COL0 = 8 * D_BR

TM = 1024
M_P = BATCH * SEQ
M_S = DEC_BATCH * DEC_SEQ
M_ALL = M_P + M_S
N_PT = M_P // TM
TILES_PER_SEQ = SEQ // TM
N_TILES = N_PT + 1

LANES = 128
HALO_A = 32
HALO_P = 16
HALO_S = 8
CONV_R = 128
EP_R = 128
LN_R = 64
ROLL_B = 64
FFN_ST_W = 1408
U_R = 1024
U_BLK = 512

TN_G = 256
TM_O = 512
TK_U = 512
TN_D = 512

VMEM_LIMIT = 60000 * 1024

f32 = jnp.float32
bf16 = jnp.bfloat16


def _cparams(n_axes):
    return pltpu.CompilerParams(
        dimension_semantics=("arbitrary",) * n_axes, vmem_limit_bytes=VMEM_LIMIT)


def _once(block_shape, index_map):
    return pl.BlockSpec(block_shape, index_map, pipeline_mode=pl.Buffered(1))


def _rms(x, g):
    return x * lax.rsqrt(jnp.mean(x * x, axis=-1, keepdims=True) + EPS) * g


def _layernorm(x, g, b):
    mu = jnp.mean(x, axis=-1, keepdims=True)
    d = x - mu
    var = jnp.mean(d * d, axis=-1, keepdims=True)
    return d * lax.rsqrt(var + EPS) * g + b


def _sigmoid(x):
    return 1.0 / (1.0 + jnp.exp(-x))


def _silu(x):
    h = 0.5 * x
    return h + h * jnp.tanh(h)


def _gelu_tanh(x):
    c = 0.7978845608028654
    return 0.5 * x * (1.0 + jnp.tanh(c * (x + 0.044715 * (x * x * x))))


def _cast_blocks(moves, wb_ref, rows_per_step=256):
    k = wb_ref.shape[0]

    def body(c, carry):
        r = pl.multiple_of(c * rows_per_step, rows_per_step)
        for w_ref, src, dst, n in moves:
            wb_ref[pl.ds(r, rows_per_step), dst:dst + n] = (
                w_ref[pl.ds(r, rows_per_step), src:src + n].astype(bf16))
        return carry

    lax.fori_loop(0, k // rows_per_step, body, 0)


def _cast_rows(w_ref, wb_ref, col0=0):
    _cast_blocks([(w_ref, 0, col0, w_ref.shape[1])], wb_ref)


def _dot(a, b):
    return jnp.dot(a, b, preferred_element_type=f32)


def _blk(k):
    return slice(k * LANES, (k + 1) * LANES)


def _norm_rows(src_ref, put, rows, chunk=256):
    def body(c, carry):
        r = pl.multiple_of(c * chunk, chunk)
        put(r, chunk, src_ref[pl.ds(r, chunk), :])
        return carry

    lax.fori_loop(0, rows // chunk, body, 0)


def _norm0_kernel(xp_ref, xs_ref, g_ref, xn_ref):
    i = pl.program_id(0)
    g = g_ref[...]

    @pl.when(i < N_PT)
    def _():
        def put(r, n, x):
            xn_ref[pl.ds(r, n), :] = _rms(x, g).astype(bf16)
        _norm_rows(xp_ref, put, TM)

    @pl.when(i == N_PT)
    def _():
        for t in range(DEC_SEQ):
            x = xs_ref[:, t, :]
            xn_ref[t * DEC_BATCH:(t + 1) * DEC_BATCH, :] = _rms(x, g).astype(bf16)


def _norm0(x_prompt, x_sample, norm_w, l):
    xp = x_prompt.reshape(M_P, D_MODEL)
    return pl.pallas_call(
        _norm0_kernel,
        grid=(N_TILES,),
        in_specs=[
            pl.BlockSpec((TM, D_MODEL), lambda i: (jnp.minimum(i, N_PT - 1), 0)),
            pl.BlockSpec((DEC_BATCH, DEC_SEQ, D_MODEL), lambda i: (0, 0, 0)),
            pl.BlockSpec((None, 1, D_MODEL), lambda i: (l, 0, 0)),
        ],
        out_specs=pl.BlockSpec((TM, D_MODEL), lambda i: (i, 0)),
        out_shape=jax.ShapeDtypeStruct((M_ALL, D_MODEL), bf16),
        compiler_params=_cparams(1),
        name="norm0",
    )(xp, x_sample, norm_w.reshape(DEPTH, 1, D_MODEL))


def _norm_kernel(x_ref, g_ref, xn_ref):
    i = pl.program_id(0)
    g = g_ref[...]

    def put(r, n, x):
        xn_ref[pl.ds(r, n), :] = _rms(x, g).astype(bf16)

    @pl.when(i < N_PT)
    def _():
        _norm_rows(x_ref, put, TM)

    @pl.when(i == N_PT)
    def _():
        _norm_rows(x_ref, put, M_S)


def _norm(x, norm_w, l):
    return pl.pallas_call(
        _norm_kernel,
        grid=(N_TILES,),
        in_specs=[
            pl.BlockSpec((TM, D_MODEL), lambda i: (i, 0)),
            pl.BlockSpec((None, 1, D_MODEL), lambda i: (l, 0, 0)),
        ],
        out_specs=pl.BlockSpec((TM, D_MODEL), lambda i: (i, 0)),
        out_shape=jax.ShapeDtypeStruct((M_ALL, D_MODEL), bf16),
        compiler_params=_cparams(1),
        name="norm",
    )(x, norm_w.reshape(DEPTH, 1, D_MODEL))


def _final_norm_kernel(x_ref, g_ref, yp_ref, ys_ref):
    i = pl.program_id(0)
    g = g_ref[...]

    @pl.when(i < N_PT)
    def _():
        def put(r, n, x):
            yp_ref[pl.ds(r, n), :] = _rms(x, g)
        _norm_rows(x_ref, put, TM)

    @pl.when(i == N_PT)
    def _():
        for t in range(DEC_SEQ):
            x = x_ref[t * DEC_BATCH:(t + 1) * DEC_BATCH, :]
            ys_ref[:, t, :] = _rms(x, g)


def _final_norm(x, norm_w):
    yp, ys = pl.pallas_call(
        _final_norm_kernel,
        grid=(N_TILES,),
        in_specs=[
            pl.BlockSpec((TM, D_MODEL), lambda i: (i, 0)),
            pl.BlockSpec((1, D_MODEL), lambda i: (0, 0)),
        ],
        out_specs=[
            pl.BlockSpec((TM, D_MODEL), lambda i: (jnp.minimum(i, N_PT - 1), 0)),
            pl.BlockSpec((DEC_BATCH, DEC_SEQ, D_MODEL), lambda i: (0, 0, 0)),
        ],
        out_shape=[
            jax.ShapeDtypeStruct((M_P, D_MODEL), f32),
            jax.ShapeDtypeStruct((DEC_BATCH, DEC_SEQ, D_MODEL), f32),
        ],
        compiler_params=_cparams(1),
        name="final_norm",
    )(x, norm_w.reshape(1, D_MODEL))
    return yp.reshape(BATCH, SEQ, D_MODEL), ys


def _time_major(state):
    return jnp.transpose(state, (0, 2, 1, 3))


def _roll_state_kernel(old_ref, *refs):
    new_refs, out_ref = refs[:-1], refs[-1]
    l = pl.program_id(0)
    npast, width = old_ref.shape[0], old_ref.shape[2]
    keep = npast - DEC_SEQ
    for s in range(keep):
        out_ref[s] = old_ref[DEC_SEQ + s]
    for d, new_ref in enumerate(new_refs):
        @pl.when(l == d)
        def _(new_ref=new_ref):
            for t in range(DEC_SEQ):
                out_ref[keep + t] = new_ref[:, t * width:(t + 1) * width]


def _roll_state(old, new_rows):
    old_t = _time_major(old)
    npast, width = old_t.shape[1], old_t.shape[3]
    blk = pl.BlockSpec((None, npast, ROLL_B, width), lambda l, j: (l, 0, j, 0))
    nspec = pl.BlockSpec((ROLL_B, DEC_SEQ * width), lambda l, j: (j, 0))
    out = pl.pallas_call(
        _roll_state_kernel,
        grid=(DEPTH, DEC_BATCH // ROLL_B),
        in_specs=[blk] + [nspec] * DEPTH,
        out_specs=blk,
        out_shape=jax.ShapeDtypeStruct(old_t.shape, f32),
        compiler_params=_cparams(2),
        name="roll_state",
    )(old_t, *new_rows)
    return _time_major(out)


def _conv31_chunk(ext_ref, base, dw_ref, lanes):
    y = None
    for r in range(8):
        rows = CONV_R if r == 0 else CONV_R + 8
        z = None
        for a in range(5):
            j = 8 * a + r - 2
            if 0 <= j < CONF_K:
                term = dw_ref[pl.ds(j, 1), lanes] * ext_ref[pl.ds(base + 8 * a, rows), lanes]
                z = term if z is None else z + term
        zr = z[r:r + CONV_R]
        y = zr if y is None else y + zr
    return y


def _stage_a_kernel(xn_ref, w_ref, dw_ref, lng_ref, lnb_ref, hist_ref,
                    act_ref, stp_ref, new_ref, wb_ref, ext_ref, y_ref):
    i = pl.program_id(0)
    nblk = D_BR // LANES
    npast = CONF_K - 1

    @pl.when(i == 0)
    def _():
        moves = []
        for k in range(nblk):
            moves.append((w_ref, k * LANES, 2 * k * LANES, LANES))
            moves.append((w_ref, D_BR + k * LANES, (2 * k + 1) * LANES, LANES))
        _cast_blocks(moves, wb_ref)
        ext_ref[0:HALO_A, :] = jnp.zeros((HALO_A, D_BR), f32)

    lng = lng_ref[...]
    lnb = lnb_ref[...]

    def glu_block(p, k, rows):
        return p[rows, _blk(2 * k)] * _sigmoid(p[rows, _blk(2 * k + 1)])

    @pl.when(i < N_PT)
    def _():
        first = lax.rem(i, TILES_PER_SEQ) == 0
        ext_ref[0:HALO_A, :] = jnp.where(first, 0.0, ext_ref[0:HALO_A, :])
        p = _dot(xn_ref[...], wb_ref[...])
        for k in range(nblk):
            for c in range(TM // CONV_R):
                rows = slice(c * CONV_R, (c + 1) * CONV_R)
                ext_ref[HALO_A + c * CONV_R:HALO_A + (c + 1) * CONV_R, _blk(k)] = (
                    glu_block(p, k, rows))
            for c in range(TM // CONV_R):
                y_ref[c * CONV_R:(c + 1) * CONV_R, _blk(k)] = (
                    _conv31_chunk(ext_ref, c * CONV_R, dw_ref, _blk(k)))
        for c in range(TM // LN_R):
            rows = slice(c * LN_R, (c + 1) * LN_R)
            act_ref[rows, :] = _silu(_layernorm(y_ref[rows, :], lng, lnb)).astype(bf16)
        stp_ref[i // TILES_PER_SEQ] = ext_ref[HALO_A + TM - npast:HALO_A + TM, :]
        ext_ref[0:HALO_A, :] = ext_ref[TM:TM + HALO_A, :]

    @pl.when(i == N_PT)
    def _():
        p = _dot(xn_ref[0:M_S, :], wb_ref[...])
        glu = jnp.concatenate([glu_block(p, k, slice(0, M_S)) for k in range(nblk)], axis=1)
        rb = 16
        for q in range(DEC_BATCH // rb):
            y = [None] * DEC_SEQ
            for s in range(npast + DEC_SEQ):
                if s < npast:
                    slab = hist_ref[s, q * rb:(q + 1) * rb, :]
                else:
                    r0 = (s - npast) * DEC_BATCH + q * rb
                    slab = glu[r0:r0 + rb]
                for t in range(DEC_SEQ):
                    j = s - t
                    if 0 <= j < CONF_K:
                        term = dw_ref[pl.ds(j, 1), :] * slab
                        y[t] = term if y[t] is None else y[t] + term
            for t in range(DEC_SEQ):
                r0 = t * DEC_BATCH + q * rb
                act_ref[r0:r0 + rb, :] = _silu(_layernorm(y[t], lng, lnb)).astype(bf16)
        for t in range(DEC_SEQ):
            new_ref[:, t * D_BR:(t + 1) * D_BR] = glu[t * DEC_BATCH:(t + 1) * DEC_BATCH]


def _stage_a(xn, w_in, conf_dw, ln_g, ln_b, state, l):
    npast = CONF_K - 1
    return pl.pallas_call(
        _stage_a_kernel,
        grid=(N_TILES,),
        in_specs=[
            pl.BlockSpec((TM, D_MODEL), lambda i: (i, 0)),
            _once((None, D_MODEL, 2 * D_BR), lambda i: (l, 0, 0)),
            pl.BlockSpec((None, CONF_K, D_BR), lambda i: (l, 0, 0)),
            pl.BlockSpec((None, 1, D_BR), lambda i: (l, 0, 0)),
            pl.BlockSpec((None, 1, D_BR), lambda i: (l, 0, 0)),
            _once((None, npast, DEC_BATCH, D_BR), lambda i: (l, 0, 0, 0)),
        ],
        out_specs=[
            pl.BlockSpec((TM, D_BR), lambda i: (i, 0)),
            pl.BlockSpec((BATCH, npast, D_BR), lambda i: (0, 0, 0)),
            pl.BlockSpec((DEC_BATCH, DEC_SEQ * D_BR), lambda i: (0, 0)),
        ],
        out_shape=[
            jax.ShapeDtypeStruct((M_ALL, D_BR), bf16),
            jax.ShapeDtypeStruct((BATCH, npast, D_BR), f32),
            jax.ShapeDtypeStruct((DEC_BATCH, DEC_SEQ * D_BR), f32),
        ],
        scratch_shapes=[
            pltpu.VMEM((D_MODEL, 2 * D_BR), bf16),
            pltpu.VMEM((HALO_A + TM, D_BR), f32),
            pltpu.VMEM((TM, D_BR), f32),
        ],
        compiler_params=_cparams(1),
        name="branch_a",
    )(xn, w_in, conf_dw, ln_g.reshape(DEPTH, 1, D_BR), ln_b.reshape(DEPTH, 1, D_BR),
      _time_major(state))


def _stage_b_kernel(xn_ref, w_ref, lng_ref, lnb_ref, ws_ref, bt_ref, wss_ref, bss_ref,
                    act_ref, v_ref, wb_ref, tril_ref, bias_ref):
    i = pl.program_id(0)

    @pl.when(i == 0)
    def _():
        _cast_rows(w_ref, wb_ref)
        row = lax.broadcasted_iota(jnp.int32, (CHUNK, CHUNK), 0)
        col = lax.broadcasted_iota(jnp.int32, (CHUNK, CHUNK), 1)
        for h in range(GMLP_HEADS):
            tril_ref[h] = jnp.where(row >= col, ws_ref[h], 0.0).astype(bf16)
            bias_ref[h] = jnp.broadcast_to(bt_ref[:, h:h + 1], (CHUNK, HEAD_DIM))

    lng = lng_ref[...]
    lnb = lnb_ref[...]

    @pl.when(i < N_PT)
    def _():
        p = _dot(xn_ref[...], wb_ref[...])
        for c in range(TM // CHUNK):
            rows = slice(c * CHUNK, (c + 1) * CHUNK)
            u = _gelu_tanh(p[rows, :D_BR])
            v = _layernorm(_gelu_tanh(p[rows, D_BR:]), lng, lnb).astype(bf16)
            for h in range(GMLP_HEADS):
                lanes = slice(h * HEAD_DIM, (h + 1) * HEAD_DIM)
                mixed = _dot(tril_ref[h], v[:, lanes]) + bias_ref[h]
                act_ref[rows, lanes] = (u[:, lanes] * mixed).astype(bf16)

    @pl.when(i == N_PT)
    def _():
        p = _dot(xn_ref[0:M_S, :], wb_ref[...])
        u = _gelu_tanh(p[:, :D_BR])
        v = _layernorm(_gelu_tanh(p[:, D_BR:]), lng, lnb)
        for t in range(DEC_SEQ):
            v_ref[:, t * D_BR:(t + 1) * D_BR] = v[t * DEC_BATCH:(t + 1) * DEC_BATCH]
        for t in range(DEC_SEQ):
            rows = slice(t * DEC_BATCH, (t + 1) * DEC_BATCH)
            for h in range(GMLP_HEADS):
                lanes = slice(h * HEAD_DIM, (h + 1) * HEAD_DIM)
                mixed = jnp.full((DEC_BATCH, HEAD_DIM), bss_ref[h * DEC_SEQ + t], f32)
                for s in range(t + 1):
                    coef = wss_ref[(h * DEC_SEQ + t) * DEC_SEQ + s]
                    mixed = mixed + coef * v[s * DEC_BATCH:(s + 1) * DEC_BATCH, lanes]
                act_ref[rows, lanes] = (u[rows, lanes] * mixed).astype(bf16)


def _stage_b(xn, w_in, ln_g, ln_b, gmlp_ws, gmlp_b, l):
    bias_t = jnp.swapaxes(gmlp_b, 1, 2)
    ws_small = gmlp_ws[l, :, :DEC_SEQ, :DEC_SEQ].reshape(-1)
    b_small = gmlp_b[l, :, :DEC_SEQ].reshape(-1)
    act, v = pl.pallas_call(
        _stage_b_kernel,
        grid=(N_TILES,),
        in_specs=[
            pl.BlockSpec((TM, D_MODEL), lambda i: (i, 0)),
            _once((None, D_MODEL, 2 * D_BR), lambda i: (l, 0, 1)),
            pl.BlockSpec((None, 1, D_BR), lambda i: (l, 0, 0)),
            pl.BlockSpec((None, 1, D_BR), lambda i: (l, 0, 0)),
            pl.BlockSpec((None, GMLP_HEADS, CHUNK, CHUNK), lambda i: (l, 0, 0, 0)),
            pl.BlockSpec((None, CHUNK, GMLP_HEADS), lambda i: (l, 0, 0)),
            pl.BlockSpec(memory_space=pltpu.SMEM),
            pl.BlockSpec(memory_space=pltpu.SMEM),
        ],
        out_specs=[
            pl.BlockSpec((TM, D_BR), lambda i: (i, 0)),
            pl.BlockSpec((DEC_BATCH, DEC_SEQ * D_BR), lambda i: (0, 0)),
        ],
        out_shape=[
            jax.ShapeDtypeStruct((M_ALL, D_BR), bf16),
            jax.ShapeDtypeStruct((DEC_BATCH, DEC_SEQ * D_BR), f32),
        ],
        scratch_shapes=[
            pltpu.VMEM((D_MODEL, 2 * D_BR), bf16),
            pltpu.VMEM((GMLP_HEADS, CHUNK, CHUNK), bf16),
            pltpu.VMEM((GMLP_HEADS, CHUNK, HEAD_DIM), f32),
        ],
        compiler_params=_cparams(1),
        name="branch_b",
    )(xn, w_in, ln_g.reshape(DEPTH, 1, D_BR), ln_b.reshape(DEPTH, 1, D_BR),
      gmlp_ws, bias_t, ws_small, b_small)
    return act, v.reshape(DEC_BATCH, DEC_SEQ, D_BR)


def _stage_c_kernel(xn_ref, w0_ref, w1_ref, w2_ref, dw_ref, hist_ref,
                    act_ref, stp_ref, sts_ref, wb_ref, ext_ref):
    i = pl.program_id(0)
    nblk = D_BR // LANES

    @pl.when(i == 0)
    def _():
        moves = []
        for k in range(nblk):
            for part, w_ref in enumerate((w0_ref, w1_ref, w2_ref)):
                moves.append((w_ref, k * LANES, (3 * k + part) * LANES, LANES))
        _cast_blocks(moves, wb_ref)
        ext_ref[0:HALO_S, :] = jnp.zeros((HALO_S, D_BR), f32)

    def taps(k):
        return dw_ref[0:1, _blk(k)], dw_ref[1:2, _blk(k)], dw_ref[2:3, _blk(k)]

    @pl.when(i < N_PT)
    def _():
        first = lax.rem(i, TILES_PER_SEQ) == 0
        ext_ref[0:HALO_S, :] = jnp.where(first, 0.0, ext_ref[0:HALO_S, :])
        p = _dot(xn_ref[...], wb_ref[...])
        for k in range(nblk):
            w0, w1, w2 = taps(k)
            for c in range(TM // EP_R):
                rows = slice(c * EP_R, (c + 1) * EP_R)
                r0 = HALO_S + c * EP_R
                ext_ref[r0:r0 + EP_R, _blk(k)] = p[rows, _blk(3 * k + 1)] * p[rows, _blk(3 * k + 2)]
                z = (w0 * ext_ref[r0 - 2:r0 - 2 + EP_R, _blk(k)]
                     + w1 * ext_ref[r0 - 1:r0 - 1 + EP_R, _blk(k)]
                     + w2 * ext_ref[r0:r0 + EP_R, _blk(k)])
                act_ref[rows, _blk(k)] = (p[rows, _blk(3 * k)] * z).astype(bf16)
        stp_ref[i // TILES_PER_SEQ] = ext_ref[HALO_S + TM - (SCONV_K - 1):HALO_S + TM, :]
        ext_ref[0:HALO_S, :] = ext_ref[TM:TM + HALO_S, :]

    @pl.when(i == N_PT)
    def _():
        p = _dot(xn_ref[0:M_S, :], wb_ref[...])
        for k in range(nblk):
            w0, w1, w2 = taps(k)
            s = p[:, _blk(3 * k + 1)] * p[:, _blk(3 * k + 2)]
            xp = [hist_ref[:, _blk(k)], hist_ref[:, D_BR + k * LANES:D_BR + (k + 1) * LANES]]
            xp += [s[t * DEC_BATCH:(t + 1) * DEC_BATCH] for t in range(DEC_SEQ)]
            for t in range(DEC_SEQ):
                z = w0 * xp[t] + w1 * xp[t + 1] + w2 * xp[t + 2]
                rows = slice(t * DEC_BATCH, (t + 1) * DEC_BATCH)
                act_ref[rows, _blk(k)] = (p[rows, _blk(3 * k)] * z).astype(bf16)
            sts_ref[:, _blk(k)] = xp[DEC_SEQ]
            sts_ref[:, D_BR + k * LANES:D_BR + (k + 1) * LANES] = xp[DEC_SEQ + 1]


def _stage_c(xn, w_in, sconv_dw, state, l):
    npast = SCONV_K - 1
    hist = state.reshape(DEPTH, DEC_BATCH, npast * D_BR)
    wspec = lambda cb: _once((None, D_MODEL, D_BR), lambda i: (l, 0, cb))
    act, stp, sts = pl.pallas_call(
        _stage_c_kernel,
        grid=(N_TILES,),
        in_specs=[
            pl.BlockSpec((TM, D_MODEL), lambda i: (i, 0)),
            wspec(4), wspec(5), wspec(6),
            pl.BlockSpec((None, SCONV_K, D_BR), lambda i: (l, 0, 0)),
            pl.BlockSpec((None, DEC_BATCH, npast * D_BR), lambda i: (l, 0, 0)),
        ],
        out_specs=[
            pl.BlockSpec((TM, D_BR), lambda i: (i, 0)),
            pl.BlockSpec((BATCH, npast, D_BR), lambda i: (0, 0, 0)),
            pl.BlockSpec((DEC_BATCH, npast * D_BR), lambda i: (0, 0)),
        ],
        out_shape=[
            jax.ShapeDtypeStruct((M_ALL, D_BR), bf16),
            jax.ShapeDtypeStruct((BATCH, npast, D_BR), f32),
            jax.ShapeDtypeStruct((DEC_BATCH, npast * D_BR), f32),
        ],
        scratch_shapes=[
            pltpu.VMEM((D_MODEL, 3 * D_BR), bf16),
            pltpu.VMEM((HALO_S + TM, D_BR), f32),
        ],
        compiler_params=_cparams(1),
        name="branch_c",
    )(xn, w_in, w_in, w_in, sconv_dw, hist)
    return act, stp, sts.reshape(DEC_BATCH, npast, D_BR)


def _stage_d_kernel(xn_ref, w_ref, pw_ref, psc_ref, hist_ref,
                    act_ref, stp_ref, new_ref, wb_ref, pwb_ref, ext_ref):
    i = pl.program_id(0)
    gdim = D_BR // len(POOL_WINDOWS)

    @pl.when(i == 0)
    def _():
        _cast_rows(w_ref, wb_ref)
        pwb_ref[...] = pw_ref[...].astype(bf16)

    @pl.when(i < N_PT)
    def _():
        tis = i % TILES_PER_SEQ

        @pl.when(tis == 0)
        def _():
            ext_ref[0:HALO_P, :] = jnp.zeros((HALO_P, D_BR), f32)

        ext_ref[HALO_P:HALO_P + TM, :] = _dot(xn_ref[...], wb_ref[...])
        pos1 = tis * TM + 1 + lax.broadcasted_iota(jnp.int32, (TM, gdim), 0)
        for g, win in enumerate(POOL_WINDOWS):
            lanes = slice(g * gdim, (g + 1) * gdim)
            cur = ext_ref[HALO_P:HALO_P + TM, lanes]
            tot = cur
            for k in range(1, win):
                tot = tot + ext_ref[HALO_P - k:HALO_P - k + TM, lanes]
            cnt = jnp.minimum(pos1, win).astype(f32)
            pm = (tot / cnt - cur).astype(bf16)
            act_ref[:, lanes] = (_dot(pm, pwb_ref[g]) * psc_ref[:, lanes]).astype(bf16)

        @pl.when(tis == TILES_PER_SEQ - 1)
        def _():
            stp_ref[i // TILES_PER_SEQ] = ext_ref[HALO_P + TM - POOL_PAST:HALO_P + TM, :]

        ext_ref[0:HALO_P, :] = ext_ref[TM:TM + HALO_P, :]

    @pl.when(i == N_PT)
    def _():
        p = _dot(xn_ref[0:M_S, :], wb_ref[...])

        def slab(s, lanes):
            if s < POOL_PAST:
                return hist_ref[s, :, lanes]
            r0 = (s - POOL_PAST) * DEC_BATCH
            return p[r0:r0 + DEC_BATCH, lanes]

        for t in range(DEC_SEQ):
            rows = slice(t * DEC_BATCH, (t + 1) * DEC_BATCH)
            for g, win in enumerate(POOL_WINDOWS):
                lanes = slice(g * gdim, (g + 1) * gdim)
                cur = slab(POOL_PAST + t, lanes)
                tot = cur
                for k in range(1, win):
                    tot = tot + slab(POOL_PAST + t - k, lanes)
                cnt = float(min(win, PAST_LEN + t + 1))
                pm = (tot / cnt - cur).astype(bf16)
                act_ref[rows, lanes] = (_dot(pm, pwb_ref[g]) * psc_ref[:, lanes]).astype(bf16)
        for t in range(DEC_SEQ):
            new_ref[:, t * D_BR:(t + 1) * D_BR] = p[t * DEC_BATCH:(t + 1) * DEC_BATCH]


def _stage_d(xn, w_in, pool_w, pool_scale, state, l):
    ngroup = len(POOL_WINDOWS)
    gdim = D_BR // ngroup
    return pl.pallas_call(
        _stage_d_kernel,
        grid=(N_TILES,),
        in_specs=[
            pl.BlockSpec((TM, D_MODEL), lambda i: (i, 0)),
            _once((None, D_MODEL, D_BR), lambda i: (l, 0, 7)),
            pl.BlockSpec((None, ngroup, gdim, gdim), lambda i: (l, 0, 0, 0)),
            pl.BlockSpec((None, 1, D_BR), lambda i: (l, 0, 0)),
            _once((None, POOL_PAST, DEC_BATCH, D_BR), lambda i: (l, 0, 0, 0)),
        ],
        out_specs=[
            pl.BlockSpec((TM, D_BR), lambda i: (i, 0)),
            pl.BlockSpec((BATCH, POOL_PAST, D_BR), lambda i: (0, 0, 0)),
            pl.BlockSpec((DEC_BATCH, DEC_SEQ * D_BR), lambda i: (0, 0)),
        ],
        out_shape=[
            jax.ShapeDtypeStruct((M_ALL, D_BR), bf16),
            jax.ShapeDtypeStruct((BATCH, POOL_PAST, D_BR), f32),
            jax.ShapeDtypeStruct((DEC_BATCH, DEC_SEQ * D_BR), f32),
        ],
        scratch_shapes=[
            pltpu.VMEM((D_MODEL, D_BR), bf16),
            pltpu.VMEM((ngroup, gdim, gdim), bf16),
            pltpu.VMEM((HALO_P + TM, D_BR), f32),
        ],
        compiler_params=_cparams(1),
        name="branch_d",
    )(xn, w_in, pool_w, pool_scale.reshape(DEPTH, 1, D_BR), _time_major(state))


def _stage_g_kernel(xn_ref, a0_ref, a1_ref, a2_ref, a3_ref,
                    g0_ref, g1_ref, g2_ref, g3_ref, wbr_ref,
                    out_ref, wgb_ref, wbb_ref):
    i = pl.program_id(1)

    @pl.when(i == 0)
    def _():
        for b, g_ref in enumerate((g0_ref, g1_ref, g2_ref, g3_ref)):
            _cast_rows(g_ref, wgb_ref, b * TN_G)
        wbb_ref[...] = wbr_ref[...].astype(bf16)

    def compute(rows):
        gate = _sigmoid(_dot(xn_ref[rows, :], wgb_ref[...]))
        merged = None
        for b, a_ref in enumerate((a0_ref, a1_ref, a2_ref, a3_ref)):
            term = gate[:, b * TN_G:(b + 1) * TN_G] * _dot(a_ref[rows, :], wbb_ref[b])
            merged = term if merged is None else merged + term
        out_ref[rows, :] = merged.astype(bf16)

    @pl.when(i < N_PT)
    def _():
        compute(slice(0, TM))

    @pl.when(i == N_PT)
    def _():
        compute(slice(0, M_S))


def _stage_g(xn, acts, w_in, w_branch, l):
    gate_blk0 = GATE_COL0 // TN_G
    per_branch = D_MODEL // TN_G
    gspec = lambda b: pl.BlockSpec(
        (None, D_MODEL, TN_G), lambda c, i: (l, 0, gate_blk0 + b * per_branch + c))
    aspec = pl.BlockSpec((TM, D_BR), lambda c, i: (i, 0))
    return pl.pallas_call(
        _stage_g_kernel,
        grid=(D_MODEL // TN_G, N_TILES),
        in_specs=[
            pl.BlockSpec((TM, D_MODEL), lambda c, i: (i, 0)),
            aspec, aspec, aspec, aspec,
            gspec(0), gspec(1), gspec(2), gspec(3),
            pl.BlockSpec((None, N_BRANCH, D_BR, TN_G), lambda c, i: (l, 0, 0, c)),
        ],
        out_specs=pl.BlockSpec((TM, TN_G), lambda c, i: (i, c)),
        out_shape=jax.ShapeDtypeStruct((M_ALL, D_MODEL), bf16),
        scratch_shapes=[
            pltpu.VMEM((D_MODEL, N_BRANCH * TN_G), bf16),
            pltpu.VMEM((N_BRANCH, D_BR, TN_G), bf16),
        ],
        compiler_params=_cparams(2),
        name="gate_merge",
    )(xn, *acts, w_in, w_in, w_in, w_in, w_branch)


def _stage_res_kernel(lhs_ref, w_ref, x_ref, o_ref, wb_ref):
    i = pl.program_id(1)

    @pl.when(i == 0)
    def _():
        _cast_rows(w_ref, wb_ref)

    @pl.when(i < N_PT)
    def _():
        o_ref[...] = x_ref[...] + _dot(lhs_ref[...], wb_ref[...])

    @pl.when(i == N_PT)
    def _():
        o_ref[0:M_S, :] = x_ref[0:M_S, :] + _dot(lhs_ref[0:M_S, :], wb_ref[...])


def _stage_res(lhs, w, x, l, tn, name, single_buffer_w):
    k = lhs.shape[1]
    wshape = (None, k, tn)
    wmap = lambda c, i: (l, 0, c)
    wspec = _once(wshape, wmap) if single_buffer_w else pl.BlockSpec(wshape, wmap)
    return pl.pallas_call(
        _stage_res_kernel,
        grid=(D_MODEL // tn, N_TILES),
        in_specs=[
            pl.BlockSpec((TM, k), lambda c, i: (i, 0)),
            wspec,
            pl.BlockSpec((TM, tn), lambda c, i: (i, c)),
        ],
        out_specs=pl.BlockSpec((TM, tn), lambda c, i: (i, c)),
        out_shape=jax.ShapeDtypeStruct((M_ALL, D_MODEL), f32),
        scratch_shapes=[pltpu.VMEM((k, tn), bf16)],
        compiler_params=_cparams(2),
        name=name,
    )(lhs, w, x)


def _stage_o_kernel(first_layer, *refs):
    if first_layer:
        m_ref, w_ref, xp_ref, xs_ref, g_ref, x_ref, xn_ref, wb_ref = refs
    else:
        m_ref, w_ref, xin_ref, g_ref, x_ref, xn_ref, wb_ref = refs
    i = pl.program_id(0)

    @pl.when(i == 0)
    def _():
        _cast_rows(w_ref, wb_ref)

    g = g_ref[...]

    def finish(r0, nrows, resid_rows, proj):
        for c in range(nrows // LN_R):
            rows = slice(r0 + c * LN_R, r0 + (c + 1) * LN_R)
            x = resid_rows(slice(c * LN_R, (c + 1) * LN_R)) + proj[rows]
            x_ref[rows, :] = x
            xn_ref[rows, :] = _rms(x, g).astype(bf16)

    if first_layer:
        @pl.when(i < M_P // TM_O)
        def _():
            finish(0, TM_O, lambda r: xp_ref[r, :], _dot(m_ref[...], wb_ref[...]))

        @pl.when(i >= M_P // TM_O)
        def _():
            proj = _dot(m_ref[...], wb_ref[...])
            for t in range(DEC_SEQ):
                finish(t * DEC_BATCH, DEC_BATCH,
                       lambda r, t=t: xs_ref[r, t, :], proj)
    else:
        finish(0, TM_O, lambda r: xin_ref[r, :], _dot(m_ref[...], wb_ref[...]))


def _stage_o(merged, w_o, norm_w, l, x=None, x_prompt=None, x_sample=None):
    first_layer = x is None
    n_steps = M_ALL // TM_O
    row_blk = pl.BlockSpec((TM_O, D_MODEL), lambda i: (i, 0))
    if first_layer:
        n_p = M_P // TM_O
        resid = [x_prompt.reshape(M_P, D_MODEL), x_sample]
        resid_specs = [
            pl.BlockSpec((TM_O, D_MODEL), lambda i: (jnp.minimum(i, n_p - 1), 0)),
            pl.BlockSpec((DEC_BATCH, DEC_SEQ, D_MODEL), lambda i: (0, 0, 0)),
        ]
    else:
        resid = [x]
        resid_specs = [row_blk]
    return pl.pallas_call(
        functools.partial(_stage_o_kernel, first_layer),
        grid=(n_steps,),
        in_specs=[row_blk, _once((None, D_MODEL, D_MODEL), lambda i: (l, 0, 0))]
        + resid_specs + [pl.BlockSpec((None, 1, D_MODEL), lambda i: (l, 0, 0))],
        out_specs=[row_blk, row_blk],
        out_shape=[
            jax.ShapeDtypeStruct((M_ALL, D_MODEL), f32),
            jax.ShapeDtypeStruct((M_ALL, D_MODEL), bf16),
        ],
        scratch_shapes=[pltpu.VMEM((D_MODEL, D_MODEL), bf16)],
        compiler_params=_cparams(1),
        name="out_proj_norm",
    )(merged, w_o, *resid, norm_w.reshape(DEPTH, 1, D_MODEL))


def _stage_u_kernel(xn_ref, wg_ref, wv_ref, dwg_ref, dwv_ref,
                    hgh_ref, hvh_ref,
                    a_ref, stp_ref, sts_ref, wb_ref, ext_ref):
    i = pl.program_id(1)
    nblk = TK_U // U_BLK
    npast = FFN_K - 1

    def _blk(k):
        return slice(k * U_BLK, (k + 1) * U_BLK)

    @pl.when(i == 0)
    def _():
        moves = []
        for k in range(nblk):
            moves.append((wg_ref, k * U_BLK, 2 * k * U_BLK, U_BLK))
            moves.append((wv_ref, k * U_BLK, (2 * k + 1) * U_BLK, U_BLK))
        _cast_blocks(moves, wb_ref)
        ext_ref[0:HALO_S, :] = jnp.zeros((HALO_S, 2 * TK_U), f32)

    def taps(dw_ref, k):
        return dw_ref[0:1, _blk(k)], dw_ref[1:2, _blk(k)], dw_ref[2:3, _blk(k)]

    @pl.when(i < N_PT)
    def _():
        first = lax.rem(i, TILES_PER_SEQ) == 0
        ext_ref[0:HALO_S, :] = jnp.where(first, 0.0, ext_ref[0:HALO_S, :])
        ext_ref[HALO_S:HALO_S + TM, :] = _dot(xn_ref[...], wb_ref[...])
        for k in range(nblk):
            for c in range(TM // U_R):
                r0 = HALO_S + c * U_R

                def conv(dw_ref, lanes):
                    w0, w1, w2 = taps(dw_ref, k)
                    return (w0 * ext_ref[r0 - 2:r0 - 2 + U_R, lanes]
                            + w1 * ext_ref[r0 - 1:r0 - 1 + U_R, lanes]
                            + w2 * ext_ref[r0:r0 + U_R, lanes])

                hg = conv(dwg_ref, _blk(2 * k))
                hv = conv(dwv_ref, _blk(2 * k + 1))
                a_ref[c * U_R:(c + 1) * U_R, _blk(k)] = (_silu(hg) * hv).astype(bf16)
        b = i // TILES_PER_SEQ
        last = slice(HALO_S + TM - npast, HALO_S + TM)
        for k in range(nblk):
            stp_ref[b, 0, :, _blk(k)] = ext_ref[last, _blk(2 * k)]
            stp_ref[b, 1, :, _blk(k)] = ext_ref[last, _blk(2 * k + 1)]
        ext_ref[0:HALO_S, :] = ext_ref[TM:TM + HALO_S, :]

    @pl.when(i == N_PT)
    def _():
        h = _dot(xn_ref[0:M_S, :], wb_ref[...])

        def conv_slabs(dw_ref, hist_ref, k, lanes):
            w0, w1, w2 = taps(dw_ref, k)
            xp = [hist_ref[:, r, _blk(k)] for r in range(npast)]
            xp += [h[t * DEC_BATCH:(t + 1) * DEC_BATCH, lanes] for t in range(DEC_SEQ)]
            return [w0 * xp[t] + w1 * xp[t + 1] + w2 * xp[t + 2] for t in range(DEC_SEQ)]

        for k in range(nblk):
            hg = conv_slabs(dwg_ref, hgh_ref, k, _blk(2 * k))
            hv = conv_slabs(dwv_ref, hvh_ref, k, _blk(2 * k + 1))
            for t in range(DEC_SEQ):
                a_ref[t * DEC_BATCH:(t + 1) * DEC_BATCH, _blk(k)] = (
                    _silu(hg[t]) * hv[t]).astype(bf16)
            for r in range(npast):
                t = DEC_SEQ - npast + r
                rows = slice(t * DEC_BATCH, (t + 1) * DEC_BATCH)
                sts_ref[r, 0, :, _blk(k)] = h[rows, _blk(2 * k)]
                sts_ref[r, 1, :, _blk(k)] = h[rows, _blk(2 * k + 1)]


def _stage_u(xn, ffn_up, ffn_dw, state, l):
    nk = D_FF // TK_U
    npast = FFN_K - 1
    wspec = lambda half: pl.BlockSpec(
        (None, D_MODEL, TK_U), lambda k, i: (l, 0, half * nk + k))
    dspec = lambda half: pl.BlockSpec(
        (None, FFN_K, TK_U), lambda k, i: (l, 0, half * nk + k))
    hspec = lambda half: pl.BlockSpec(
        (None, DEC_BATCH, npast, TK_U), lambda k, i: (l, 0, 0, half * nk + k))
    a, stp, sts = pl.pallas_call(
        _stage_u_kernel,
        grid=(nk, N_TILES),
        in_specs=[
            pl.BlockSpec((TM, D_MODEL), lambda k, i: (i, 0)),
            wspec(0), wspec(1), dspec(0), dspec(1),
            hspec(0), hspec(1),
        ],
        out_specs=[
            pl.BlockSpec((TM, TK_U), lambda k, i: (i, k)),
            pl.BlockSpec((BATCH, 2, npast, TK_U), lambda k, i: (0, 0, 0, k)),
            pl.BlockSpec((npast, 2, DEC_BATCH, TK_U), lambda k, i: (0, 0, 0, k)),
        ],
        out_shape=[
            jax.ShapeDtypeStruct((M_ALL, D_FF), bf16),
            jax.ShapeDtypeStruct((BATCH, 2, npast, D_FF), f32),
            jax.ShapeDtypeStruct((npast, 2, DEC_BATCH, D_FF), f32),
        ],
        scratch_shapes=[
            pltpu.VMEM((D_MODEL, 2 * TK_U), bf16),
            pltpu.VMEM((HALO_S + TM, 2 * TK_U), f32),
        ],
        compiler_params=_cparams(2),
        name="ffn_up",
    )(xn, ffn_up, ffn_up, ffn_dw, ffn_dw, state, state)
    stp = jnp.transpose(stp, (0, 2, 1, 3)).reshape(BATCH, npast, 2 * D_FF)
    return a, stp, sts


def _ffn_state_kernel(*refs):
    in_refs, out_ref = refs[:-1], refs[-1]
    l = pl.program_id(0)
    for d, in_ref in enumerate(in_refs):
        @pl.when(l == d)
        def _(in_ref=in_ref):
            for r in range(FFN_K - 1):
                out_ref[:, r, :] = in_ref[r]


def _ffn_state(per_layer):
    npast = FFN_K - 1
    ncb = D_FF // FFN_ST_W
    spec = pl.BlockSpec((npast, None, DEC_BATCH, FFN_ST_W), lambda l, h, c: (0, h, 0, c))
    return pl.pallas_call(
        _ffn_state_kernel,
        grid=(DEPTH, 2, ncb),
        in_specs=[spec] * DEPTH,
        out_specs=pl.BlockSpec((None, DEC_BATCH, npast, FFN_ST_W),
                               lambda l, h, c: (l, 0, 0, h * ncb + c)),
        out_shape=jax.ShapeDtypeStruct((DEPTH, DEC_BATCH, npast, 2 * D_FF), f32),
        compiler_params=_cparams(3),
        name="ffn_state",
    )(*per_layer)


def kernel(x_prompt, x_sample, state_conf_conv, state_sconv, state_pool, state_ffn_conv,
           norm_mix, w_in, conf_dw, conf_ln_g, conf_ln_b, gmlp_ln_g, gmlp_ln_b, gmlp_ws,
           gmlp_b, sconv_dw, pool_w, pool_scale, w_branch, w_o, norm_ffn, ffn_up, ffn_dw,
           ffn_down, norm_final):
    conf_p, conf_new, sconv_p, sconv_s, pool_p, pool_new, ffn_p, ffn_new, v_s = (
        [] for _ in range(9))
    x = None
    for l in range(DEPTH):
        if l == 0:
            xn = _norm0(x_prompt, x_sample, norm_mix, l)
        else:
            xn = _norm(x, norm_mix, l)
        act_a, st_p, new = _stage_a(xn, w_in, conf_dw, conf_ln_g, conf_ln_b, state_conf_conv, l)
        conf_p.append(st_p)
        conf_new.append(new)
        act_b, v = _stage_b(xn, w_in, gmlp_ln_g, gmlp_ln_b, gmlp_ws, gmlp_b, l)
        v_s.append(v)
        act_c, st_p, st_s = _stage_c(xn, w_in, sconv_dw, state_sconv, l)
        sconv_p.append(st_p)
        sconv_s.append(st_s)
        act_d, st_p, new = _stage_d(xn, w_in, pool_w, pool_scale, state_pool, l)
        pool_p.append(st_p)
        pool_new.append(new)
        merged = _stage_g(xn, (act_a, act_b, act_c, act_d), w_in, w_branch, l)
        if l == 0:
            x, xn = _stage_o(merged, w_o, norm_ffn, l, x_prompt=x_prompt, x_sample=x_sample)
        else:
            x, xn = _stage_o(merged, w_o, norm_ffn, l, x=x)
        a, st_p, new = _stage_u(xn, ffn_up, ffn_dw, state_ffn_conv, l)
        ffn_p.append(st_p)
        ffn_new.append(new)
        x = _stage_res(a, ffn_down, x, l, TN_D, "ffn_down", True)
    y_prompt, y_sample = _final_norm(x, norm_final)
    conf_s = _roll_state(state_conf_conv, conf_new)
    pool_s = _roll_state(state_pool, pool_new)
    ffn_s = _ffn_state(ffn_new)
    st = jnp.stack
    return (y_prompt, y_sample, st(conf_p), conf_s, st(sconv_p), st(sconv_s),
            st(pool_p), pool_s, st(ffn_p), ffn_s, st(v_s))
```

```python
import functools

import jax
import jax.numpy as jnp
from jax import lax
from jax.experimental import pallas as pl
from jax.experimental.pallas import tpu as pltpu

D_MODEL = 2048
BATCH = 4
SEQ = 2048
DEPTH = 2
DEC_BATCH = 128
DEC_SEQ = 4
PAST_LEN = 16384
D_BR = 512
N_BRANCH = 4
CONF_K = 31
GMLP_HEADS = 4
HEAD_DIM = D_BR // GMLP_HEADS
CHUNK = 128
SCONV_K = 3
POOL_WINDOWS = (2, 4, 8, 16)
POOL_PAST = 15
D_FF = 5632
FFN_K = 3
EPS = 1e-6
GATE_COL0 = 8 * D_BR

TM = 1024
M_P = BATCH * SEQ
M_S = DEC_BATCH * DEC_SEQ
M_ALL = M_P + M_S
N_PT = M_P // TM
TILES_PER_SEQ = SEQ // TM
N_TILES = N_PT + 1

LANES = 128
HALO_A = 32
HALO_P = 16
HALO_S = 8
CONV_R = 128
EP_R = 128
LN_R = 64
ROLL_B = 64
FFN_ST_W = 1408
U_R = 1024
U_BLK = 512

TN_G = 256
TM_O = 512
TM_R = 512
TK_U = 512
TN_D = 512

VMEM_LIMIT = 60000 * 1024

f32 = jnp.float32
bf16 = jnp.bfloat16


def _cparams(n_axes):
    return pltpu.CompilerParams(
        dimension_semantics=("arbitrary",) * n_axes, vmem_limit_bytes=VMEM_LIMIT)


def _once(block_shape, index_map):
    return pl.BlockSpec(block_shape, index_map, pipeline_mode=pl.Buffered(1))


def _rms(x, g):
    return x * lax.rsqrt(jnp.mean(x * x, axis=-1, keepdims=True) + EPS) * g


def _layernorm(x, g, b):
    mu = jnp.mean(x, axis=-1, keepdims=True)
    d = x - mu
    var = jnp.mean(d * d, axis=-1, keepdims=True)
    return d * lax.rsqrt(var + EPS) * g + b


def _sigmoid(x):
    return 1.0 / (1.0 + jnp.exp(-x))


def _silu(x):
    h = 0.5 * x
    return h + h * jnp.tanh(h)


def _gelu_tanh(x):
    c = 0.7978845608028654
    return 0.5 * x * (1.0 + jnp.tanh(c * (x + 0.044715 * (x * x * x))))


def _cast_blocks(moves, wb_ref, rows_per_step=256):
    k = wb_ref.shape[0]

    def body(c, carry):
        r = pl.multiple_of(c * rows_per_step, rows_per_step)
        for w_ref, src, dst, n in moves:
            wb_ref[pl.ds(r, rows_per_step), dst:dst + n] = (
                w_ref[pl.ds(r, rows_per_step), src:src + n].astype(bf16))
        return carry

    lax.fori_loop(0, k // rows_per_step, body, 0)


def _cast_rows(w_ref, wb_ref, col0=0):
    _cast_blocks([(w_ref, 0, col0, w_ref.shape[1])], wb_ref)


def _dot(a, b):
    return jnp.dot(a, b, preferred_element_type=f32)


def _blk(k):
    return slice(k * LANES, (k + 1) * LANES)


def _norm_rows(src_ref, put, rows, chunk=256):
    def body(c, carry):
        r = pl.multiple_of(c * chunk, chunk)
        put(r, chunk, src_ref[pl.ds(r, chunk), :])
        return carry

    lax.fori_loop(0, rows // chunk, body, 0)


def _norm0_kernel(xp_ref, xs_ref, g_ref, xn_ref):
    i = pl.program_id(0)
    g = g_ref[...]

    @pl.when(i < N_PT)
    def _():
        def put(r, n, x):
            xn_ref[pl.ds(r, n), :] = _rms(x, g).astype(bf16)
        _norm_rows(xp_ref, put, TM)

    @pl.when(i == N_PT)
    def _():
        for t in range(DEC_SEQ):
            x = xs_ref[:, t, :]
            xn_ref[t * DEC_BATCH:(t + 1) * DEC_BATCH, :] = _rms(x, g).astype(bf16)


def _norm0(x_prompt, x_sample, norm_w, l):
    xp = x_prompt.reshape(M_P, D_MODEL)
    return pl.pallas_call(
        _norm0_kernel,
        grid=(N_TILES,),
        in_specs=[
            pl.BlockSpec((TM, D_MODEL), lambda i: (jnp.minimum(i, N_PT - 1), 0)),
            pl.BlockSpec((DEC_BATCH, DEC_SEQ, D_MODEL), lambda i: (0, 0, 0)),
            pl.BlockSpec((None, 1, D_MODEL), lambda i: (l, 0, 0)),
        ],
        out_specs=pl.BlockSpec((TM, D_MODEL), lambda i: (i, 0)),
        out_shape=jax.ShapeDtypeStruct((M_ALL, D_MODEL), bf16),
        compiler_params=_cparams(1),
        name="norm0",
    )(xp, x_sample, norm_w.reshape(DEPTH, 1, D_MODEL))


def _norm_kernel(x_ref, g_ref, xn_ref):
    i = pl.program_id(0)
    g = g_ref[...]

    def put(r, n, x):
        xn_ref[pl.ds(r, n), :] = _rms(x, g).astype(bf16)

    @pl.when(i < N_PT)
    def _():
        _norm_rows(x_ref, put, TM)

    @pl.when(i == N_PT)
    def _():
        _norm_rows(x_ref, put, M_S)


def _norm(x, norm_w, l):
    return pl.pallas_call(
        _norm_kernel,
        grid=(N_TILES,),
        in_specs=[
            pl.BlockSpec((TM, D_MODEL), lambda i: (i, 0)),
            pl.BlockSpec((None, 1, D_MODEL), lambda i: (l, 0, 0)),
        ],
        out_specs=pl.BlockSpec((TM, D_MODEL), lambda i: (i, 0)),
        out_shape=jax.ShapeDtypeStruct((M_ALL, D_MODEL), bf16),
        compiler_params=_cparams(1),
        name="norm",
    )(x, norm_w.reshape(DEPTH, 1, D_MODEL))


def _final_norm_kernel(x_ref, g_ref, yp_ref, ys_ref):
    i = pl.program_id(0)
    g = g_ref[...]

    @pl.when(i < N_PT)
    def _():
        def put(r, n, x):
            yp_ref[pl.ds(r, n), :] = _rms(x, g)
        _norm_rows(x_ref, put, TM)

    @pl.when(i == N_PT)
    def _():
        for t in range(DEC_SEQ):
            x = x_ref[t * DEC_BATCH:(t + 1) * DEC_BATCH, :]
            ys_ref[:, t, :] = _rms(x, g)


def _final_norm(x, norm_w):
    yp, ys = pl.pallas_call(
        _final_norm_kernel,
        grid=(N_TILES,),
        in_specs=[
            pl.BlockSpec((TM, D_MODEL), lambda i: (i, 0)),
            pl.BlockSpec((1, D_MODEL), lambda i: (0, 0)),
        ],
        out_specs=[
            pl.BlockSpec((TM, D_MODEL), lambda i: (jnp.minimum(i, N_PT - 1), 0)),
            pl.BlockSpec((DEC_BATCH, DEC_SEQ, D_MODEL), lambda i: (0, 0, 0)),
        ],
        out_shape=[
            jax.ShapeDtypeStruct((M_P, D_MODEL), f32),
            jax.ShapeDtypeStruct((DEC_BATCH, DEC_SEQ, D_MODEL), f32),
        ],
        compiler_params=_cparams(1),
        name="final_norm",
    )(x, norm_w.reshape(1, D_MODEL))
    return yp.reshape(BATCH, SEQ, D_MODEL), ys


def _time_major(state):
    return jnp.transpose(state, (0, 2, 1, 3))


def _roll_state_kernel(old_ref, *refs):
    new_refs, out_ref = refs[:-1], refs[-1]
    l = pl.program_id(0)
    npast, width = old_ref.shape[0], old_ref.shape[2]
    keep = npast - DEC_SEQ
    for s in range(keep):
        out_ref[s] = old_ref[DEC_SEQ + s]
    for d, new_ref in enumerate(new_refs):
        @pl.when(l == d)
        def _(new_ref=new_ref):
            for t in range(DEC_SEQ):
                out_ref[keep + t] = new_ref[:, t * width:(t + 1) * width]


def _roll_state(old, new_rows):
    old_t = _time_major(old)
    npast, width = old_t.shape[1], old_t.shape[3]
    blk = pl.BlockSpec((None, npast, ROLL_B, width), lambda l, j: (l, 0, j, 0))
    nspec = pl.BlockSpec((ROLL_B, DEC_SEQ * width), lambda l, j: (j, 0))
    out = pl.pallas_call(
        _roll_state_kernel,
        grid=(DEPTH, DEC_BATCH // ROLL_B),
        in_specs=[blk] + [nspec] * DEPTH,
        out_specs=blk,
        out_shape=jax.ShapeDtypeStruct(old_t.shape, f32),
        compiler_params=_cparams(2),
        name="roll_state",
    )(old_t, *new_rows)
    return _time_major(out)


def _conv31_chunk(ext_ref, base, dw_ref, lanes):
    y = None
    for r in range(8):
        rows = CONV_R if r == 0 else CONV_R + 8
        z = None
        for a in range(5):
            j = 8 * a + r - 2
            if 0 <= j < CONF_K:
                term = dw_ref[pl.ds(j, 1), lanes] * ext_ref[pl.ds(base + 8 * a, rows), lanes]
                z = term if z is None else z + term
        zr = z[r:r + CONV_R]
        y = zr if y is None else y + zr
    return y


def _stage_a_kernel(xn_ref, w_ref, dw_ref, lng_ref, lnb_ref, hist_ref,
                    act_ref, stp_ref, new_ref, wb_ref, ext_ref, y_ref):
    i = pl.program_id(0)
    nblk = D_BR // LANES
    npast = CONF_K - 1

    @pl.when(i == 0)
    def _():
        moves = []
        for k in range(nblk):
            moves.append((w_ref, k * LANES, 2 * k * LANES, LANES))
            moves.append((w_ref, D_BR + k * LANES, (2 * k + 1) * LANES, LANES))
        _cast_blocks(moves, wb_ref)
        ext_ref[0:HALO_A, :] = jnp.zeros((HALO_A, D_BR), f32)

    lng = lng_ref[...]
    lnb = lnb_ref[...]

    def glu_block(p, k, rows):
        return p[rows, _blk(2 * k)] * _sigmoid(p[rows, _blk(2 * k + 1)])

    @pl.when(i < N_PT)
    def _():
        first = lax.rem(i, TILES_PER_SEQ) == 0
        ext_ref[0:HALO_A, :] = jnp.where(first, 0.0, ext_ref[0:HALO_A, :])
        p = _dot(xn_ref[...], wb_ref[...])
        for k in range(nblk):
            for c in range(TM // CONV_R):
                rows = slice(c * CONV_R, (c + 1) * CONV_R)
                ext_ref[HALO_A + c * CONV_R:HALO_A + (c + 1) * CONV_R, _blk(k)] = (
                    glu_block(p, k, rows))
            for c in range(TM // CONV_R):
                y_ref[c * CONV_R:(c + 1) * CONV_R, _blk(k)] = (
                    _conv31_chunk(ext_ref, c * CONV_R, dw_ref, _blk(k)))
        for c in range(TM // LN_R):
            rows = slice(c * LN_R, (c + 1) * LN_R)
            act_ref[rows, :] = _silu(_layernorm(y_ref[rows, :], lng, lnb)).astype(bf16)
        stp_ref[i // TILES_PER_SEQ] = ext_ref[HALO_A + TM - npast:HALO_A + TM, :]
        ext_ref[0:HALO_A, :] = ext_ref[TM:TM + HALO_A, :]

    @pl.when(i == N_PT)
    def _():
        p = _dot(xn_ref[0:M_S, :], wb_ref[...])
        glu = jnp.concatenate([glu_block(p, k, slice(0, M_S)) for k in range(nblk)], axis=1)
        rb = 16
        for q in range(DEC_BATCH // rb):
            y = [None] * DEC_SEQ
            for s in range(npast + DEC_SEQ):
                if s < npast:
                    slab = hist_ref[s, q * rb:(q + 1) * rb, :]
                else:
                    r0 = (s - npast) * DEC_BATCH + q * rb
                    slab = glu[r0:r0 + rb]
                for t in range(DEC_SEQ):
                    j = s - t
                    if 0 <= j < CONF_K:
                        term = dw_ref[pl.ds(j, 1), :] * slab
                        y[t] = term if y[t] is None else y[t] + term
            for t in range(DEC_SEQ):
                r0 = t * DEC_BATCH + q * rb
                act_ref[r0:r0 + rb, :] = _silu(_layernorm(y[t], lng, lnb)).astype(bf16)
        for t in range(DEC_SEQ):
            new_ref[:, t * D_BR:(t + 1) * D_BR] = glu[t * DEC_BATCH:(t + 1) * DEC_BATCH]


def _stage_a(xn, w_in, conf_dw, ln_g, ln_b, state, l):
    npast = CONF_K - 1
    return pl.pallas_call(
        _stage_a_kernel,
        grid=(N_TILES,),
        in_specs=[
            pl.BlockSpec((TM, D_MODEL), lambda i: (i, 0)),
            _once((None, D_MODEL, 2 * D_BR), lambda i: (l, 0, 0)),
            pl.BlockSpec((None, CONF_K, D_BR), lambda i: (l, 0, 0)),
            pl.BlockSpec((None, 1, D_BR), lambda i: (l, 0, 0)),
            pl.BlockSpec((None, 1, D_BR), lambda i: (l, 0, 0)),
            _once((None, npast, DEC_BATCH, D_BR), lambda i: (l, 0, 0, 0)),
        ],
        out_specs=[
            pl.BlockSpec((TM, D_BR), lambda i: (i, 0)),
            pl.BlockSpec((BATCH, npast, D_BR), lambda i: (0, 0, 0)),
            pl.BlockSpec((DEC_BATCH, DEC_SEQ * D_BR), lambda i: (0, 0)),
        ],
        out_shape=[
            jax.ShapeDtypeStruct((M_ALL, D_BR), bf16),
            jax.ShapeDtypeStruct((BATCH, npast, D_BR), f32),
            jax.ShapeDtypeStruct((DEC_BATCH, DEC_SEQ * D_BR), f32),
        ],
        scratch_shapes=[
            pltpu.VMEM((D_MODEL, 2 * D_BR), bf16),
            pltpu.VMEM((HALO_A + TM, D_BR), f32),
            pltpu.VMEM((TM, D_BR), f32),
        ],
        compiler_params=_cparams(1),
        name="branch_a",
    )(xn, w_in, conf_dw, ln_g.reshape(DEPTH, 1, D_BR), ln_b.reshape(DEPTH, 1, D_BR),
      _time_major(state))


def _stage_b_kernel(xn_ref, w_ref, lng_ref, lnb_ref, ws_ref, bt_ref, wss_ref, bss_ref,
                    act_ref, v_ref, wb_ref, tril_ref, bias_ref):
    i = pl.program_id(0)

    @pl.when(i == 0)
    def _():
        _cast_rows(w_ref, wb_ref)
        row = lax.broadcasted_iota(jnp.int32, (CHUNK, CHUNK), 0)
        col = lax.broadcasted_iota(jnp.int32, (CHUNK, CHUNK), 1)
        for h in range(GMLP_HEADS):
            tril_ref[h] = jnp.where(row >= col, ws_ref[h], 0.0).astype(bf16)
            bias_ref[h] = jnp.broadcast_to(bt_ref[:, h:h + 1], (CHUNK, HEAD_DIM))

    lng = lng_ref[...]
    lnb = lnb_ref[...]

    @pl.when(i < N_PT)
    def _():
        p = _dot(xn_ref[...], wb_ref[...])
        for c in range(TM // CHUNK):
            rows = slice(c * CHUNK, (c + 1) * CHUNK)
            u = _gelu_tanh(p[rows, :D_BR])
            v = _layernorm(_gelu_tanh(p[rows, D_BR:]), lng, lnb).astype(bf16)
            for h in range(GMLP_HEADS):
                lanes = slice(h * HEAD_DIM, (h + 1) * HEAD_DIM)
                mixed = _dot(tril_ref[h], v[:, lanes]) + bias_ref[h]
                act_ref[rows, lanes] = (u[:, lanes] * mixed).astype(bf16)

    @pl.when(i == N_PT)
    def _():
        p = _dot(xn_ref[0:M_S, :], wb_ref[...])
        u = _gelu_tanh(p[:, :D_BR])
        v = _layernorm(_gelu_tanh(p[:, D_BR:]), lng, lnb)
        for t in range(DEC_SEQ):
            v_ref[:, t * D_BR:(t + 1) * D_BR] = v[t * DEC_BATCH:(t + 1) * DEC_BATCH]
        for t in range(DEC_SEQ):
            rows = slice(t * DEC_BATCH, (t + 1) * DEC_BATCH)
            for h in range(GMLP_HEADS):
                lanes = slice(h * HEAD_DIM, (h + 1) * HEAD_DIM)
                mixed = jnp.full((DEC_BATCH, HEAD_DIM), bss_ref[h * DEC_SEQ + t], f32)
                for s in range(t + 1):
                    coef = wss_ref[(h * DEC_SEQ + t) * DEC_SEQ + s]
                    mixed = mixed + coef * v[s * DEC_BATCH:(s + 1) * DEC_BATCH, lanes]
                act_ref[rows, lanes] = (u[rows, lanes] * mixed).astype(bf16)


def _stage_b(xn, w_in, ln_g, ln_b, gmlp_ws, gmlp_b, l):
    bias_t = jnp.swapaxes(gmlp_b, 1, 2)
    ws_small = gmlp_ws[l, :, :DEC_SEQ, :DEC_SEQ].reshape(-1)
    b_small = gmlp_b[l, :, :DEC_SEQ].reshape(-1)
    act, v = pl.pallas_call(
        _stage_b_kernel,
        grid=(N_TILES,),
        in_specs=[
            pl.BlockSpec((TM, D_MODEL), lambda i: (i, 0)),
            _once((None, D_MODEL, 2 * D_BR), lambda i: (l, 0, 1)),
            pl.BlockSpec((None, 1, D_BR), lambda i: (l, 0, 0)),
            pl.BlockSpec((None, 1, D_BR), lambda i: (l, 0, 0)),
            pl.BlockSpec((None, GMLP_HEADS, CHUNK, CHUNK), lambda i: (l, 0, 0, 0)),
            pl.BlockSpec((None, CHUNK, GMLP_HEADS), lambda i: (l, 0, 0)),
            pl.BlockSpec(memory_space=pltpu.SMEM),
            pl.BlockSpec(memory_space=pltpu.SMEM),
        ],
        out_specs=[
            pl.BlockSpec((TM, D_BR), lambda i: (i, 0)),
            pl.BlockSpec((DEC_BATCH, DEC_SEQ * D_BR), lambda i: (0, 0)),
        ],
        out_shape=[
            jax.ShapeDtypeStruct((M_ALL, D_BR), bf16),
            jax.ShapeDtypeStruct((DEC_BATCH, DEC_SEQ * D_BR), f32),
        ],
        scratch_shapes=[
            pltpu.VMEM((D_MODEL, 2 * D_BR), bf16),
            pltpu.VMEM((GMLP_HEADS, CHUNK, CHUNK), bf16),
            pltpu.VMEM((GMLP_HEADS, CHUNK, HEAD_DIM), f32),
        ],
        compiler_params=_cparams(1),
        name="branch_b",
    )(xn, w_in, ln_g.reshape(DEPTH, 1, D_BR), ln_b.reshape(DEPTH, 1, D_BR),
      gmlp_ws, bias_t, ws_small, b_small)
    return act, v.reshape(DEC_BATCH, DEC_SEQ, D_BR)


def _stage_c_kernel(xn_ref, w0_ref, w1_ref, w2_ref, dw_ref, hist_ref,
                    act_ref, stp_ref, sts_ref, wb_ref, ext_ref):
    i = pl.program_id(0)
    nblk = D_BR // LANES

    @pl.when(i == 0)
    def _():
        moves = []
        for k in range(nblk):
            for part, w_ref in enumerate((w0_ref, w1_ref, w2_ref)):
                moves.append((w_ref, k * LANES, (3 * k + part) * LANES, LANES))
        _cast_blocks(moves, wb_ref)
        ext_ref[0:HALO_S, :] = jnp.zeros((HALO_S, D_BR), f32)

    def taps(k):
        return dw_ref[0:1, _blk(k)], dw_ref[1:2, _blk(k)], dw_ref[2:3, _blk(k)]

    @pl.when(i < N_PT)
    def _():
        first = lax.rem(i, TILES_PER_SEQ) == 0
        ext_ref[0:HALO_S, :] = jnp.where(first, 0.0, ext_ref[0:HALO_S, :])
        p = _dot(xn_ref[...], wb_ref[...])
        for k in range(nblk):
            w0, w1, w2 = taps(k)
            for c in range(TM // EP_R):
                rows = slice(c * EP_R, (c + 1) * EP_R)
                r0 = HALO_S + c * EP_R
                ext_ref[r0:r0 + EP_R, _blk(k)] = p[rows, _blk(3 * k + 1)] * p[rows, _blk(3 * k + 2)]
                z = (w0 * ext_ref[r0 - 2:r0 - 2 + EP_R, _blk(k)]
                     + w1 * ext_ref[r0 - 1:r0 - 1 + EP_R, _blk(k)]
                     + w2 * ext_ref[r0:r0 + EP_R, _blk(k)])
                act_ref[rows, _blk(k)] = (p[rows, _blk(3 * k)] * z).astype(bf16)
        stp_ref[i // TILES_PER_SEQ] = ext_ref[HALO_S + TM - (SCONV_K - 1):HALO_S + TM, :]
        ext_ref[0:HALO_S, :] = ext_ref[TM:TM + HALO_S, :]

    @pl.when(i == N_PT)
    def _():
        p = _dot(xn_ref[0:M_S, :], wb_ref[...])
        for k in range(nblk):
            w0, w1, w2 = taps(k)
            s = p[:, _blk(3 * k + 1)] * p[:, _blk(3 * k + 2)]
            xp = [hist_ref[:, _blk(k)], hist_ref[:, D_BR + k * LANES:D_BR + (k + 1) * LANES]]
            xp += [s[t * DEC_BATCH:(t + 1) * DEC_BATCH] for t in range(DEC_SEQ)]
            for t in range(DEC_SEQ):
                z = w0 * xp[t] + w1 * xp[t + 1] + w2 * xp[t + 2]
                rows = slice(t * DEC_BATCH, (t + 1) * DEC_BATCH)
                act_ref[rows, _blk(k)] = (p[rows, _blk(3 * k)] * z).astype(bf16)
            sts_ref[:, _blk(k)] = xp[DEC_SEQ]
            sts_ref[:, D_BR + k * LANES:D_BR + (k + 1) * LANES] = xp[DEC_SEQ + 1]


def _stage_c(xn, w_in, sconv_dw, state, l):
    npast = SCONV_K - 1
    hist = state.reshape(DEPTH, DEC_BATCH, npast * D_BR)
    wspec = lambda cb: _once((None, D_MODEL, D_BR), lambda i: (l, 0, cb))
    act, stp, sts = pl.pallas_call(
        _stage_c_kernel,
        grid=(N_TILES,),
        in_specs=[
            pl.BlockSpec((TM, D_MODEL), lambda i: (i, 0)),
            wspec(4), wspec(5), wspec(6),
            pl.BlockSpec((None, SCONV_K, D_BR), lambda i: (l, 0, 0)),
            pl.BlockSpec((None, DEC_BATCH, npast * D_BR), lambda i: (l, 0, 0)),
        ],
        out_specs=[
            pl.BlockSpec((TM, D_BR), lambda i: (i, 0)),
            pl.BlockSpec((BATCH, npast, D_BR), lambda i: (0, 0, 0)),
            pl.BlockSpec((DEC_BATCH, npast * D_BR), lambda i: (0, 0)),
        ],
        out_shape=[
            jax.ShapeDtypeStruct((M_ALL, D_BR), bf16),
            jax.ShapeDtypeStruct((BATCH, npast, D_BR), f32),
            jax.ShapeDtypeStruct((DEC_BATCH, npast * D_BR), f32),
        ],
        scratch_shapes=[
            pltpu.VMEM((D_MODEL, 3 * D_BR), bf16),
            pltpu.VMEM((HALO_S + TM, D_BR), f32),
        ],
        compiler_params=_cparams(1),
        name="branch_c",
    )(xn, w_in, w_in, w_in, sconv_dw, hist)
    return act, stp, sts.reshape(DEC_BATCH, npast, D_BR)


def _stage_d_kernel(xn_ref, w_ref, pw_ref, psc_ref, hist_ref,
                    act_ref, stp_ref, new_ref, wb_ref, pwb_ref, ext_ref):
    i = pl.program_id(0)
    gdim = D_BR // len(POOL_WINDOWS)

    @pl.when(i == 0)
    def _():
        _cast_rows(w_ref, wb_ref)
        pwb_ref[...] = pw_ref[...].astype(bf16)

    @pl.when(i < N_PT)
    def _():
        tis = i % TILES_PER_SEQ

        @pl.when(tis == 0)
        def _():
            ext_ref[0:HALO_P, :] = jnp.zeros((HALO_P, D_BR), f32)

        ext_ref[HALO_P:HALO_P + TM, :] = _dot(xn_ref[...], wb_ref[...])
        pos1 = tis * TM + 1 + lax.broadcasted_iota(jnp.int32, (TM, gdim), 0)
        for g, win in enumerate(POOL_WINDOWS):
            lanes = slice(g * gdim, (g + 1) * gdim)
            cur = ext_ref[HALO_P:HALO_P + TM, lanes]
            tot = cur
            for k in range(1, win):
                tot = tot + ext_ref[HALO_P - k:HALO_P - k + TM, lanes]
            cnt = jnp.minimum(pos1, win).astype(f32)
            pm = (tot / cnt - cur).astype(bf16)
            act_ref[:, lanes] = (_dot(pm, pwb_ref[g]) * psc_ref[:, lanes]).astype(bf16)

        @pl.when(tis == TILES_PER_SEQ - 1)
        def _():
            stp_ref[i // TILES_PER_SEQ] = ext_ref[HALO_P + TM - POOL_PAST:HALO_P + TM, :]

        ext_ref[0:HALO_P, :] = ext_ref[TM:TM + HALO_P, :]

    @pl.when(i == N_PT)
    def _():
        p = _dot(xn_ref[0:M_S, :], wb_ref[...])

        def slab(s, lanes):
            if s < POOL_PAST:
                return hist_ref[s, :, lanes]
            r0 = (s - POOL_PAST) * DEC_BATCH
            return p[r0:r0 + DEC_BATCH, lanes]

        for t in range(DEC_SEQ):
            rows = slice(t * DEC_BATCH, (t + 1) * DEC_BATCH)
            for g, win in enumerate(POOL_WINDOWS):
                lanes = slice(g * gdim, (g + 1) * gdim)
                cur = slab(POOL_PAST + t, lanes)
                tot = cur
                for k in range(1, win):
                    tot = tot + slab(POOL_PAST + t - k, lanes)
                cnt = float(min(win, PAST_LEN + t + 1))
                pm = (tot / cnt - cur).astype(bf16)
                act_ref[rows, lanes] = (_dot(pm, pwb_ref[g]) * psc_ref[:, lanes]).astype(bf16)
        for t in range(DEC_SEQ):
            new_ref[:, t * D_BR:(t + 1) * D_BR] = p[t * DEC_BATCH:(t + 1) * DEC_BATCH]


def _stage_d(xn, w_in, pool_w, pool_scale, state, l):
    ngroup = len(POOL_WINDOWS)
    gdim = D_BR // ngroup
    return pl.pallas_call(
        _stage_d_kernel,
        grid=(N_TILES,),
        in_specs=[
            pl.BlockSpec((TM, D_MODEL), lambda i: (i, 0)),
            _once((None, D_MODEL, D_BR), lambda i: (l, 0, 7)),
            pl.BlockSpec((None, ngroup, gdim, gdim), lambda i: (l, 0, 0, 0)),
            pl.BlockSpec((None, 1, D_BR), lambda i: (l, 0, 0)),
            _once((None, POOL_PAST, DEC_BATCH, D_BR), lambda i: (l, 0, 0, 0)),
        ],
        out_specs=[
            pl.BlockSpec((TM, D_BR), lambda i: (i, 0)),
            pl.BlockSpec((BATCH, POOL_PAST, D_BR), lambda i: (0, 0, 0)),
            pl.BlockSpec((DEC_BATCH, DEC_SEQ * D_BR), lambda i: (0, 0)),
        ],
        out_shape=[
            jax.ShapeDtypeStruct((M_ALL, D_BR), bf16),
            jax.ShapeDtypeStruct((BATCH, POOL_PAST, D_BR), f32),
            jax.ShapeDtypeStruct((DEC_BATCH, DEC_SEQ * D_BR), f32),
        ],
        scratch_shapes=[
            pltpu.VMEM((D_MODEL, D_BR), bf16),
            pltpu.VMEM((ngroup, gdim, gdim), bf16),
            pltpu.VMEM((HALO_P + TM, D_BR), f32),
        ],
        compiler_params=_cparams(1),
        name="branch_d",
    )(xn, w_in, pool_w, pool_scale.reshape(DEPTH, 1, D_BR), _time_major(state))


def _stage_g_kernel(xn_ref, a0_ref, a1_ref, a2_ref, a3_ref,
                    g0_ref, g1_ref, g2_ref, g3_ref, wbr_ref,
                    out_ref, wgb_ref, wbb_ref):
    i = pl.program_id(1)

    @pl.when(i == 0)
    def _():
        for b, g_ref in enumerate((g0_ref, g1_ref, g2_ref, g3_ref)):
            _cast_rows(g_ref, wgb_ref, b * TN_G)
        wbb_ref[...] = wbr_ref[...].astype(bf16)

    gate = _sigmoid(_dot(xn_ref[...], wgb_ref[...]))
    merged = None
    for b, a_ref in enumerate((a0_ref, a1_ref, a2_ref, a3_ref)):
        term = gate[:, b * TN_G:(b + 1) * TN_G] * _dot(a_ref[...], wbb_ref[b])
        merged = term if merged is None else merged + term
    out_ref[...] = merged.astype(bf16)


def _stage_g(xn, acts, w_in, w_branch, l):
    gate_blk0 = GATE_COL0 // TN_G
    per_branch = D_MODEL // TN_G
    gspec = lambda b: pl.BlockSpec(
        (None, D_MODEL, TN_G), lambda c, i: (l, 0, gate_blk0 + b * per_branch + c))
    aspec = pl.BlockSpec((TM_R, D_BR), lambda c, i: (i, 0))
    return pl.pallas_call(
        _stage_g_kernel,
        grid=(D_MODEL // TN_G, M_ALL // TM_R),
        in_specs=[
            pl.BlockSpec((TM_R, D_MODEL), lambda c, i: (i, 0)),
            aspec, aspec, aspec, aspec,
            gspec(0), gspec(1), gspec(2), gspec(3),
            pl.BlockSpec((None, N_BRANCH, D_BR, TN_G), lambda c, i: (l, 0, 0, c)),
        ],
        out_specs=pl.BlockSpec((TM_R, TN_G), lambda c, i: (i, c)),
        out_shape=jax.ShapeDtypeStruct((M_ALL, D_MODEL), bf16),
        scratch_shapes=[
            pltpu.VMEM((D_MODEL, N_BRANCH * TN_G), bf16),
            pltpu.VMEM((N_BRANCH, D_BR, TN_G), bf16),
        ],
        compiler_params=_cparams(2),
        name="gate_merge",
    )(xn, *acts, w_in, w_in, w_in, w_in, w_branch)


def _stage_res_kernel(lhs_ref, w_ref, x_ref, o_ref, wb_ref):
    i = pl.program_id(1)

    @pl.when(i == 0)
    def _():
        _cast_rows(w_ref, wb_ref)

    o_ref[...] = x_ref[...] + _dot(lhs_ref[...], wb_ref[...])


def _stage_res(lhs, w, x, l, tn, name, single_buffer_w):
    k = lhs.shape[1]
    wshape = (None, k, tn)
    wmap = lambda c, i: (l, 0, c)
    wspec = _once(wshape, wmap) if single_buffer_w else pl.BlockSpec(wshape, wmap)
    return pl.pallas_call(
        _stage_res_kernel,
        grid=(D_MODEL // tn, M_ALL // TM_R),
        in_specs=[
            pl.BlockSpec((TM_R, k), lambda c, i: (i, 0)),
            wspec,
            pl.BlockSpec((TM_R, tn), lambda c, i: (i, c)),
        ],
        out_specs=pl.BlockSpec((TM_R, tn), lambda c, i: (i, c)),
        out_shape=jax.ShapeDtypeStruct((M_ALL, D_MODEL), f32),
        scratch_shapes=[pltpu.VMEM((k, tn), bf16)],
        compiler_params=_cparams(2),
        name=name,
    )(lhs, w, x)


def _stage_o_kernel(first_layer, *refs):
    if first_layer:
        m_ref, w_ref, xp_ref, xs_ref, g_ref, x_ref, xn_ref, wb_ref = refs
    else:
        m_ref, w_ref, xin_ref, g_ref, x_ref, xn_ref, wb_ref = refs
    i = pl.program_id(0)

    @pl.when(i == 0)
    def _():
        _cast_rows(w_ref, wb_ref)

    g = g_ref[...]

    def finish(r0, nrows, resid_rows, proj):
        for c in range(nrows // LN_R):
            rows = slice(r0 + c * LN_R, r0 + (c + 1) * LN_R)
            x = resid_rows(slice(c * LN_R, (c + 1) * LN_R)) + proj[rows]
            x_ref[rows, :] = x
            xn_ref[rows, :] = _rms(x, g).astype(bf16)

    if first_layer:
        @pl.when(i < M_P // TM_O)
        def _():
            finish(0, TM_O, lambda r: xp_ref[r, :], _dot(m_ref[...], wb_ref[...]))

        @pl.when(i >= M_P // TM_O)
        def _():
            proj = _dot(m_ref[...], wb_ref[...])
            for t in range(DEC_SEQ):
                finish(t * DEC_BATCH, DEC_BATCH,
                       lambda r, t=t: xs_ref[r, t, :], proj)
    else:
        finish(0, TM_O, lambda r: xin_ref[r, :], _dot(m_ref[...], wb_ref[...]))


def _stage_o(merged, w_o, norm_w, l, x=None, x_prompt=None, x_sample=None):
    first_layer = x is None
    n_steps = M_ALL // TM_O
    row_blk = pl.BlockSpec((TM_O, D_MODEL), lambda i: (i, 0))
    if first_layer:
        n_p = M_P // TM_O
        resid = [x_prompt.reshape(M_P, D_MODEL), x_sample]
        resid_specs = [
            pl.BlockSpec((TM_O, D_MODEL), lambda i: (jnp.minimum(i, n_p - 1), 0)),
            pl.BlockSpec((DEC_BATCH, DEC_SEQ, D_MODEL), lambda i: (0, 0, 0)),
        ]
    else:
        resid = [x]
        resid_specs = [row_blk]
    return pl.pallas_call(
        functools.partial(_stage_o_kernel, first_layer),
        grid=(n_steps,),
        in_specs=[row_blk, _once((None, D_MODEL, D_MODEL), lambda i: (l, 0, 0))]
        + resid_specs + [pl.BlockSpec((None, 1, D_MODEL), lambda i: (l, 0, 0))],
        out_specs=[row_blk, row_blk],
        out_shape=[
            jax.ShapeDtypeStruct((M_ALL, D_MODEL), f32),
            jax.ShapeDtypeStruct((M_ALL, D_MODEL), bf16),
        ],
        scratch_shapes=[pltpu.VMEM((D_MODEL, D_MODEL), bf16)],
        compiler_params=_cparams(1),
        name="out_proj_norm",
    )(merged, w_o, *resid, norm_w.reshape(DEPTH, 1, D_MODEL))


def _stage_u_kernel(xn_ref, wg_ref, wv_ref, dwg_ref, dwv_ref,
                    hgh_ref, hvh_ref,
                    a_ref, stp_ref, sts_ref, wb_ref, ext_ref):
    i = pl.program_id(1)
    nblk = TK_U // U_BLK
    npast = FFN_K - 1

    def _blk(k):
        return slice(k * U_BLK, (k + 1) * U_BLK)

    @pl.when(i == 0)
    def _():
        moves = []
        for k in range(nblk):
            moves.append((wg_ref, k * U_BLK, 2 * k * U_BLK, U_BLK))
            moves.append((wv_ref, k * U_BLK, (2 * k + 1) * U_BLK, U_BLK))
        _cast_blocks(moves, wb_ref)
        ext_ref[0:HALO_S, :] = jnp.zeros((HALO_S, 2 * TK_U), f32)

    def taps(dw_ref, k):
        return dw_ref[0:1, _blk(k)], dw_ref[1:2, _blk(k)], dw_ref[2:3, _blk(k)]

    @pl.when(i < N_PT)
    def _():
        first = lax.rem(i, TILES_PER_SEQ) == 0
        ext_ref[0:HALO_S, :] = jnp.where(first, 0.0, ext_ref[0:HALO_S, :])
        ext_ref[HALO_S:HALO_S + TM, :] = _dot(xn_ref[...], wb_ref[...])
        for k in range(nblk):
            for c in range(TM // U_R):
                r0 = HALO_S + c * U_R

                def conv(dw_ref, lanes):
                    w0, w1, w2 = taps(dw_ref, k)
                    return (w0 * ext_ref[r0 - 2:r0 - 2 + U_R, lanes]
                            + w1 * ext_ref[r0 - 1:r0 - 1 + U_R, lanes]
                            + w2 * ext_ref[r0:r0 + U_R, lanes])

                hg = conv(dwg_ref, _blk(2 * k))
                hv = conv(dwv_ref, _blk(2 * k + 1))
                a_ref[c * U_R:(c + 1) * U_R, _blk(k)] = (_silu(hg) * hv).astype(bf16)
        b = i // TILES_PER_SEQ
        last = slice(HALO_S + TM - npast, HALO_S + TM)
        for k in range(nblk):
            stp_ref[b, 0, :, _blk(k)] = ext_ref[last, _blk(2 * k)]
            stp_ref[b, 1, :, _blk(k)] = ext_ref[last, _blk(2 * k + 1)]
        ext_ref[0:HALO_S, :] = ext_ref[TM:TM + HALO_S, :]

    @pl.when(i == N_PT)
    def _():
        h = _dot(xn_ref[0:M_S, :], wb_ref[...])

        def conv_slabs(dw_ref, hist_ref, k, lanes):
            w0, w1, w2 = taps(dw_ref, k)
            xp = [hist_ref[:, r, _blk(k)] for r in range(npast)]
            xp += [h[t * DEC_BATCH:(t + 1) * DEC_BATCH, lanes] for t in range(DEC_SEQ)]
            return [w0 * xp[t] + w1 * xp[t + 1] + w2 * xp[t + 2] for t in range(DEC_SEQ)]

        for k in range(nblk):
            hg = conv_slabs(dwg_ref, hgh_ref, k, _blk(2 * k))
            hv = conv_slabs(dwv_ref, hvh_ref, k, _blk(2 * k + 1))
            for t in range(DEC_SEQ):
                a_ref[t * DEC_BATCH:(t + 1) * DEC_BATCH, _blk(k)] = (
                    _silu(hg[t]) * hv[t]).astype(bf16)
            for r in range(npast):
                t = DEC_SEQ - npast + r
                rows = slice(t * DEC_BATCH, (t + 1) * DEC_BATCH)
                sts_ref[r, 0, :, _blk(k)] = h[rows, _blk(2 * k)]
                sts_ref[r, 1, :, _blk(k)] = h[rows, _blk(2 * k + 1)]


def _stage_u(xn, ffn_up, ffn_dw, state, l):
    nk = D_FF // TK_U
    npast = FFN_K - 1
    wspec = lambda half: pl.BlockSpec(
        (None, D_MODEL, TK_U), lambda k, i: (l, 0, half * nk + k))
    dspec = lambda half: pl.BlockSpec(
        (None, FFN_K, TK_U), lambda k, i: (l, 0, half * nk + k))
    hspec = lambda half: pl.BlockSpec(
        (None, DEC_BATCH, npast, TK_U), lambda k, i: (l, 0, 0, half * nk + k))
    a, stp, sts = pl.pallas_call(
        _stage_u_kernel,
        grid=(nk, N_TILES),
        in_specs=[
            pl.BlockSpec((TM, D_MODEL), lambda k, i: (i, 0)),
            wspec(0), wspec(1), dspec(0), dspec(1),
            hspec(0), hspec(1),
        ],
        out_specs=[
            pl.BlockSpec((TM, TK_U), lambda k, i: (i, k)),
            pl.BlockSpec((BATCH, 2, npast, TK_U), lambda k, i: (0, 0, 0, k)),
            pl.BlockSpec((npast, 2, DEC_BATCH, TK_U), lambda k, i: (0, 0, 0, k)),
        ],
        out_shape=[
            jax.ShapeDtypeStruct((M_ALL, D_FF), bf16),
            jax.ShapeDtypeStruct((BATCH, 2, npast, D_FF), f32),
            jax.ShapeDtypeStruct((npast, 2, DEC_BATCH, D_FF), f32),
        ],
        scratch_shapes=[
            pltpu.VMEM((D_MODEL, 2 * TK_U), bf16),
            pltpu.VMEM((HALO_S + TM, 2 * TK_U), f32),
        ],
        compiler_params=_cparams(2),
        name="ffn_up",
    )(xn, ffn_up, ffn_up, ffn_dw, ffn_dw, state, state)
    stp = jnp.transpose(stp, (0, 2, 1, 3)).reshape(BATCH, npast, 2 * D_FF)
    return a, stp, sts


def _ffn_state_kernel(*refs):
    in_refs, out_ref = refs[:-1], refs[-1]
    l = pl.program_id(0)
    for d, in_ref in enumerate(in_refs):
        @pl.when(l == d)
        def _(in_ref=in_ref):
            for r in range(FFN_K - 1):
                out_ref[:, r, :] = in_ref[r]


def _ffn_state(per_layer):
    npast = FFN_K - 1
    ncb = D_FF // FFN_ST_W
    spec = pl.BlockSpec((npast, None, DEC_BATCH, FFN_ST_W), lambda l, h, c: (0, h, 0, c))
    return pl.pallas_call(
        _ffn_state_kernel,
        grid=(DEPTH, 2, ncb),
        in_specs=[spec] * DEPTH,
        out_specs=pl.BlockSpec((None, DEC_BATCH, npast, FFN_ST_W),
                               lambda l, h, c: (l, 0, 0, h * ncb + c)),
        out_shape=jax.ShapeDtypeStruct((DEPTH, DEC_BATCH, npast, 2 * D_FF), f32),
        compiler_params=_cparams(3),
        name="ffn_state",
    )(*per_layer)


def kernel(x_prompt, x_sample, state_conf_conv, state_sconv, state_pool, state_ffn_conv,
           norm_mix, w_in, conf_dw, conf_ln_g, conf_ln_b, gmlp_ln_g, gmlp_ln_b, gmlp_ws,
           gmlp_b, sconv_dw, pool_w, pool_scale, w_branch, w_o, norm_ffn, ffn_up, ffn_dw,
           ffn_down, norm_final):
    conf_p, conf_new, sconv_p, sconv_s, pool_p, pool_new, ffn_p, ffn_new, v_s = (
        [] for _ in range(9))
    x = None
    for l in range(DEPTH):
        if l == 0:
            xn = _norm0(x_prompt, x_sample, norm_mix, l)
        else:
            xn = _norm(x, norm_mix, l)
        act_a, st_p, new = _stage_a(xn, w_in, conf_dw, conf_ln_g, conf_ln_b, state_conf_conv, l)
        conf_p.append(st_p)
        conf_new.append(new)
        act_b, v = _stage_b(xn, w_in, gmlp_ln_g, gmlp_ln_b, gmlp_ws, gmlp_b, l)
        v_s.append(v)
        act_c, st_p, st_s = _stage_c(xn, w_in, sconv_dw, state_sconv, l)
        sconv_p.append(st_p)
        sconv_s.append(st_s)
        act_d, st_p, new = _stage_d(xn, w_in, pool_w, pool_scale, state_pool, l)
        pool_p.append(st_p)
        pool_new.append(new)
        merged = _stage_g(xn, (act_a, act_b, act_c, act_d), w_in, w_branch, l)
        if l == 0:
            x, xn = _stage_o(merged, w_o, norm_ffn, l, x_prompt=x_prompt, x_sample=x_sample)
        else:
            x, xn = _stage_o(merged, w_o, norm_ffn, l, x=x)
        a, st_p, new = _stage_u(xn, ffn_up, ffn_dw, state_ffn_conv, l)
        ffn_p.append(st_p)
        ffn_new.append(new)
        x = _stage_res(a, ffn_down, x, l, TN_D, "ffn_down", False)
    y_prompt, y_sample = _final_norm(x, norm_final)
    conf_s = _roll_state(state_conf_conv, conf_new)
    pool_s = _roll_state(state_pool, pool_new)
    ffn_s = _ffn_state(ffn_new)
    st = jnp.stack
    return (y_prompt, y_sample, st(conf_p), conf_s, st(sconv_p), st(sconv_s),
            st(pool_p), pool_s, st(ffn_p), ffn_s, st(v_s))
```

```python
import functools

import jax
import jax.numpy as jnp
from jax import lax
from jax.experimental import pallas as pl
from jax.experimental.pallas import tpu as pltpu

D_MODEL = 2048
BATCH = 4
SEQ = 2048
DEPTH = 2
DEC_BATCH = 128
DEC_SEQ = 4
PAST_LEN = 16384
D_BR = 512
N_BRANCH = 4
CONF_K = 31
GMLP_HEADS = 4
HEAD_DIM = D_BR // GMLP_HEADS
CHUNK = 128
SCONV_K = 3
POOL_WINDOWS = (2, 4, 8, 16)
POOL_PAST = 15
D_FF = 5632
FFN_K = 3
EPS = 1e-6
GATE_COL0 = 8 * D_BR

TM = 1024
M_P = BATCH * SEQ
M_S = DEC_BATCH * DEC_SEQ
M_ALL = M_P + M_S
N_PT = M_P // TM
TILES_PER_SEQ = SEQ // TM
N_TILES = N_PT + 1

LANES = 128
HALO_A = 32
HALO_P = 16
HALO_S = 8
CONV_R = 128
EP_R = 128
LN_R = 64
ROLL_B = 128
FFN_ST_W = 2816
U_R = 1024
U_BLK = 512

TN_G = 256
TM_O = 512
TK_U = 512
TN_D = 512

VMEM_LIMIT = 60000 * 1024

f32 = jnp.float32
bf16 = jnp.bfloat16


def _cparams(n_axes):
    return pltpu.CompilerParams(
        dimension_semantics=("arbitrary",) * n_axes, vmem_limit_bytes=VMEM_LIMIT)


def _once(block_shape, index_map):
    return pl.BlockSpec(block_shape, index_map, pipeline_mode=pl.Buffered(1))


def _rms(x, g):
    return x * lax.rsqrt(jnp.mean(x * x, axis=-1, keepdims=True) + EPS) * g


def _layernorm(x, g, b):
    mu = jnp.mean(x, axis=-1, keepdims=True)
    d = x - mu
    var = jnp.mean(d * d, axis=-1, keepdims=True)
    return d * lax.rsqrt(var + EPS) * g + b


def _sigmoid(x):
    return 1.0 / (1.0 + jnp.exp(-x))


def _silu(x):
    h = 0.5 * x
    return h + h * jnp.tanh(h)


def _gelu_tanh(x):
    c = 0.7978845608028654
    return 0.5 * x * (1.0 + jnp.tanh(c * (x + 0.044715 * (x * x * x))))


def _cast_blocks(moves, wb_ref, rows_per_step=256):
    k = wb_ref.shape[0]

    def body(c, carry):
        r = pl.multiple_of(c * rows_per_step, rows_per_step)
        for w_ref, src, dst, n in moves:
            wb_ref[pl.ds(r, rows_per_step), dst:dst + n] = (
                w_ref[pl.ds(r, rows_per_step), src:src + n].astype(bf16))
        return carry

    lax.fori_loop(0, k // rows_per_step, body, 0)


def _cast_rows(w_ref, wb_ref, col0=0):
    _cast_blocks([(w_ref, 0, col0, w_ref.shape[1])], wb_ref)


def _dot(a, b):
    return jnp.dot(a, b, preferred_element_type=f32)


def _blk(k):
    return slice(k * LANES, (k + 1) * LANES)


def _norm_rows(src_ref, put, rows, chunk=256):
    def body(c, carry):
        r = pl.multiple_of(c * chunk, chunk)
        put(r, chunk, src_ref[pl.ds(r, chunk), :])
        return carry

    lax.fori_loop(0, rows // chunk, body, 0)


def _norm0_kernel(xp_ref, xs_ref, g_ref, xn_ref):
    i = pl.program_id(0)
    g = g_ref[...]

    @pl.when(i < N_PT)
    def _():
        def put(r, n, x):
            xn_ref[pl.ds(r, n), :] = _rms(x, g).astype(bf16)
        _norm_rows(xp_ref, put, TM)

    @pl.when(i == N_PT)
    def _():
        for t in range(DEC_SEQ):
            x = xs_ref[:, t, :]
            xn_ref[t * DEC_BATCH:(t + 1) * DEC_BATCH, :] = _rms(x, g).astype(bf16)


def _norm0(x_prompt, x_sample, norm_w, l):
    xp = x_prompt.reshape(M_P, D_MODEL)
    return pl.pallas_call(
        _norm0_kernel,
        grid=(N_TILES,),
        in_specs=[
            pl.BlockSpec((TM, D_MODEL), lambda i: (jnp.minimum(i, N_PT - 1), 0)),
            pl.BlockSpec((DEC_BATCH, DEC_SEQ, D_MODEL), lambda i: (0, 0, 0)),
            pl.BlockSpec((None, 1, D_MODEL), lambda i: (l, 0, 0)),
        ],
        out_specs=pl.BlockSpec((TM, D_MODEL), lambda i: (i, 0)),
        out_shape=jax.ShapeDtypeStruct((M_ALL, D_MODEL), bf16),
        compiler_params=_cparams(1),
        name="norm0",
    )(xp, x_sample, norm_w.reshape(DEPTH, 1, D_MODEL))


def _norm_kernel(x_ref, g_ref, xn_ref):
    i = pl.program_id(0)
    g = g_ref[...]

    def put(r, n, x):
        xn_ref[pl.ds(r, n), :] = _rms(x, g).astype(bf16)

    @pl.when(i < N_PT)
    def _():
        _norm_rows(x_ref, put, TM)

    @pl.when(i == N_PT)
    def _():
        _norm_rows(x_ref, put, M_S)


def _norm(x, norm_w, l):
    return pl.pallas_call(
        _norm_kernel,
        grid=(N_TILES,),
        in_specs=[
            pl.BlockSpec((TM, D_MODEL), lambda i: (i, 0)),
            pl.BlockSpec((None, 1, D_MODEL), lambda i: (l, 0, 0)),
        ],
        out_specs=pl.BlockSpec((TM, D_MODEL), lambda i: (i, 0)),
        out_shape=jax.ShapeDtypeStruct((M_ALL, D_MODEL), bf16),
        compiler_params=_cparams(1),
        name="norm",
    )(x, norm_w.reshape(DEPTH, 1, D_MODEL))


def _final_norm_kernel(x_ref, g_ref, yp_ref, ys_ref):
    i = pl.program_id(0)
    g = g_ref[...]

    @pl.when(i < N_PT)
    def _():
        def put(r, n, x):
            yp_ref[pl.ds(r, n), :] = _rms(x, g)
        _norm_rows(x_ref, put, TM)

    @pl.when(i == N_PT)
    def _():
        for t in range(DEC_SEQ):
            x = x_ref[t * DEC_BATCH:(t + 1) * DEC_BATCH, :]
            ys_ref[:, t, :] = _rms(x, g)


def _final_norm(x, norm_w):
    yp, ys = pl.pallas_call(
        _final_norm_kernel,
        grid=(N_TILES,),
        in_specs=[
            pl.BlockSpec((TM, D_MODEL), lambda i: (i, 0)),
            pl.BlockSpec((1, D_MODEL), lambda i: (0, 0)),
        ],
        out_specs=[
            pl.BlockSpec((TM, D_MODEL), lambda i: (jnp.minimum(i, N_PT - 1), 0)),
            pl.BlockSpec((DEC_BATCH, DEC_SEQ, D_MODEL), lambda i: (0, 0, 0)),
        ],
        out_shape=[
            jax.ShapeDtypeStruct((M_P, D_MODEL), f32),
            jax.ShapeDtypeStruct((DEC_BATCH, DEC_SEQ, D_MODEL), f32),
        ],
        compiler_params=_cparams(1),
        name="final_norm",
    )(x, norm_w.reshape(1, D_MODEL))
    return yp.reshape(BATCH, SEQ, D_MODEL), ys


def _time_major(state):
    return jnp.transpose(state, (0, 2, 1, 3))


def _roll_state_kernel(old_ref, *refs):
    new_refs, out_ref = refs[:-1], refs[-1]
    l = pl.program_id(0)
    npast, width = old_ref.shape[0], old_ref.shape[2]
    keep = npast - DEC_SEQ
    for s in range(keep):
        out_ref[s] = old_ref[DEC_SEQ + s]
    for d, new_ref in enumerate(new_refs):
        @pl.when(l == d)
        def _(new_ref=new_ref):
            for t in range(DEC_SEQ):
                out_ref[keep + t] = new_ref[:, t * width:(t + 1) * width]


def _roll_state(old, new_rows):
    old_t = _time_major(old)
    npast, width = old_t.shape[1], old_t.shape[3]
    blk = pl.BlockSpec((None, npast, ROLL_B, width), lambda l, j: (l, 0, j, 0))
    nspec = pl.BlockSpec((ROLL_B, DEC_SEQ * width), lambda l, j: (j, 0))
    out = pl.pallas_call(
        _roll_state_kernel,
        grid=(DEPTH, DEC_BATCH // ROLL_B),
        in_specs=[blk] + [nspec] * DEPTH,
        out_specs=blk,
        out_shape=jax.ShapeDtypeStruct(old_t.shape, f32),
        compiler_params=_cparams(2),
        name="roll_state",
    )(old_t, *new_rows)
    return _time_major(out)


def _conv31_chunk(ext_ref, base, dw_ref, lanes):
    y = None
    for r in range(8):
        rows = CONV_R if r == 0 else CONV_R + 8
        z = None
        for a in range(5):
            j = 8 * a + r - 2
            if 0 <= j < CONF_K:
                term = dw_ref[pl.ds(j, 1), lanes] * ext_ref[pl.ds(base + 8 * a, rows), lanes]
                z = term if z is None else z + term
        zr = z[r:r + CONV_R]
        y = zr if y is None else y + zr
    return y


def _stage_a_kernel(xn_ref, w_ref, dw_ref, lng_ref, lnb_ref, hist_ref,
                    act_ref, stp_ref, new_ref, wb_ref, ext_ref, y_ref):
    i = pl.program_id(0)
    nblk = D_BR // LANES
    npast = CONF_K - 1

    @pl.when(i == 0)
    def _():
        moves = []
        for k in range(nblk):
            moves.append((w_ref, k * LANES, 2 * k * LANES, LANES))
            moves.append((w_ref, D_BR + k * LANES, (2 * k + 1) * LANES, LANES))
        _cast_blocks(moves, wb_ref)
        ext_ref[0:HALO_A, :] = jnp.zeros((HALO_A, D_BR), f32)

    lng = lng_ref[...]
    lnb = lnb_ref[...]

    def glu_block(p, k, rows):
        return p[rows, _blk(2 * k)] * _sigmoid(p[rows, _blk(2 * k + 1)])

    @pl.when(i < N_PT)
    def _():
        first = lax.rem(i, TILES_PER_SEQ) == 0
        ext_ref[0:HALO_A, :] = jnp.where(first, 0.0, ext_ref[0:HALO_A, :])
        p = _dot(xn_ref[...], wb_ref[...])
        for k in range(nblk):
            for c in range(TM // CONV_R):
                rows = slice(c * CONV_R, (c + 1) * CONV_R)
                ext_ref[HALO_A + c * CONV_R:HALO_A + (c + 1) * CONV_R, _blk(k)] = (
                    glu_block(p, k, rows))
            for c in range(TM // CONV_R):
                y_ref[c * CONV_R:(c + 1) * CONV_R, _blk(k)] = (
                    _conv31_chunk(ext_ref, c * CONV_R, dw_ref, _blk(k)))
        for c in range(TM // LN_R):
            rows = slice(c * LN_R, (c + 1) * LN_R)
            act_ref[rows, :] = _silu(_layernorm(y_ref[rows, :], lng, lnb)).astype(bf16)
        stp_ref[i // TILES_PER_SEQ] = ext_ref[HALO_A + TM - npast:HALO_A + TM, :]
        ext_ref[0:HALO_A, :] = ext_ref[TM:TM + HALO_A, :]

    @pl.when(i == N_PT)
    def _():
        p = _dot(xn_ref[0:M_S, :], wb_ref[...])
        glu = jnp.concatenate([glu_block(p, k, slice(0, M_S)) for k in range(nblk)], axis=1)
        rb = 16
        for q in range(DEC_BATCH // rb):
            y = [None] * DEC_SEQ
            for s in range(npast + DEC_SEQ):
                if s < npast:
                    slab = hist_ref[s, q * rb:(q + 1) * rb, :]
                else:
                    r0 = (s - npast) * DEC_BATCH + q * rb
                    slab = glu[r0:r0 + rb]
                for t in range(DEC_SEQ):
                    j = s - t
                    if 0 <= j < CONF_K:
                        term = dw_ref[pl.ds(j, 1), :] * slab
                        y[t] = term if y[t] is None else y[t] + term
            for t in range(DEC_SEQ):
                r0 = t * DEC_BATCH + q * rb
                act_ref[r0:r0 + rb, :] = _silu(_layernorm(y[t], lng, lnb)).astype(bf16)
        for t in range(DEC_SEQ):
            new_ref[:, t * D_BR:(t + 1) * D_BR] = glu[t * DEC_BATCH:(t + 1) * DEC_BATCH]


def _stage_a(xn, w_in, conf_dw, ln_g, ln_b, state, l):
    npast = CONF_K - 1
    return pl.pallas_call(
        _stage_a_kernel,
        grid=(N_TILES,),
        in_specs=[
            pl.BlockSpec((TM, D_MODEL), lambda i: (i, 0)),
            _once((None, D_MODEL, 2 * D_BR), lambda i: (l, 0, 0)),
            pl.BlockSpec((None, CONF_K, D_BR), lambda i: (l, 0, 0)),
            pl.BlockSpec((None, 1, D_BR), lambda i: (l, 0, 0)),
            pl.BlockSpec((None, 1, D_BR), lambda i: (l, 0, 0)),
            _once((None, npast, DEC_BATCH, D_BR), lambda i: (l, 0, 0, 0)),
        ],
        out_specs=[
            pl.BlockSpec((TM, D_BR), lambda i: (i, 0)),
            pl.BlockSpec((BATCH, npast, D_BR), lambda i: (0, 0, 0)),
            pl.BlockSpec((DEC_BATCH, DEC_SEQ * D_BR), lambda i: (0, 0)),
        ],
        out_shape=[
            jax.ShapeDtypeStruct((M_ALL, D_BR), bf16),
            jax.ShapeDtypeStruct((BATCH, npast, D_BR), f32),
            jax.ShapeDtypeStruct((DEC_BATCH, DEC_SEQ * D_BR), f32),
        ],
        scratch_shapes=[
            pltpu.VMEM((D_MODEL, 2 * D_BR), bf16),
            pltpu.VMEM((HALO_A + TM, D_BR), f32),
            pltpu.VMEM((TM, D_BR), f32),
        ],
        compiler_params=_cparams(1),
        name="branch_a",
    )(xn, w_in, conf_dw, ln_g.reshape(DEPTH, 1, D_BR), ln_b.reshape(DEPTH, 1, D_BR),
      _time_major(state))


def _stage_b_kernel(xn_ref, w_ref, lng_ref, lnb_ref, ws_ref, bt_ref, wss_ref, bss_ref,
                    act_ref, v_ref, wb_ref, tril_ref, bias_ref):
    i = pl.program_id(0)

    @pl.when(i == 0)
    def _():
        _cast_blocks([(w_ref, D_BR, 0, D_BR), (w_ref, 0, D_BR, D_BR)], wb_ref)
        row = lax.broadcasted_iota(jnp.int32, (CHUNK, CHUNK), 0)
        col = lax.broadcasted_iota(jnp.int32, (CHUNK, CHUNK), 1)
        for h in range(GMLP_HEADS):
            tril_ref[h] = jnp.where(row >= col, ws_ref[h], 0.0).astype(bf16)
            bias_ref[h] = jnp.broadcast_to(bt_ref[:, h:h + 1], (CHUNK, HEAD_DIM))

    lng = lng_ref[...]
    lnb = lnb_ref[...]

    @pl.when(i < N_PT)
    def _():
        p = _dot(xn_ref[...], wb_ref[...])
        for c in range(TM // CHUNK):
            rows = slice(c * CHUNK, (c + 1) * CHUNK)
            u = _gelu_tanh(p[rows, D_BR:])
            v = _layernorm(_gelu_tanh(p[rows, :D_BR]), lng, lnb).astype(bf16)
            for h in range(GMLP_HEADS):
                lanes = slice(h * HEAD_DIM, (h + 1) * HEAD_DIM)
                mixed = _dot(tril_ref[h], v[:, lanes]) + bias_ref[h]
                act_ref[rows, lanes] = (u[:, lanes] * mixed).astype(bf16)

    @pl.when(i == N_PT)
    def _():
        p = _dot(xn_ref[0:M_S, :], wb_ref[...])
        u = _gelu_tanh(p[:, D_BR:])
        v = _layernorm(_gelu_tanh(p[:, :D_BR]), lng, lnb)
        for t in range(DEC_SEQ):
            v_ref[:, t * D_BR:(t + 1) * D_BR] = v[t * DEC_BATCH:(t + 1) * DEC_BATCH]
        for t in range(DEC_SEQ):
            rows = slice(t * DEC_BATCH, (t + 1) * DEC_BATCH)
            for h in range(GMLP_HEADS):
                lanes = slice(h * HEAD_DIM, (h + 1) * HEAD_DIM)
                mixed = jnp.full((DEC_BATCH, HEAD_DIM), bss_ref[h * DEC_SEQ + t], f32)
                for s in range(t + 1):
                    coef = wss_ref[(h * DEC_SEQ + t) * DEC_SEQ + s]
                    mixed = mixed + coef * v[s * DEC_BATCH:(s + 1) * DEC_BATCH, lanes]
                act_ref[rows, lanes] = (u[rows, lanes] * mixed).astype(bf16)


def _stage_b(xn, w_in, ln_g, ln_b, gmlp_ws, gmlp_b, l):
    bias_t = jnp.swapaxes(gmlp_b, 1, 2)
    ws_small = gmlp_ws[l, :, :DEC_SEQ, :DEC_SEQ].reshape(-1)
    b_small = gmlp_b[l, :, :DEC_SEQ].reshape(-1)
    act, v = pl.pallas_call(
        _stage_b_kernel,
        grid=(N_TILES,),
        in_specs=[
            pl.BlockSpec((TM, D_MODEL), lambda i: (i, 0)),
            _once((None, D_MODEL, 2 * D_BR), lambda i: (l, 0, 1)),
            pl.BlockSpec((None, 1, D_BR), lambda i: (l, 0, 0)),
            pl.BlockSpec((None, 1, D_BR), lambda i: (l, 0, 0)),
            pl.BlockSpec((None, GMLP_HEADS, CHUNK, CHUNK), lambda i: (l, 0, 0, 0)),
            pl.BlockSpec((None, CHUNK, GMLP_HEADS), lambda i: (l, 0, 0)),
            pl.BlockSpec(memory_space=pltpu.SMEM),
            pl.BlockSpec(memory_space=pltpu.SMEM),
        ],
        out_specs=[
            pl.BlockSpec((TM, D_BR), lambda i: (i, 0)),
            pl.BlockSpec((DEC_BATCH, DEC_SEQ * D_BR), lambda i: (0, 0)),
        ],
        out_shape=[
            jax.ShapeDtypeStruct((M_ALL, D_BR), bf16),
            jax.ShapeDtypeStruct((DEC_BATCH, DEC_SEQ * D_BR), f32),
        ],
        scratch_shapes=[
            pltpu.VMEM((D_MODEL, 2 * D_BR), bf16),
            pltpu.VMEM((GMLP_HEADS, CHUNK, CHUNK), bf16),
            pltpu.VMEM((GMLP_HEADS, CHUNK, HEAD_DIM), f32),
        ],
        compiler_params=_cparams(1),
        name="branch_b",
    )(xn, w_in, ln_g.reshape(DEPTH, 1, D_BR), ln_b.reshape(DEPTH, 1, D_BR),
      gmlp_ws, bias_t, ws_small, b_small)
    return act, v.reshape(DEC_BATCH, DEC_SEQ, D_BR)


def _stage_c_kernel(xn_ref, w0_ref, w1_ref, w2_ref, dw_ref, hist_ref,
                    act_ref, stp_ref, sts_ref, wb_ref, ext_ref):
    i = pl.program_id(0)
    nblk = D_BR // LANES

    @pl.when(i == 0)
    def _():
        moves = []
        for k in range(nblk):
            for part, w_ref in enumerate((w0_ref, w1_ref, w2_ref)):
                moves.append((w_ref, k * LANES, (3 * k + part) * LANES, LANES))
        _cast_blocks(moves, wb_ref)
        ext_ref[0:HALO_S, :] = jnp.zeros((HALO_S, D_BR), f32)

    def taps(k):
        return dw_ref[0:1, _blk(k)], dw_ref[1:2, _blk(k)], dw_ref[2:3, _blk(k)]

    @pl.when(i < N_PT)
    def _():
        first = lax.rem(i, TILES_PER_SEQ) == 0
        ext_ref[0:HALO_S, :] = jnp.where(first, 0.0, ext_ref[0:HALO_S, :])
        p = _dot(xn_ref[...], wb_ref[...])
        for k in range(nblk):
            w0, w1, w2 = taps(k)
            for c in range(TM // EP_R):
                rows = slice(c * EP_R, (c + 1) * EP_R)
                r0 = HALO_S + c * EP_R
                ext_ref[r0:r0 + EP_R, _blk(k)] = p[rows, _blk(3 * k + 1)] * p[rows, _blk(3 * k + 2)]
                z = (w0 * ext_ref[r0 - 2:r0 - 2 + EP_R, _blk(k)]
                     + w1 * ext_ref[r0 - 1:r0 - 1 + EP_R, _blk(k)]
                     + w2 * ext_ref[r0:r0 + EP_R, _blk(k)])
                act_ref[rows, _blk(k)] = (p[rows, _blk(3 * k)] * z).astype(bf16)
        stp_ref[i // TILES_PER_SEQ] = ext_ref[HALO_S + TM - (SCONV_K - 1):HALO_S + TM, :]
        ext_ref[0:HALO_S, :] = ext_ref[TM:TM + HALO_S, :]

    @pl.when(i == N_PT)
    def _():
        p = _dot(xn_ref[0:M_S, :], wb_ref[...])
        for k in range(nblk):
            w0, w1, w2 = taps(k)
            s = p[:, _blk(3 * k + 1)] * p[:, _blk(3 * k + 2)]
            xp = [hist_ref[:, _blk(k)], hist_ref[:, D_BR + k * LANES:D_BR + (k + 1) * LANES]]
            xp += [s[t * DEC_BATCH:(t + 1) * DEC_BATCH] for t in range(DEC_SEQ)]
            for t in range(DEC_SEQ):
                z = w0 * xp[t] + w1 * xp[t + 1] + w2 * xp[t + 2]
                rows = slice(t * DEC_BATCH, (t + 1) * DEC_BATCH)
                act_ref[rows, _blk(k)] = (p[rows, _blk(3 * k)] * z).astype(bf16)
            sts_ref[:, _blk(k)] = xp[DEC_SEQ]
            sts_ref[:, D_BR + k * LANES:D_BR + (k + 1) * LANES] = xp[DEC_SEQ + 1]


def _stage_c(xn, w_in, sconv_dw, state, l):
    npast = SCONV_K - 1
    hist = state.reshape(DEPTH, DEC_BATCH, npast * D_BR)
    wspec = lambda cb: _once((None, D_MODEL, D_BR), lambda i: (l, 0, cb))
    act, stp, sts = pl.pallas_call(
        _stage_c_kernel,
        grid=(N_TILES,),
        in_specs=[
            pl.BlockSpec((TM, D_MODEL), lambda i: (i, 0)),
            wspec(4), wspec(5), wspec(6),
            pl.BlockSpec((None, SCONV_K, D_BR), lambda i: (l, 0, 0)),
            pl.BlockSpec((None, DEC_BATCH, npast * D_BR), lambda i: (l, 0, 0)),
        ],
        out_specs=[
            pl.BlockSpec((TM, D_BR), lambda i: (i, 0)),
            pl.BlockSpec((BATCH, npast, D_BR), lambda i: (0, 0, 0)),
            pl.BlockSpec((DEC_BATCH, npast * D_BR), lambda i: (0, 0)),
        ],
        out_shape=[
            jax.ShapeDtypeStruct((M_ALL, D_BR), bf16),
            jax.ShapeDtypeStruct((BATCH, npast, D_BR), f32),
            jax.ShapeDtypeStruct((DEC_BATCH, npast * D_BR), f32),
        ],
        scratch_shapes=[
            pltpu.VMEM((D_MODEL, 3 * D_BR), bf16),
            pltpu.VMEM((HALO_S + TM, D_BR), f32),
        ],
        compiler_params=_cparams(1),
        name="branch_c",
    )(xn, w_in, w_in, w_in, sconv_dw, hist)
    return act, stp, sts.reshape(DEC_BATCH, npast, D_BR)


def _stage_d_kernel(xn_ref, w_ref, pw_ref, psc_ref, hist_ref,
                    act_ref, stp_ref, new_ref, wb_ref, pwb_ref, ext_ref):
    i = pl.program_id(0)
    gdim = D_BR // len(POOL_WINDOWS)

    @pl.when(i == 0)
    def _():
        _cast_rows(w_ref, wb_ref)
        pwb_ref[...] = pw_ref[...].astype(bf16)

    @pl.when(i < N_PT)
    def _():
        tis = i % TILES_PER_SEQ

        @pl.when(tis == 0)
        def _():
            ext_ref[0:HALO_P, :] = jnp.zeros((HALO_P, D_BR), f32)

        ext_ref[HALO_P:HALO_P + TM, :] = _dot(xn_ref[...], wb_ref[...])
        pos1 = tis * TM + 1 + lax.broadcasted_iota(jnp.int32, (TM, gdim), 0)
        for g, win in enumerate(POOL_WINDOWS):
            lanes = slice(g * gdim, (g + 1) * gdim)
            cur = ext_ref[HALO_P:HALO_P + TM, lanes]
            tot = cur
            for k in range(1, win):
                tot = tot + ext_ref[HALO_P - k:HALO_P - k + TM, lanes]
            cnt = jnp.minimum(pos1, win).astype(f32)
            pm = (tot / cnt - cur).astype(bf16)
            act_ref[:, lanes] = (_dot(pm, pwb_ref[g]) * psc_ref[:, lanes]).astype(bf16)

        @pl.when(tis == TILES_PER_SEQ - 1)
        def _():
            stp_ref[i // TILES_PER_SEQ] = ext_ref[HALO_P + TM - POOL_PAST:HALO_P + TM, :]

        ext_ref[0:HALO_P, :] = ext_ref[TM:TM + HALO_P, :]

    @pl.when(i == N_PT)
    def _():
        p = _dot(xn_ref[0:M_S, :], wb_ref[...])

        def slab(s, lanes):
            if s < POOL_PAST:
                return hist_ref[s, :, lanes]
            r0 = (s - POOL_PAST) * DEC_BATCH
            return p[r0:r0 + DEC_BATCH, lanes]

        for t in range(DEC_SEQ):
            rows = slice(t * DEC_BATCH, (t + 1) * DEC_BATCH)
            for g, win in enumerate(POOL_WINDOWS):
                lanes = slice(g * gdim, (g + 1) * gdim)
                cur = slab(POOL_PAST + t, lanes)
                tot = cur
                for k in range(1, win):
                    tot = tot + slab(POOL_PAST + t - k, lanes)
                cnt = float(min(win, PAST_LEN + t + 1))
                pm = (tot / cnt - cur).astype(bf16)
                act_ref[rows, lanes] = (_dot(pm, pwb_ref[g]) * psc_ref[:, lanes]).astype(bf16)
        for t in range(DEC_SEQ):
            new_ref[:, t * D_BR:(t + 1) * D_BR] = p[t * DEC_BATCH:(t + 1) * DEC_BATCH]


def _stage_d(xn, w_in, pool_w, pool_scale, state, l):
    ngroup = len(POOL_WINDOWS)
    gdim = D_BR // ngroup
    return pl.pallas_call(
        _stage_d_kernel,
        grid=(N_TILES,),
        in_specs=[
            pl.BlockSpec((TM, D_MODEL), lambda i: (i, 0)),
            _once((None, D_MODEL, D_BR), lambda i: (l, 0, 7)),
            pl.BlockSpec((None, ngroup, gdim, gdim), lambda i: (l, 0, 0, 0)),
            pl.BlockSpec((None, 1, D_BR), lambda i: (l, 0, 0)),
            _once((None, POOL_PAST, DEC_BATCH, D_BR), lambda i: (l, 0, 0, 0)),
        ],
        out_specs=[
            pl.BlockSpec((TM, D_BR), lambda i: (i, 0)),
            pl.BlockSpec((BATCH, POOL_PAST, D_BR), lambda i: (0, 0, 0)),
            pl.BlockSpec((DEC_BATCH, DEC_SEQ * D_BR), lambda i: (0, 0)),
        ],
        out_shape=[
            jax.ShapeDtypeStruct((M_ALL, D_BR), bf16),
            jax.ShapeDtypeStruct((BATCH, POOL_PAST, D_BR), f32),
            jax.ShapeDtypeStruct((DEC_BATCH, DEC_SEQ * D_BR), f32),
        ],
        scratch_shapes=[
            pltpu.VMEM((D_MODEL, D_BR), bf16),
            pltpu.VMEM((ngroup, gdim, gdim), bf16),
            pltpu.VMEM((HALO_P + TM, D_BR), f32),
        ],
        compiler_params=_cparams(1),
        name="branch_d",
    )(xn, w_in, pool_w, pool_scale.reshape(DEPTH, 1, D_BR), _time_major(state))


def _stage_g_kernel(xn_ref, a0_ref, a1_ref, a2_ref, a3_ref,
                    g0_ref, g1_ref, g2_ref, g3_ref, wbr_ref,
                    out_ref, wgb_ref, wbb_ref):
    i = pl.program_id(1)

    @pl.when(i == 0)
    def _():
        for b, g_ref in enumerate((g0_ref, g1_ref, g2_ref, g3_ref)):
            _cast_rows(g_ref, wgb_ref, b * TN_G)
        wbb_ref[...] = wbr_ref[...].astype(bf16)

    def compute(rows):
        merged = None
        for b, a_ref in enumerate((a0_ref, a1_ref, a2_ref, a3_ref)):
            gate = _sigmoid(_dot(xn_ref[rows, :], wgb_ref[:, b * TN_G:(b + 1) * TN_G]))
            term = gate * _dot(a_ref[rows, :], wbb_ref[b])
            merged = term if merged is None else merged + term
        out_ref[rows, :] = merged.astype(bf16)

    @pl.when(i < N_PT)
    def _():
        compute(slice(0, TM))

    @pl.when(i == N_PT)
    def _():
        compute(slice(0, M_S))


def _stage_g(xn, acts, w_in, w_branch, l):
    gate_blk0 = GATE_COL0 // TN_G
    per_branch = D_MODEL // TN_G
    gspec = lambda b: pl.BlockSpec(
        (None, D_MODEL, TN_G), lambda c, i: (l, 0, gate_blk0 + b * per_branch + c))
    aspec = pl.BlockSpec((TM, D_BR), lambda c, i: (i, 0))
    return pl.pallas_call(
        _stage_g_kernel,
        grid=(D_MODEL // TN_G, N_TILES),
        in_specs=[
            pl.BlockSpec((TM, D_MODEL), lambda c, i: (i, 0)),
            aspec, aspec, aspec, aspec,
            gspec(0), gspec(1), gspec(2), gspec(3),
            pl.BlockSpec((None, N_BRANCH, D_BR, TN_G), lambda c, i: (l, 0, 0, c)),
        ],
        out_specs=pl.BlockSpec((TM, TN_G), lambda c, i: (i, c)),
        out_shape=jax.ShapeDtypeStruct((M_ALL, D_MODEL), bf16),
        scratch_shapes=[
            pltpu.VMEM((D_MODEL, N_BRANCH * TN_G), bf16),
            pltpu.VMEM((N_BRANCH, D_BR, TN_G), bf16),
        ],
        compiler_params=_cparams(2),
        name="gate_merge",
    )(xn, *acts, w_in, w_in, w_in, w_in, w_branch)


def _stage_res_kernel(lhs_ref, w_ref, x_ref, o_ref, wb_ref):
    i = pl.program_id(1)

    @pl.when(i == 0)
    def _():
        _cast_rows(w_ref, wb_ref)

    @pl.when(i < N_PT)
    def _():
        o_ref[...] = x_ref[...] + _dot(lhs_ref[...], wb_ref[...])

    @pl.when(i == N_PT)
    def _():
        o_ref[0:M_S, :] = x_ref[0:M_S, :] + _dot(lhs_ref[0:M_S, :], wb_ref[...])


def _stage_res(lhs, w, x, l, tn, name, single_buffer_w):
    k = lhs.shape[1]
    wshape = (None, k, tn)
    wmap = lambda c, i: (l, 0, c)
    wspec = _once(wshape, wmap) if single_buffer_w else pl.BlockSpec(wshape, wmap)
    return pl.pallas_call(
        _stage_res_kernel,
        grid=(D_MODEL // tn, N_TILES),
        in_specs=[
            pl.BlockSpec((TM, k), lambda c, i: (i, 0)),
            wspec,
            pl.BlockSpec((TM, tn), lambda c, i: (i, c)),
        ],
        out_specs=pl.BlockSpec((TM, tn), lambda c, i: (i, c)),
        out_shape=jax.ShapeDtypeStruct((M_ALL, D_MODEL), f32),
        scratch_shapes=[pltpu.VMEM((k, tn), bf16)],
        compiler_params=_cparams(2),
        name=name,
    )(lhs, w, x)


def _stage_o_kernel(first_layer, *refs):
    if first_layer:
        m_ref, w_ref, xp_ref, xs_ref, g_ref, x_ref, xn_ref, wb_ref = refs
    else:
        m_ref, w_ref, xin_ref, g_ref, x_ref, xn_ref, wb_ref = refs
    i = pl.program_id(0)

    @pl.when(i == 0)
    def _():
        _cast_rows(w_ref, wb_ref)

    g = g_ref[...]

    def finish(r0, nrows, resid_rows, proj):
        for c in range(nrows // LN_R):
            rows = slice(r0 + c * LN_R, r0 + (c + 1) * LN_R)
            x = resid_rows(slice(c * LN_R, (c + 1) * LN_R)) + proj[rows]
            x_ref[rows, :] = x
            xn_ref[rows, :] = _rms(x, g).astype(bf16)

    if first_layer:
        @pl.when(i < M_P // TM_O)
        def _():
            finish(0, TM_O, lambda r: xp_ref[r, :], _dot(m_ref[...], wb_ref[...]))

        @pl.when(i >= M_P // TM_O)
        def _():
            proj = _dot(m_ref[...], wb_ref[...])
            for t in range(DEC_SEQ):
                finish(t * DEC_BATCH, DEC_BATCH,
                       lambda r, t=t: xs_ref[r, t, :], proj)
    else:
        finish(0, TM_O, lambda r: xin_ref[r, :], _dot(m_ref[...], wb_ref[...]))


def _stage_o(merged, w_o, norm_w, l, x=None, x_prompt=None, x_sample=None):
    first_layer = x is None
    n_steps = M_ALL // TM_O
    row_blk = pl.BlockSpec((TM_O, D_MODEL), lambda i: (i, 0))
    if first_layer:
        n_p = M_P // TM_O
        resid = [x_prompt.reshape(M_P, D_MODEL), x_sample]
        resid_specs = [
            pl.BlockSpec((TM_O, D_MODEL), lambda i: (jnp.minimum(i, n_p - 1), 0)),
            pl.BlockSpec((DEC_BATCH, DEC_SEQ, D_MODEL), lambda i: (0, 0, 0)),
        ]
    else:
        resid = [x]
        resid_specs = [row_blk]
    return pl.pallas_call(
        functools.partial(_stage_o_kernel, first_layer),
        grid=(n_steps,),
        in_specs=[row_blk, _once((None, D_MODEL, D_MODEL), lambda i: (l, 0, 0))]
        + resid_specs + [pl.BlockSpec((None, 1, D_MODEL), lambda i: (l, 0, 0))],
        out_specs=[row_blk, row_blk],
        out_shape=[
            jax.ShapeDtypeStruct((M_ALL, D_MODEL), f32),
            jax.ShapeDtypeStruct((M_ALL, D_MODEL), bf16),
        ],
        scratch_shapes=[pltpu.VMEM((D_MODEL, D_MODEL), bf16)],
        compiler_params=_cparams(1),
        name="out_proj_norm",
    )(merged, w_o, *resid, norm_w.reshape(DEPTH, 1, D_MODEL))


def _stage_u_kernel(xn_ref, wg_ref, wv_ref, dwg_ref, dwv_ref,
                    hgh_ref, hvh_ref,
                    a_ref, stp_ref, sts_ref, wb_ref, ext_ref):
    i = pl.program_id(1)
    nblk = TK_U // U_BLK
    npast = FFN_K - 1

    def _blk(k):
        return slice(k * U_BLK, (k + 1) * U_BLK)

    @pl.when(i == 0)
    def _():
        moves = []
        for k in range(nblk):
            moves.append((wg_ref, k * U_BLK, 2 * k * U_BLK, U_BLK))
            moves.append((wv_ref, k * U_BLK, (2 * k + 1) * U_BLK, U_BLK))
        _cast_blocks(moves, wb_ref)
        ext_ref[0:HALO_S, :] = jnp.zeros((HALO_S, 2 * TK_U), f32)

    def taps(dw_ref, k):
        return dw_ref[0:1, _blk(k)], dw_ref[1:2, _blk(k)], dw_ref[2:3, _blk(k)]

    @pl.when(i < N_PT)
    def _():
        first = lax.rem(i, TILES_PER_SEQ) == 0
        ext_ref[0:HALO_S, :] = jnp.where(first, 0.0, ext_ref[0:HALO_S, :])
        h = _dot(xn_ref[...], wb_ref[...])
        ext_ref[HALO_S:HALO_S + TM, :] = h
        for k in range(nblk):
            for c in range(TM // U_R):
                r0 = HALO_S + c * U_R

                def conv(dw_ref, lanes):
                    w0, w1, w2 = taps(dw_ref, k)
                    return (w0 * ext_ref[r0 - 2:r0 - 2 + U_R, lanes]
                            + w1 * ext_ref[r0 - 1:r0 - 1 + U_R, lanes]
                            + w2 * h[c * U_R:(c + 1) * U_R, lanes])

                hg = conv(dwg_ref, _blk(2 * k))
                hv = conv(dwv_ref, _blk(2 * k + 1))
                a_ref[c * U_R:(c + 1) * U_R, _blk(k)] = (_silu(hg) * hv).astype(bf16)
        b = i // TILES_PER_SEQ
        last = slice(HALO_S + TM - npast, HALO_S + TM)
        for k in range(nblk):
            stp_ref[b, 0, :, _blk(k)] = ext_ref[last, _blk(2 * k)]
            stp_ref[b, 1, :, _blk(k)] = ext_ref[last, _blk(2 * k + 1)]
        ext_ref[0:HALO_S, :] = ext_ref[TM:TM + HALO_S, :]

    @pl.when(i == N_PT)
    def _():
        h = _dot(xn_ref[0:M_S, :], wb_ref[...])

        def conv_slabs(dw_ref, hist_ref, k, lanes):
            w0, w1, w2 = taps(dw_ref, k)
            xp = [hist_ref[:, r, _blk(k)] for r in range(npast)]
            xp += [h[t * DEC_BATCH:(t + 1) * DEC_BATCH, lanes] for t in range(DEC_SEQ)]
            return [w0 * xp[t] + w1 * xp[t + 1] + w2 * xp[t + 2] for t in range(DEC_SEQ)]

        for k in range(nblk):
            hg = conv_slabs(dwg_ref, hgh_ref, k, _blk(2 * k))
            hv = conv_slabs(dwv_ref, hvh_ref, k, _blk(2 * k + 1))
            for t in range(DEC_SEQ):
                a_ref[t * DEC_BATCH:(t + 1) * DEC_BATCH, _blk(k)] = (
                    _silu(hg[t]) * hv[t]).astype(bf16)
            for r in range(npast):
                t = DEC_SEQ - npast + r
                rows = slice(t * DEC_BATCH, (t + 1) * DEC_BATCH)
                sts_ref[r, 0, :, _blk(k)] = h[rows, _blk(2 * k)]
                sts_ref[r, 1, :, _blk(k)] = h[rows, _blk(2 * k + 1)]


def _stage_u(xn, ffn_up, ffn_dw, state, l):
    nk = D_FF // TK_U
    npast = FFN_K - 1
    wspec = lambda half: pl.BlockSpec(
        (None, D_MODEL, TK_U), lambda k, i: (l, 0, half * nk + k))
    dspec = lambda half: pl.BlockSpec(
        (None, FFN_K, TK_U), lambda k, i: (l, 0, half * nk + k))
    hspec = lambda half: pl.BlockSpec(
        (None, DEC_BATCH, npast, TK_U), lambda k, i: (l, 0, 0, half * nk + k))
    a, stp, sts = pl.pallas_call(
        _stage_u_kernel,
        grid=(nk, N_TILES),
        in_specs=[
            pl.BlockSpec((TM, D_MODEL), lambda k, i: (i, 0)),
            wspec(0), wspec(1), dspec(0), dspec(1),
            hspec(0), hspec(1),
        ],
        out_specs=[
            pl.BlockSpec((TM, TK_U), lambda k, i: (i, k)),
            pl.BlockSpec((BATCH, 2, npast, TK_U), lambda k, i: (0, 0, 0, k)),
            pl.BlockSpec((npast, 2, DEC_BATCH, TK_U), lambda k, i: (0, 0, 0, k)),
        ],
        out_shape=[
            jax.ShapeDtypeStruct((M_ALL, D_FF), bf16),
            jax.ShapeDtypeStruct((BATCH, 2, npast, D_FF), f32),
            jax.ShapeDtypeStruct((npast, 2, DEC_BATCH, D_FF), f32),
        ],
        scratch_shapes=[
            pltpu.VMEM((D_MODEL, 2 * TK_U), bf16),
            pltpu.VMEM((HALO_S + TM, 2 * TK_U), f32),
        ],
        compiler_params=_cparams(2),
        name="ffn_up",
    )(xn, ffn_up, ffn_up, ffn_dw, ffn_dw, state, state)
    stp = jnp.transpose(stp, (0, 2, 1, 3)).reshape(BATCH, npast, 2 * D_FF)
    return a, stp, sts


def _ffn_state_kernel(*refs):
    in_refs, out_ref = refs[:-1], refs[-1]
    l = pl.program_id(0)
    for d, in_ref in enumerate(in_refs):
        @pl.when(l == d)
        def _(in_ref=in_ref):
            for r in range(FFN_K - 1):
                out_ref[:, r, :] = in_ref[r]


def _ffn_state(per_layer):
    npast = FFN_K - 1
    ncb = D_FF // FFN_ST_W
    spec = pl.BlockSpec((npast, None, DEC_BATCH, FFN_ST_W), lambda l, h, c: (0, h, 0, c))
    return pl.pallas_call(
        _ffn_state_kernel,
        grid=(DEPTH, 2, ncb),
        in_specs=[spec] * DEPTH,
        out_specs=pl.BlockSpec((None, DEC_BATCH, npast, FFN_ST_W),
                               lambda l, h, c: (l, 0, 0, h * ncb + c)),
        out_shape=jax.ShapeDtypeStruct((DEPTH, DEC_BATCH, npast, 2 * D_FF), f32),
        compiler_params=_cparams(3),
        name="ffn_state",
    )(*per_layer)


def kernel(x_prompt, x_sample, state_conf_conv, state_sconv, state_pool, state_ffn_conv,
           norm_mix, w_in, conf_dw, conf_ln_g, conf_ln_b, gmlp_ln_g, gmlp_ln_b, gmlp_ws,
           gmlp_b, sconv_dw, pool_w, pool_scale, w_branch, w_o, norm_ffn, ffn_up, ffn_dw,
           ffn_down, norm_final):
    conf_p, conf_new, sconv_p, sconv_s, pool_p, pool_new, ffn_p, ffn_new, v_s = (
        [] for _ in range(9))
    x = None
    for l in range(DEPTH):
        if l == 0:
            xn = _norm0(x_prompt, x_sample, norm_mix, l)
        else:
            xn = _norm(x, norm_mix, l)
        act_a, st_p, new = _stage_a(xn, w_in, conf_dw, conf_ln_g, conf_ln_b, state_conf_conv, l)
        conf_p.append(st_p)
        conf_new.append(new)
        act_b, v = _stage_b(xn, w_in, gmlp_ln_g, gmlp_ln_b, gmlp_ws, gmlp_b, l)
        v_s.append(v)
        act_c, st_p, st_s = _stage_c(xn, w_in, sconv_dw, state_sconv, l)
        sconv_p.append(st_p)
        sconv_s.append(st_s)
        act_d, st_p, new = _stage_d(xn, w_in, pool_w, pool_scale, state_pool, l)
        pool_p.append(st_p)
        pool_new.append(new)
        merged = _stage_g(xn, (act_a, act_b, act_c, act_d), w_in, w_branch, l)
        if l == 0:
            x, xn = _stage_o(merged, w_o, norm_ffn, l, x_prompt=x_prompt, x_sample=x_sample)
        else:
            x, xn = _stage_o(merged, w_o, norm_ffn, l, x=x)
        a, st_p, new = _stage_u(xn, ffn_up, ffn_dw, state_ffn_conv, l)
        ffn_p.append(st_p)
        ffn_new.append(new)
        x = _stage_res(a, ffn_down, x, l, TN_D, "ffn_down", True)
    y_prompt, y_sample = _final_norm(x, norm_final)
    conf_s = _roll_state(state_conf_conv, conf_new)
    pool_s = _roll_state(state_pool, pool_new)
    ffn_s = _ffn_state(ffn_new)
    st = jnp.stack
    return (y_prompt, y_sample, st(conf_p), conf_s, st(sconv_p), st(sconv_s),
            st(pool_p), pool_s, st(ffn_p), ffn_s, st(v_s))
```

```python
import functools

import jax
import jax.numpy as jnp
from jax import lax
from jax.experimental import pallas as pl
from jax.experimental.pallas import tpu as pltpu

D_MODEL = 2048
BATCH = 4
SEQ = 2048
DEPTH = 2
DEC_BATCH = 128
DEC_SEQ = 4
PAST_LEN = 16384
D_BR = 512
N_BRANCH = 4
CONF_K = 31
GMLP_HEADS = 4
HEAD_DIM = D_BR // GMLP_HEADS
CHUNK = 128
SCONV_K = 3
POOL_WINDOWS = (2, 4, 8, 16)
POOL_PAST = 15
D_FF = 5632
FFN_K = 3
EPS = 1e-6
GATE_COL0 = 8 * D_BR

TM = 1024
M_P = BATCH * SEQ
M_S = DEC_BATCH * DEC_SEQ
M_ALL = M_P + M_S
N_PT = M_P // TM
TILES_PER_SEQ = SEQ // TM
N_TILES = N_PT + 1

LANES = 128
HALO_A = 32
HALO_P = 16
HALO_S = 8
CONV_R = 128
EP_R = 128
LN_R = 64
ROLL_B = 128
FFN_ST_W = 2816
U_R = 1024
U_BLK = 512

TN_G = 256
TM_O = 512
TK_U = 512
TN_D = 512

VMEM_LIMIT = 60000 * 1024

f32 = jnp.float32
bf16 = jnp.bfloat16


def _cparams(n_axes):
    return pltpu.CompilerParams(
        dimension_semantics=("arbitrary",) * n_axes, vmem_limit_bytes=VMEM_LIMIT)


def _once(block_shape, index_map):
    return pl.BlockSpec(block_shape, index_map, pipeline_mode=pl.Buffered(1))


def _rms(x, g):
    return x * lax.rsqrt(jnp.mean(x * x, axis=-1, keepdims=True) + EPS) * g


def _layernorm(x, g, b):
    mu = jnp.mean(x, axis=-1, keepdims=True)
    d = x - mu
    var = jnp.mean(d * d, axis=-1, keepdims=True)
    return d * lax.rsqrt(var + EPS) * g + b


def _sigmoid(x):
    return 1.0 / (1.0 + jnp.exp(-x))


def _silu(x):
    h = 0.5 * x
    return h + h * jnp.tanh(h)


def _gelu_tanh(x):
    c = 0.7978845608028654
    return 0.5 * x * (1.0 + jnp.tanh(c * (x + 0.044715 * (x * x * x))))


def _cast_blocks(moves, wb_ref, rows_per_step=256):
    k = wb_ref.shape[0]

    def body(c, carry):
        r = pl.multiple_of(c * rows_per_step, rows_per_step)
        for w_ref, src, dst, n in moves:
            wb_ref[pl.ds(r, rows_per_step), dst:dst + n] = (
                w_ref[pl.ds(r, rows_per_step), src:src + n].astype(bf16))
        return carry

    lax.fori_loop(0, k // rows_per_step, body, 0)


def _cast_rows(w_ref, wb_ref, col0=0):
    _cast_blocks([(w_ref, 0, col0, w_ref.shape[1])], wb_ref)


def _dot(a, b):
    return jnp.dot(a, b, preferred_element_type=f32)


def _blk(k):
    return slice(k * LANES, (k + 1) * LANES)


def _norm_rows(src_ref, put, rows, chunk=256):
    def body(c, carry):
        r = pl.multiple_of(c * chunk, chunk)
        put(r, chunk, src_ref[pl.ds(r, chunk), :])
        return carry

    lax.fori_loop(0, rows // chunk, body, 0)


def _final_norm_kernel(x_ref, g_ref, yp_ref, ys_ref):
    i = pl.program_id(0)
    g = g_ref[...]

    @pl.when(i < N_PT)
    def _():
        def put(r, n, x):
            yp_ref[pl.ds(r, n), :] = _rms(x, g)
        _norm_rows(x_ref, put, TM)

    @pl.when(i == N_PT)
    def _():
        for t in range(DEC_SEQ):
            x = x_ref[t * DEC_BATCH:(t + 1) * DEC_BATCH, :]
            ys_ref[:, t, :] = _rms(x, g)


def _final_norm(x, norm_w):
    yp, ys = pl.pallas_call(
        _final_norm_kernel,
        grid=(N_TILES,),
        in_specs=[
            pl.BlockSpec((TM, D_MODEL), lambda i: (i, 0)),
            pl.BlockSpec((1, D_MODEL), lambda i: (0, 0)),
        ],
        out_specs=[
            pl.BlockSpec((TM, D_MODEL), lambda i: (jnp.minimum(i, N_PT - 1), 0)),
            pl.BlockSpec((DEC_BATCH, DEC_SEQ, D_MODEL), lambda i: (0, 0, 0)),
        ],
        out_shape=[
            jax.ShapeDtypeStruct((M_P, D_MODEL), f32),
            jax.ShapeDtypeStruct((DEC_BATCH, DEC_SEQ, D_MODEL), f32),
        ],
        compiler_params=_cparams(1),
        name="final_norm",
    )(x, norm_w.reshape(1, D_MODEL))
    return yp.reshape(BATCH, SEQ, D_MODEL), ys


def _time_major(state):
    return jnp.transpose(state, (0, 2, 1, 3))


def _roll_state_kernel(old_ref, *refs):
    new_refs, out_ref = refs[:-1], refs[-1]
    l = pl.program_id(0)
    npast, width = old_ref.shape[0], old_ref.shape[2]
    keep = npast - DEC_SEQ
    for s in range(keep):
        out_ref[s] = old_ref[DEC_SEQ + s]
    for d, new_ref in enumerate(new_refs):
        @pl.when(l == d)
        def _(new_ref=new_ref):
            for t in range(DEC_SEQ):
                out_ref[keep + t] = new_ref[:, t * width:(t + 1) * width]


def _roll_state(old, new_rows):
    old_t = _time_major(old)
    npast, width = old_t.shape[1], old_t.shape[3]
    blk = pl.BlockSpec((None, npast, ROLL_B, width), lambda l, j: (l, 0, j, 0))
    nspec = pl.BlockSpec((ROLL_B, DEC_SEQ * width), lambda l, j: (j, 0))
    out = pl.pallas_call(
        _roll_state_kernel,
        grid=(DEPTH, DEC_BATCH // ROLL_B),
        in_specs=[blk] + [nspec] * DEPTH,
        out_specs=blk,
        out_shape=jax.ShapeDtypeStruct(old_t.shape, f32),
        compiler_params=_cparams(2),
        name="roll_state",
    )(old_t, *new_rows)
    return _time_major(out)


def _conv31_chunk(ext_ref, base, dw_ref, lanes):
    y = None
    for r in range(8):
        rows = CONV_R if r == 0 else CONV_R + 8
        z = None
        for a in range(5):
            j = 8 * a + r - 2
            if 0 <= j < CONF_K:
                term = dw_ref[pl.ds(j, 1), lanes] * ext_ref[pl.ds(base + 8 * a, rows), lanes]
                z = term if z is None else z + term
        zr = z[r:r + CONV_R]
        y = zr if y is None else y + zr
    return y


def _stage_a_kernel(xn_ref, w_ref, dw_ref, lng_ref, lnb_ref, hist_ref,
                    act_ref, stp_ref, new_ref, wb_ref, ext_ref, y_ref):
    i = pl.program_id(0)
    nblk = D_BR // LANES
    npast = CONF_K - 1

    @pl.when(i == 0)
    def _():
        moves = []
        for k in range(nblk):
            moves.append((w_ref, k * LANES, 2 * k * LANES, LANES))
            moves.append((w_ref, D_BR + k * LANES, (2 * k + 1) * LANES, LANES))
        _cast_blocks(moves, wb_ref)
        ext_ref[0:HALO_A, :] = jnp.zeros((HALO_A, D_BR), f32)

    lng = lng_ref[...]
    lnb = lnb_ref[...]

    def glu_block(p, k, rows):
        return p[rows, _blk(2 * k)] * _sigmoid(p[rows, _blk(2 * k + 1)])

    @pl.when(i < N_PT)
    def _():
        first = lax.rem(i, TILES_PER_SEQ) == 0
        ext_ref[0:HALO_A, :] = jnp.where(first, 0.0, ext_ref[0:HALO_A, :])
        p = _dot(xn_ref[...], wb_ref[...])
        for k in range(nblk):
            for c in range(TM // CONV_R):
                rows = slice(c * CONV_R, (c + 1) * CONV_R)
                ext_ref[HALO_A + c * CONV_R:HALO_A + (c + 1) * CONV_R, _blk(k)] = (
                    glu_block(p, k, rows))
            for c in range(TM // CONV_R):
                y_ref[c * CONV_R:(c + 1) * CONV_R, _blk(k)] = (
                    _conv31_chunk(ext_ref, c * CONV_R, dw_ref, _blk(k)))
        for c in range(TM // LN_R):
            rows = slice(c * LN_R, (c + 1) * LN_R)
            act_ref[rows, :] = _silu(_layernorm(y_ref[rows, :], lng, lnb)).astype(bf16)
        stp_ref[i // TILES_PER_SEQ] = ext_ref[HALO_A + TM - npast:HALO_A + TM, :]
        ext_ref[0:HALO_A, :] = ext_ref[TM:TM + HALO_A, :]

    @pl.when(i == N_PT)
    def _():
        p = _dot(xn_ref[0:M_S, :], wb_ref[...])
        glu = jnp.concatenate([glu_block(p, k, slice(0, M_S)) for k in range(nblk)], axis=1)
        rb = 16
        for q in range(DEC_BATCH // rb):
            y = [None] * DEC_SEQ
            for s in range(npast + DEC_SEQ):
                if s < npast:
                    slab = hist_ref[s, q * rb:(q + 1) * rb, :]
                else:
                    r0 = (s - npast) * DEC_BATCH + q * rb
                    slab = glu[r0:r0 + rb]
                for t in range(DEC_SEQ):
                    j = s - t
                    if 0 <= j < CONF_K:
                        term = dw_ref[pl.ds(j, 1), :] * slab
                        y[t] = term if y[t] is None else y[t] + term
            for t in range(DEC_SEQ):
                r0 = t * DEC_BATCH + q * rb
                act_ref[r0:r0 + rb, :] = _silu(_layernorm(y[t], lng, lnb)).astype(bf16)
        for t in range(DEC_SEQ):
            new_ref[:, t * D_BR:(t + 1) * D_BR] = glu[t * DEC_BATCH:(t + 1) * DEC_BATCH]


def _stage_a(xn, w_in, conf_dw, ln_g, ln_b, state, l):
    npast = CONF_K - 1
    return pl.pallas_call(
        _stage_a_kernel,
        grid=(N_TILES,),
        in_specs=[
            pl.BlockSpec((TM, D_MODEL), lambda i: (i, 0)),
            _once((None, D_MODEL, 2 * D_BR), lambda i: (l, 0, 0)),
            pl.BlockSpec((None, CONF_K, D_BR), lambda i: (l, 0, 0)),
            pl.BlockSpec((None, 1, D_BR), lambda i: (l, 0, 0)),
            pl.BlockSpec((None, 1, D_BR), lambda i: (l, 0, 0)),
            _once((None, npast, DEC_BATCH, D_BR), lambda i: (l, 0, 0, 0)),
        ],
        out_specs=[
            pl.BlockSpec((TM, D_BR), lambda i: (i, 0)),
            pl.BlockSpec((BATCH, npast, D_BR), lambda i: (0, 0, 0)),
            pl.BlockSpec((DEC_BATCH, DEC_SEQ * D_BR), lambda i: (0, 0)),
        ],
        out_shape=[
            jax.ShapeDtypeStruct((M_ALL, D_BR), bf16),
            jax.ShapeDtypeStruct((BATCH, npast, D_BR), f32),
            jax.ShapeDtypeStruct((DEC_BATCH, DEC_SEQ * D_BR), f32),
        ],
        scratch_shapes=[
            pltpu.VMEM((D_MODEL, 2 * D_BR), bf16),
            pltpu.VMEM((HALO_A + TM, D_BR), f32),
            pltpu.VMEM((TM, D_BR), f32),
        ],
        compiler_params=_cparams(1),
        name="branch_a",
    )(xn, w_in, conf_dw, ln_g.reshape(DEPTH, 1, D_BR), ln_b.reshape(DEPTH, 1, D_BR),
      _time_major(state))


def _stage_b_kernel(xn_ref, w_ref, lng_ref, lnb_ref, ws_ref, bt_ref, wss_ref, bss_ref,
                    act_ref, v_ref, wb_ref, tril_ref, bias_ref):
    i = pl.program_id(0)

    @pl.when(i == 0)
    def _():
        _cast_blocks([(w_ref, D_BR, 0, D_BR), (w_ref, 0, D_BR, D_BR)], wb_ref)
        row = lax.broadcasted_iota(jnp.int32, (CHUNK, CHUNK), 0)
        col = lax.broadcasted_iota(jnp.int32, (CHUNK, CHUNK), 1)
        for h in range(GMLP_HEADS):
            tril_ref[h] = jnp.where(row >= col, ws_ref[h], 0.0).astype(bf16)
            bias_ref[h] = jnp.broadcast_to(bt_ref[:, h:h + 1], (CHUNK, HEAD_DIM))

    lng = lng_ref[...]
    lnb = lnb_ref[...]

    @pl.when(i < N_PT)
    def _():
        p = _dot(xn_ref[...], wb_ref[...])
        for c in range(TM // CHUNK):
            rows = slice(c * CHUNK, (c + 1) * CHUNK)
            u = _gelu_tanh(p[rows, D_BR:])
            v = _layernorm(_gelu_tanh(p[rows, :D_BR]), lng, lnb).astype(bf16)
            for h in range(GMLP_HEADS):
                lanes = slice(h * HEAD_DIM, (h + 1) * HEAD_DIM)
                mixed = _dot(tril_ref[h], v[:, lanes]) + bias_ref[h]
                act_ref[rows, lanes] = (u[:, lanes] * mixed).astype(bf16)

    @pl.when(i == N_PT)
    def _():
        p = _dot(xn_ref[0:M_S, :], wb_ref[...])
        u = _gelu_tanh(p[:, D_BR:])
        v = _layernorm(_gelu_tanh(p[:, :D_BR]), lng, lnb)
        for t in range(DEC_SEQ):
            v_ref[:, t * D_BR:(t + 1) * D_BR] = v[t * DEC_BATCH:(t + 1) * DEC_BATCH]
        for t in range(DEC_SEQ):
            rows = slice(t * DEC_BATCH, (t + 1) * DEC_BATCH)
            for h in range(GMLP_HEADS):
                lanes = slice(h * HEAD_DIM, (h + 1) * HEAD_DIM)
                mixed = jnp.full((DEC_BATCH, HEAD_DIM), bss_ref[h * DEC_SEQ + t], f32)
                for s in range(t + 1):
                    coef = wss_ref[(h * DEC_SEQ + t) * DEC_SEQ + s]
                    mixed = mixed + coef * v[s * DEC_BATCH:(s + 1) * DEC_BATCH, lanes]
                act_ref[rows, lanes] = (u[rows, lanes] * mixed).astype(bf16)


def _stage_b(xn, w_in, ln_g, ln_b, gmlp_ws, gmlp_b, l):
    bias_t = jnp.swapaxes(gmlp_b, 1, 2)
    ws_small = gmlp_ws[l, :, :DEC_SEQ, :DEC_SEQ].reshape(-1)
    b_small = gmlp_b[l, :, :DEC_SEQ].reshape(-1)
    act, v = pl.pallas_call(
        _stage_b_kernel,
        grid=(N_TILES,),
        in_specs=[
            pl.BlockSpec((TM, D_MODEL), lambda i: (i, 0)),
            _once((None, D_MODEL, 2 * D_BR), lambda i: (l, 0, 1)),
            pl.BlockSpec((None, 1, D_BR), lambda i: (l, 0, 0)),
            pl.BlockSpec((None, 1, D_BR), lambda i: (l, 0, 0)),
            pl.BlockSpec((None, GMLP_HEADS, CHUNK, CHUNK), lambda i: (l, 0, 0, 0)),
            pl.BlockSpec((None, CHUNK, GMLP_HEADS), lambda i: (l, 0, 0)),
            pl.BlockSpec(memory_space=pltpu.SMEM),
            pl.BlockSpec(memory_space=pltpu.SMEM),
        ],
        out_specs=[
            pl.BlockSpec((TM, D_BR), lambda i: (i, 0)),
            pl.BlockSpec((DEC_BATCH, DEC_SEQ * D_BR), lambda i: (0, 0)),
        ],
        out_shape=[
            jax.ShapeDtypeStruct((M_ALL, D_BR), bf16),
            jax.ShapeDtypeStruct((DEC_BATCH, DEC_SEQ * D_BR), f32),
        ],
        scratch_shapes=[
            pltpu.VMEM((D_MODEL, 2 * D_BR), bf16),
            pltpu.VMEM((GMLP_HEADS, CHUNK, CHUNK), bf16),
            pltpu.VMEM((GMLP_HEADS, CHUNK, HEAD_DIM), f32),
        ],
        compiler_params=_cparams(1),
        name="branch_b",
    )(xn, w_in, ln_g.reshape(DEPTH, 1, D_BR), ln_b.reshape(DEPTH, 1, D_BR),
      gmlp_ws, bias_t, ws_small, b_small)
    return act, v.reshape(DEC_BATCH, DEC_SEQ, D_BR)


def _stage_c_kernel(xn_ref, w0_ref, w1_ref, w2_ref, dw_ref, hist_ref,
                    act_ref, stp_ref, sts_ref, wb_ref, ext_ref):
    i = pl.program_id(0)
    nblk = D_BR // LANES

    @pl.when(i == 0)
    def _():
        moves = []
        for k in range(nblk):
            for part, w_ref in enumerate((w0_ref, w1_ref, w2_ref)):
                moves.append((w_ref, k * LANES, (3 * k + part) * LANES, LANES))
        _cast_blocks(moves, wb_ref)
        ext_ref[0:HALO_S, :] = jnp.zeros((HALO_S, D_BR), f32)

    def taps(k):
        return dw_ref[0:1, _blk(k)], dw_ref[1:2, _blk(k)], dw_ref[2:3, _blk(k)]

    @pl.when(i < N_PT)
    def _():
        first = lax.rem(i, TILES_PER_SEQ) == 0
        ext_ref[0:HALO_S, :] = jnp.where(first, 0.0, ext_ref[0:HALO_S, :])
        p = _dot(xn_ref[...], wb_ref[...])
        for k in range(nblk):
            w0, w1, w2 = taps(k)
            for c in range(TM // EP_R):
                rows = slice(c * EP_R, (c + 1) * EP_R)
                r0 = HALO_S + c * EP_R
                ext_ref[r0:r0 + EP_R, _blk(k)] = p[rows, _blk(3 * k + 1)] * p[rows, _blk(3 * k + 2)]
                z = (w0 * ext_ref[r0 - 2:r0 - 2 + EP_R, _blk(k)]
                     + w1 * ext_ref[r0 - 1:r0 - 1 + EP_R, _blk(k)]
                     + w2 * ext_ref[r0:r0 + EP_R, _blk(k)])
                act_ref[rows, _blk(k)] = (p[rows, _blk(3 * k)] * z).astype(bf16)
        stp_ref[i // TILES_PER_SEQ] = ext_ref[HALO_S + TM - (SCONV_K - 1):HALO_S + TM, :]
        ext_ref[0:HALO_S, :] = ext_ref[TM:TM + HALO_S, :]

    @pl.when(i == N_PT)
    def _():
        p = _dot(xn_ref[0:M_S, :], wb_ref[...])
        for k in range(nblk):
            w0, w1, w2 = taps(k)
            s = p[:, _blk(3 * k + 1)] * p[:, _blk(3 * k + 2)]
            xp = [hist_ref[:, _blk(k)], hist_ref[:, D_BR + k * LANES:D_BR + (k + 1) * LANES]]
            xp += [s[t * DEC_BATCH:(t + 1) * DEC_BATCH] for t in range(DEC_SEQ)]
            for t in range(DEC_SEQ):
                z = w0 * xp[t] + w1 * xp[t + 1] + w2 * xp[t + 2]
                rows = slice(t * DEC_BATCH, (t + 1) * DEC_BATCH)
                act_ref[rows, _blk(k)] = (p[rows, _blk(3 * k)] * z).astype(bf16)
            sts_ref[:, _blk(k)] = xp[DEC_SEQ]
            sts_ref[:, D_BR + k * LANES:D_BR + (k + 1) * LANES] = xp[DEC_SEQ + 1]


def _stage_c(xn, w_in, sconv_dw, state, l):
    npast = SCONV_K - 1
    hist = state.reshape(DEPTH, DEC_BATCH, npast * D_BR)
    wspec = lambda cb: _once((None, D_MODEL, D_BR), lambda i: (l, 0, cb))
    act, stp, sts = pl.pallas_call(
        _stage_c_kernel,
        grid=(N_TILES,),
        in_specs=[
            pl.BlockSpec((TM, D_MODEL), lambda i: (i, 0)),
            wspec(4), wspec(5), wspec(6),
            pl.BlockSpec((None, SCONV_K, D_BR), lambda i: (l, 0, 0)),
            pl.BlockSpec((None, DEC_BATCH, npast * D_BR), lambda i: (l, 0, 0)),
        ],
        out_specs=[
            pl.BlockSpec((TM, D_BR), lambda i: (i, 0)),
            pl.BlockSpec((BATCH, npast, D_BR), lambda i: (0, 0, 0)),
            pl.BlockSpec((DEC_BATCH, npast * D_BR), lambda i: (0, 0)),
        ],
        out_shape=[
            jax.ShapeDtypeStruct((M_ALL, D_BR), bf16),
            jax.ShapeDtypeStruct((BATCH, npast, D_BR), f32),
            jax.ShapeDtypeStruct((DEC_BATCH, npast * D_BR), f32),
        ],
        scratch_shapes=[
            pltpu.VMEM((D_MODEL, 3 * D_BR), bf16),
            pltpu.VMEM((HALO_S + TM, D_BR), f32),
        ],
        compiler_params=_cparams(1),
        name="branch_c",
    )(xn, w_in, w_in, w_in, sconv_dw, hist)
    return act, stp, sts.reshape(DEC_BATCH, npast, D_BR)


def _stage_d_kernel(first_layer, *refs):
    if first_layer:
        (xp_ref, xs_ref, g_ref, w_ref, pw_ref, psc_ref, hist_ref,
         act_ref, stp_ref, new_ref, xn_ref, wb_ref, pwb_ref, ext_ref) = refs
    else:
        (xp_ref, g_ref, w_ref, pw_ref, psc_ref, hist_ref,
         act_ref, stp_ref, new_ref, xn_ref, wb_ref, pwb_ref, ext_ref) = refs
    i = pl.program_id(0)
    gdim = D_BR // len(POOL_WINDOWS)
    gain = g_ref[...]

    def put_xn(r, n, x):
        xn_ref[pl.ds(r, n), :] = _rms(x, gain).astype(bf16)

    @pl.when(i == 0)
    def _():
        _cast_rows(w_ref, wb_ref)
        pwb_ref[...] = pw_ref[...].astype(bf16)

    @pl.when(i < N_PT)
    def _():
        tis = i % TILES_PER_SEQ

        @pl.when(tis == 0)
        def _():
            ext_ref[0:HALO_P, :] = jnp.zeros((HALO_P, D_BR), f32)

        _norm_rows(xp_ref, put_xn, TM)
        ext_ref[HALO_P:HALO_P + TM, :] = _dot(xn_ref[...], wb_ref[...])
        pos1 = tis * TM + 1 + lax.broadcasted_iota(jnp.int32, (TM, gdim), 0)
        for g, win in enumerate(POOL_WINDOWS):
            lanes = slice(g * gdim, (g + 1) * gdim)
            cur = ext_ref[HALO_P:HALO_P + TM, lanes]
            tot = cur
            for k in range(1, win):
                tot = tot + ext_ref[HALO_P - k:HALO_P - k + TM, lanes]
            cnt = jnp.minimum(pos1, win).astype(f32)
            pm = (tot / cnt - cur).astype(bf16)
            act_ref[:, lanes] = (_dot(pm, pwb_ref[g]) * psc_ref[:, lanes]).astype(bf16)

        @pl.when(tis == TILES_PER_SEQ - 1)
        def _():
            stp_ref[i // TILES_PER_SEQ] = ext_ref[HALO_P + TM - POOL_PAST:HALO_P + TM, :]

        ext_ref[0:HALO_P, :] = ext_ref[TM:TM + HALO_P, :]

    @pl.when(i == N_PT)
    def _():
        if first_layer:
            for t in range(DEC_SEQ):
                xn_ref[t * DEC_BATCH:(t + 1) * DEC_BATCH, :] = (
                    _rms(xs_ref[:, t, :], gain).astype(bf16))
        else:
            _norm_rows(xp_ref, put_xn, M_S)
        p = _dot(xn_ref[0:M_S, :], wb_ref[...])

        def slab(s, lanes):
            if s < POOL_PAST:
                return hist_ref[s, :, lanes]
            r0 = (s - POOL_PAST) * DEC_BATCH
            return p[r0:r0 + DEC_BATCH, lanes]

        for t in range(DEC_SEQ):
            rows = slice(t * DEC_BATCH, (t + 1) * DEC_BATCH)
            for g, win in enumerate(POOL_WINDOWS):
                lanes = slice(g * gdim, (g + 1) * gdim)
                cur = slab(POOL_PAST + t, lanes)
                tot = cur
                for k in range(1, win):
                    tot = tot + slab(POOL_PAST + t - k, lanes)
                cnt = float(min(win, PAST_LEN + t + 1))
                pm = (tot / cnt - cur).astype(bf16)
                act_ref[rows, lanes] = (_dot(pm, pwb_ref[g]) * psc_ref[:, lanes]).astype(bf16)
        for t in range(DEC_SEQ):
            new_ref[:, t * D_BR:(t + 1) * D_BR] = p[t * DEC_BATCH:(t + 1) * DEC_BATCH]


def _stage_d(w_in, pool_w, pool_scale, state, norm_w, l, x=None, x_prompt=None, x_sample=None):
    ngroup = len(POOL_WINDOWS)
    gdim = D_BR // ngroup
    first_layer = x is None
    if first_layer:
        xs = [x_prompt.reshape(M_P, D_MODEL), x_sample]
        x_specs = [
            pl.BlockSpec((TM, D_MODEL), lambda i: (jnp.minimum(i, N_PT - 1), 0)),
            pl.BlockSpec((DEC_BATCH, DEC_SEQ, D_MODEL), lambda i: (0, 0, 0)),
        ]
    else:
        xs = [x]
        x_specs = [pl.BlockSpec((TM, D_MODEL), lambda i: (i, 0))]
    return pl.pallas_call(
        functools.partial(_stage_d_kernel, first_layer),
        grid=(N_TILES,),
        in_specs=x_specs + [
            pl.BlockSpec((None, 1, D_MODEL), lambda i: (l, 0, 0)),
            _once((None, D_MODEL, D_BR), lambda i: (l, 0, 7)),
            pl.BlockSpec((None, ngroup, gdim, gdim), lambda i: (l, 0, 0, 0)),
            pl.BlockSpec((None, 1, D_BR), lambda i: (l, 0, 0)),
            _once((None, POOL_PAST, DEC_BATCH, D_BR), lambda i: (l, 0, 0, 0)),
        ],
        out_specs=[
            pl.BlockSpec((TM, D_BR), lambda i: (i, 0)),
            pl.BlockSpec((BATCH, POOL_PAST, D_BR), lambda i: (0, 0, 0)),
            pl.BlockSpec((DEC_BATCH, DEC_SEQ * D_BR), lambda i: (0, 0)),
            pl.BlockSpec((TM, D_MODEL), lambda i: (i, 0)),
        ],
        out_shape=[
            jax.ShapeDtypeStruct((M_ALL, D_BR), bf16),
            jax.ShapeDtypeStruct((BATCH, POOL_PAST, D_BR), f32),
            jax.ShapeDtypeStruct((DEC_BATCH, DEC_SEQ * D_BR), f32),
            jax.ShapeDtypeStruct((M_ALL, D_MODEL), bf16),
        ],
        scratch_shapes=[
            pltpu.VMEM((D_MODEL, D_BR), bf16),
            pltpu.VMEM((ngroup, gdim, gdim), bf16),
            pltpu.VMEM((HALO_P + TM, D_BR), f32),
        ],
        compiler_params=_cparams(1),
        name="branch_d",
    )(*xs, norm_w.reshape(DEPTH, 1, D_MODEL), w_in, pool_w, pool_scale.reshape(DEPTH, 1, D_BR),
      _time_major(state))


def _stage_g_kernel(xn_ref, a0_ref, a1_ref, a2_ref, a3_ref,
                    g0_ref, g1_ref, g2_ref, g3_ref, wbr_ref,
                    out_ref, wgb_ref, wbb_ref):
    i = pl.program_id(1)

    @pl.when(i == 0)
    def _():
        for b, g_ref in enumerate((g0_ref, g1_ref, g2_ref, g3_ref)):
            _cast_rows(g_ref, wgb_ref, b * TN_G)
        wbb_ref[...] = wbr_ref[...].astype(bf16)

    def compute(rows):
        merged = None
        for b, a_ref in enumerate((a0_ref, a1_ref, a2_ref, a3_ref)):
            gate = _sigmoid(_dot(xn_ref[rows, :], wgb_ref[:, b * TN_G:(b + 1) * TN_G]))
            term = gate * _dot(a_ref[rows, :], wbb_ref[b])
            merged = term if merged is None else merged + term
        out_ref[rows, :] = merged.astype(bf16)

    @pl.when(i < N_PT)
    def _():
        compute(slice(0, TM))

    @pl.when(i == N_PT)
    def _():
        compute(slice(0, M_S))


def _stage_g(xn, acts, w_in, w_branch, l):
    gate_blk0 = GATE_COL0 // TN_G
    per_branch = D_MODEL // TN_G
    gspec = lambda b: pl.BlockSpec(
        (None, D_MODEL, TN_G), lambda c, i: (l, 0, gate_blk0 + b * per_branch + c))
    aspec = pl.BlockSpec((TM, D_BR), lambda c, i: (i, 0))
    return pl.pallas_call(
        _stage_g_kernel,
        grid=(D_MODEL // TN_G, N_TILES),
        in_specs=[
            pl.BlockSpec((TM, D_MODEL), lambda c, i: (i, 0)),
            aspec, aspec, aspec, aspec,
            gspec(0), gspec(1), gspec(2), gspec(3),
            pl.BlockSpec((None, N_BRANCH, D_BR, TN_G), lambda c, i: (l, 0, 0, c)),
        ],
        out_specs=pl.BlockSpec((TM, TN_G), lambda c, i: (i, c)),
        out_shape=jax.ShapeDtypeStruct((M_ALL, D_MODEL), bf16),
        scratch_shapes=[
            pltpu.VMEM((D_MODEL, N_BRANCH * TN_G), bf16),
            pltpu.VMEM((N_BRANCH, D_BR, TN_G), bf16),
        ],
        compiler_params=_cparams(2),
        name="gate_merge",
    )(xn, *acts, w_in, w_in, w_in, w_in, w_branch)


def _stage_res_kernel(lhs_ref, w_ref, x_ref, o_ref, wb_ref):
    i = pl.program_id(1)

    @pl.when(i == 0)
    def _():
        _cast_rows(w_ref, wb_ref)

    @pl.when(i < N_PT)
    def _():
        o_ref[...] = x_ref[...] + _dot(lhs_ref[...], wb_ref[...])

    @pl.when(i == N_PT)
    def _():
        o_ref[0:M_S, :] = x_ref[0:M_S, :] + _dot(lhs_ref[0:M_S, :], wb_ref[...])


def _stage_res(lhs, w, x, l, tn, name, single_buffer_w):
    k = lhs.shape[1]
    wshape = (None, k, tn)
    wmap = lambda c, i: (l, 0, c)
    wspec = _once(wshape, wmap) if single_buffer_w else pl.BlockSpec(wshape, wmap)
    return pl.pallas_call(
        _stage_res_kernel,
        grid=(D_MODEL // tn, N_TILES),
        in_specs=[
            pl.BlockSpec((TM, k), lambda c, i: (i, 0)),
            wspec,
            pl.BlockSpec((TM, tn), lambda c, i: (i, c)),
        ],
        out_specs=pl.BlockSpec((TM, tn), lambda c, i: (i, c)),
        out_shape=jax.ShapeDtypeStruct((M_ALL, D_MODEL), f32),
        scratch_shapes=[pltpu.VMEM((k, tn), bf16)],
        compiler_params=_cparams(2),
        name=name,
    )(lhs, w, x)


def _stage_o_kernel(first_layer, *refs):
    if first_layer:
        m_ref, w_ref, xp_ref, xs_ref, g_ref, x_ref, xn_ref, wb_ref = refs
    else:
        m_ref, w_ref, xin_ref, g_ref, x_ref, xn_ref, wb_ref = refs
    i = pl.program_id(0)

    @pl.when(i == 0)
    def _():
        _cast_rows(w_ref, wb_ref)

    g = g_ref[...]

    def finish(r0, nrows, resid_rows, proj):
        for c in range(nrows // LN_R):
            rows = slice(r0 + c * LN_R, r0 + (c + 1) * LN_R)
            x = resid_rows(slice(c * LN_R, (c + 1) * LN_R)) + proj[rows]
            x_ref[rows, :] = x
            xn_ref[rows, :] = _rms(x, g).astype(bf16)

    if first_layer:
        @pl.when(i < M_P // TM_O)
        def _():
            finish(0, TM_O, lambda r: xp_ref[r, :], _dot(m_ref[...], wb_ref[...]))

        @pl.when(i >= M_P // TM_O)
        def _():
            proj = _dot(m_ref[...], wb_ref[...])
            for t in range(DEC_SEQ):
                finish(t * DEC_BATCH, DEC_BATCH,
                       lambda r, t=t: xs_ref[r, t, :], proj)
    else:
        finish(0, TM_O, lambda r: xin_ref[r, :], _dot(m_ref[...], wb_ref[...]))


def _stage_o(merged, w_o, norm_w, l, x=None, x_prompt=None, x_sample=None):
    first_layer = x is None
    n_steps = M_ALL // TM_O
    row_blk = pl.BlockSpec((TM_O, D_MODEL), lambda i: (i, 0))
    if first_layer:
        n_p = M_P // TM_O
        resid = [x_prompt.reshape(M_P, D_MODEL), x_sample]
        resid_specs = [
            pl.BlockSpec((TM_O, D_MODEL), lambda i: (jnp.minimum(i, n_p - 1), 0)),
            pl.BlockSpec((DEC_BATCH, DEC_SEQ, D_MODEL), lambda i: (0, 0, 0)),
        ]
    else:
        resid = [x]
        resid_specs = [row_blk]
    return pl.pallas_call(
        functools.partial(_stage_o_kernel, first_layer),
        grid=(n_steps,),
        in_specs=[row_blk, _once((None, D_MODEL, D_MODEL), lambda i: (l, 0, 0))]
        + resid_specs + [pl.BlockSpec((None, 1, D_MODEL), lambda i: (l, 0, 0))],
        out_specs=[row_blk, row_blk],
        out_shape=[
            jax.ShapeDtypeStruct((M_ALL, D_MODEL), f32),
            jax.ShapeDtypeStruct((M_ALL, D_MODEL), bf16),
        ],
        scratch_shapes=[pltpu.VMEM((D_MODEL, D_MODEL), bf16)],
        compiler_params=_cparams(1),
        name="out_proj_norm",
    )(merged, w_o, *resid, norm_w.reshape(DEPTH, 1, D_MODEL))


def _stage_u_kernel(xn_ref, wg_ref, wv_ref, dwg_ref, dwv_ref,
                    hgh_ref, hvh_ref,
                    a_ref, stp_ref, sts_ref, wb_ref, ext_ref):
    i = pl.program_id(1)
    nblk = TK_U // U_BLK
    npast = FFN_K - 1

    def _blk(k):
        return slice(k * U_BLK, (k + 1) * U_BLK)

    @pl.when(i == 0)
    def _():
        moves = []
        for k in range(nblk):
            moves.append((wg_ref, k * U_BLK, 2 * k * U_BLK, U_BLK))
            moves.append((wv_ref, k * U_BLK, (2 * k + 1) * U_BLK, U_BLK))
        _cast_blocks(moves, wb_ref)
        ext_ref[0:HALO_S, :] = jnp.zeros((HALO_S, 2 * TK_U), f32)

    def taps(dw_ref, k):
        return dw_ref[0:1, _blk(k)], dw_ref[1:2, _blk(k)], dw_ref[2:3, _blk(k)]

    @pl.when(i < N_PT)
    def _():
        first = lax.rem(i, TILES_PER_SEQ) == 0
        ext_ref[0:HALO_S, :] = jnp.where(first, 0.0, ext_ref[0:HALO_S, :])
        h = _dot(xn_ref[...], wb_ref[...])
        ext_ref[HALO_S:HALO_S + TM, :] = h
        for k in range(nblk):
            for c in range(TM // U_R):
                r0 = HALO_S + c * U_R

                def conv(dw_ref, lanes):
                    w0, w1, w2 = taps(dw_ref, k)
                    return (w0 * ext_ref[r0 - 2:r0 - 2 + U_R, lanes]
                            + w1 * ext_ref[r0 - 1:r0 - 1 + U_R, lanes]
                            + w2 * h[c * U_R:(c + 1) * U_R, lanes])

                hg = conv(dwg_ref, _blk(2 * k))
                hv = conv(dwv_ref, _blk(2 * k + 1))
                a_ref[c * U_R:(c + 1) * U_R, _blk(k)] = (_silu(hg) * hv).astype(bf16)
        b = i // TILES_PER_SEQ
        last = slice(HALO_S + TM - npast, HALO_S + TM)
        for k in range(nblk):
            stp_ref[b, 0, :, _blk(k)] = ext_ref[last, _blk(2 * k)]
            stp_ref[b, 1, :, _blk(k)] = ext_ref[last, _blk(2 * k + 1)]
        ext_ref[0:HALO_S, :] = ext_ref[TM:TM + HALO_S, :]

    @pl.when(i == N_PT)
    def _():
        h = _dot(xn_ref[0:M_S, :], wb_ref[...])

        def conv_slabs(dw_ref, hist_ref, k, lanes):
            w0, w1, w2 = taps(dw_ref, k)
            xp = [hist_ref[:, r, _blk(k)] for r in range(npast)]
            xp += [h[t * DEC_BATCH:(t + 1) * DEC_BATCH, lanes] for t in range(DEC_SEQ)]
            return [w0 * xp[t] + w1 * xp[t + 1] + w2 * xp[t + 2] for t in range(DEC_SEQ)]

        for k in range(nblk):
            hg = conv_slabs(dwg_ref, hgh_ref, k, _blk(2 * k))
            hv = conv_slabs(dwv_ref, hvh_ref, k, _blk(2 * k + 1))
            for t in range(DEC_SEQ):
                a_ref[t * DEC_BATCH:(t + 1) * DEC_BATCH, _blk(k)] = (
                    _silu(hg[t]) * hv[t]).astype(bf16)
            for r in range(npast):
                t = DEC_SEQ - npast + r
                rows = slice(t * DEC_BATCH, (t + 1) * DEC_BATCH)
                sts_ref[r, 0, :, _blk(k)] = h[rows, _blk(2 * k)]
                sts_ref[r, 1, :, _blk(k)] = h[rows, _blk(2 * k + 1)]


def _stage_u(xn, ffn_up, ffn_dw, state, l):
    nk = D_FF // TK_U
    npast = FFN_K - 1
    wspec = lambda half: pl.BlockSpec(
        (None, D_MODEL, TK_U), lambda k, i: (l, 0, half * nk + k))
    dspec = lambda half: pl.BlockSpec(
        (None, FFN_K, TK_U), lambda k, i: (l, 0, half * nk + k))
    hspec = lambda half: pl.BlockSpec(
        (None, DEC_BATCH, npast, TK_U), lambda k, i: (l, 0, 0, half * nk + k))
    a, stp, sts = pl.pallas_call(
        _stage_u_kernel,
        grid=(nk, N_TILES),
        in_specs=[
            pl.BlockSpec((TM, D_MODEL), lambda k, i: (i, 0)),
            wspec(0), wspec(1), dspec(0), dspec(1),
            hspec(0), hspec(1),
        ],
        out_specs=[
            pl.BlockSpec((TM, TK_U), lambda k, i: (i, k)),
            pl.BlockSpec((BATCH, 2, npast, TK_U), lambda k, i: (0, 0, 0, k)),
            pl.BlockSpec((npast, 2, DEC_BATCH, TK_U), lambda k, i: (0, 0, 0, k)),
        ],
        out_shape=[
            jax.ShapeDtypeStruct((M_ALL, D_FF), bf16),
            jax.ShapeDtypeStruct((BATCH, 2, npast, D_FF), f32),
            jax.ShapeDtypeStruct((npast, 2, DEC_BATCH, D_FF), f32),
        ],
        scratch_shapes=[
            pltpu.VMEM((D_MODEL, 2 * TK_U), bf16),
            pltpu.VMEM((HALO_S + TM, 2 * TK_U), f32),
        ],
        compiler_params=_cparams(2),
        name="ffn_up",
    )(xn, ffn_up, ffn_up, ffn_dw, ffn_dw, state, state)
    stp = jnp.transpose(stp, (0, 2, 1, 3)).reshape(BATCH, npast, 2 * D_FF)
    return a, stp, sts


def _ffn_state_kernel(*refs):
    in_refs, out_ref = refs[:-1], refs[-1]
    l = pl.program_id(0)
    for d, in_ref in enumerate(in_refs):
        @pl.when(l == d)
        def _(in_ref=in_ref):
            for r in range(FFN_K - 1):
                out_ref[:, r, :] = in_ref[r]


def _ffn_state(per_layer):
    npast = FFN_K - 1
    ncb = D_FF // FFN_ST_W
    spec = pl.BlockSpec((npast, None, DEC_BATCH, FFN_ST_W), lambda l, h, c: (0, h, 0, c))
    return pl.pallas_call(
        _ffn_state_kernel,
        grid=(DEPTH, 2, ncb),
        in_specs=[spec] * DEPTH,
        out_specs=pl.BlockSpec((None, DEC_BATCH, npast, FFN_ST_W),
                               lambda l, h, c: (l, 0, 0, h * ncb + c)),
        out_shape=jax.ShapeDtypeStruct((DEPTH, DEC_BATCH, npast, 2 * D_FF), f32),
        compiler_params=_cparams(3),
        name="ffn_state",
    )(*per_layer)


def kernel(x_prompt, x_sample, state_conf_conv, state_sconv, state_pool, state_ffn_conv,
           norm_mix, w_in, conf_dw, conf_ln_g, conf_ln_b, gmlp_ln_g, gmlp_ln_b, gmlp_ws,
           gmlp_b, sconv_dw, pool_w, pool_scale, w_branch, w_o, norm_ffn, ffn_up, ffn_dw,
           ffn_down, norm_final):
    conf_p, conf_new, sconv_p, sconv_s, pool_p, pool_new, ffn_p, ffn_new, v_s = (
        [] for _ in range(9))
    x = None
    for l in range(DEPTH):
        if l == 0:
            act_d, st_p, new, xn = _stage_d(w_in, pool_w, pool_scale, state_pool, norm_mix, l,
                                            x_prompt=x_prompt, x_sample=x_sample)
        else:
            act_d, st_p, new, xn = _stage_d(w_in, pool_w, pool_scale, state_pool, norm_mix, l, x=x)
        pool_p.append(st_p)
        pool_new.append(new)
        act_a, st_p, new = _stage_a(xn, w_in, conf_dw, conf_ln_g, conf_ln_b, state_conf_conv, l)
        conf_p.append(st_p)
        conf_new.append(new)
        act_b, v = _stage_b(xn, w_in, gmlp_ln_g, gmlp_ln_b, gmlp_ws, gmlp_b, l)
        v_s.append(v)
        act_c, st_p, st_s = _stage_c(xn, w_in, sconv_dw, state_sconv, l)
        sconv_p.append(st_p)
        sconv_s.append(st_s)
        merged = _stage_g(xn, (act_a, act_b, act_c, act_d), w_in, w_branch, l)
        if l == 0:
            x, xn = _stage_o(merged, w_o, norm_ffn, l, x_prompt=x_prompt, x_sample=x_sample)
        else:
            x, xn = _stage_o(merged, w_o, norm_ffn, l, x=x)
        a, st_p, new = _stage_u(xn, ffn_up, ffn_dw, state_ffn_conv, l)
        ffn_p.append(st_p)
        ffn_new.append(new)
        x = _stage_res(a, ffn_down, x, l, TN_D, "ffn_down", True)
    y_prompt, y_sample = _final_norm(x, norm_final)
    conf_s = _roll_state(state_conf_conv, conf_new)
    pool_s = _roll_state(state_pool, pool_new)
    ffn_s = _ffn_state(ffn_new)
    st = jnp.stack
    return (y_prompt, y_sample, st(conf_p), conf_s, st(sconv_p), st(sconv_s),
            st(pool_p), pool_s, st(ffn_p), ffn_s, st(v_s))
```

```python
import functools

import jax
import jax.numpy as jnp
from jax import lax
from jax.experimental import pallas as pl
from jax.experimental.pallas import tpu as pltpu

D_MODEL = 2048
BATCH = 4
SEQ = 2048
DEPTH = 2
DEC_BATCH = 128
DEC_SEQ = 4
PAST_LEN = 16384
D_BR = 512
N_BRANCH = 4
CONF_K = 31
GMLP_HEADS = 4
HEAD_DIM = D_BR // GMLP_HEADS
CHUNK = 128
SCONV_K = 3
POOL_WINDOWS = (2, 4, 8, 16)
POOL_PAST = 15
D_FF = 5632
FFN_K = 3
EPS = 1e-6
GATE_COL0 = 8 * D_BR

TM = 1024
M_P = BATCH * SEQ
M_S = DEC_BATCH * DEC_SEQ
M_ALL = M_P + M_S
N_PT = M_P // TM
TILES_PER_SEQ = SEQ // TM
N_TILES = N_PT + 1

LANES = 128
HALO_A = 32
HALO_P = 16
HALO_S = 8
CONV_R = 128
EP_R = 128
LN_R = 64
ROLL_B = 128
FFN_ST_W = 2816
U_R = 1024
U_BLK = 512

TN_G = 256
TM_O = 512
TK_U = 512
TN_D = 512

VMEM_LIMIT = 60000 * 1024

f32 = jnp.float32
bf16 = jnp.bfloat16


def _cparams(n_axes):
    return pltpu.CompilerParams(
        dimension_semantics=("arbitrary",) * n_axes, vmem_limit_bytes=VMEM_LIMIT)


def _once(block_shape, index_map):
    return pl.BlockSpec(block_shape, index_map, pipeline_mode=pl.Buffered(1))


def _rms(x, g):
    return x * lax.rsqrt(jnp.mean(x * x, axis=-1, keepdims=True) + EPS) * g


def _layernorm(x, g, b):
    mu = jnp.mean(x, axis=-1, keepdims=True)
    d = x - mu
    var = jnp.mean(d * d, axis=-1, keepdims=True)
    return d * lax.rsqrt(var + EPS) * g + b


def _sigmoid(x):
    return 1.0 / (1.0 + jnp.exp(-x))


def _silu(x):
    h = 0.5 * x
    return h + h * jnp.tanh(h)


def _gelu_tanh(x):
    c = 0.7978845608028654
    return 0.5 * x * (1.0 + jnp.tanh(c * (x + 0.044715 * (x * x * x))))


def _cast_blocks(moves, wb_ref, rows_per_step=256):
    k = wb_ref.shape[0]

    def body(c, carry):
        r = pl.multiple_of(c * rows_per_step, rows_per_step)
        for w_ref, src, dst, n in moves:
            wb_ref[pl.ds(r, rows_per_step), dst:dst + n] = (
                w_ref[pl.ds(r, rows_per_step), src:src + n].astype(bf16))
        return carry

    lax.fori_loop(0, k // rows_per_step, body, 0)


def _cast_rows(w_ref, wb_ref, col0=0):
    _cast_blocks([(w_ref, 0, col0, w_ref.shape[1])], wb_ref)


def _dot(a, b):
    return jnp.dot(a, b, preferred_element_type=f32)


def _blk(k):
    return slice(k * LANES, (k + 1) * LANES)


def _sample_first(i):
    return lax.rem(i + N_PT, N_TILES)


def _norm_rows(src_ref, put, rows, chunk=256):
    def body(c, carry):
        r = pl.multiple_of(c * chunk, chunk)
        put(r, chunk, src_ref[pl.ds(r, chunk), :])
        return carry

    lax.fori_loop(0, rows // chunk, body, 0)


def _final_norm_kernel(x_ref, g_ref, yp_ref, ys_ref):
    i = pl.program_id(0)
    g = g_ref[...]

    @pl.when(i < N_PT)
    def _():
        def put(r, n, x):
            yp_ref[pl.ds(r, n), :] = _rms(x, g)
        _norm_rows(x_ref, put, TM)

    @pl.when(i == N_PT)
    def _():
        for t in range(DEC_SEQ):
            x = x_ref[t * DEC_BATCH:(t + 1) * DEC_BATCH, :]
            ys_ref[:, t, :] = _rms(x, g)


def _final_norm(x, norm_w):
    yp, ys = pl.pallas_call(
        _final_norm_kernel,
        grid=(N_TILES,),
        in_specs=[
            pl.BlockSpec((TM, D_MODEL), lambda i: (i, 0)),
            pl.BlockSpec((1, D_MODEL), lambda i: (0, 0)),
        ],
        out_specs=[
            pl.BlockSpec((TM, D_MODEL), lambda i: (jnp.minimum(i, N_PT - 1), 0)),
            pl.BlockSpec((DEC_BATCH, DEC_SEQ, D_MODEL), lambda i: (0, 0, 0)),
        ],
        out_shape=[
            jax.ShapeDtypeStruct((M_P, D_MODEL), f32),
            jax.ShapeDtypeStruct((DEC_BATCH, DEC_SEQ, D_MODEL), f32),
        ],
        compiler_params=_cparams(1),
        name="final_norm",
    )(x, norm_w.reshape(1, D_MODEL))
    return yp.reshape(BATCH, SEQ, D_MODEL), ys


def _time_major(state):
    return jnp.transpose(state, (0, 2, 1, 3))


def _roll_state_kernel(old_ref, *refs):
    new_refs, out_ref = refs[:-1], refs[-1]
    l = pl.program_id(0)
    npast, width = old_ref.shape[0], old_ref.shape[2]
    keep = npast - DEC_SEQ
    for s in range(keep):
        out_ref[s] = old_ref[DEC_SEQ + s]
    for d, new_ref in enumerate(new_refs):
        @pl.when(l == d)
        def _(new_ref=new_ref):
            for t in range(DEC_SEQ):
                out_ref[keep + t] = new_ref[:, t * width:(t + 1) * width]


def _roll_state(old, new_rows):
    old_t = _time_major(old)
    npast, width = old_t.shape[1], old_t.shape[3]
    blk = pl.BlockSpec((None, npast, ROLL_B, width), lambda l, j: (l, 0, j, 0))
    nspec = pl.BlockSpec((ROLL_B, DEC_SEQ * width), lambda l, j: (j, 0))
    out = pl.pallas_call(
        _roll_state_kernel,
        grid=(DEPTH, DEC_BATCH // ROLL_B),
        in_specs=[blk] + [nspec] * DEPTH,
        out_specs=blk,
        out_shape=jax.ShapeDtypeStruct(old_t.shape, f32),
        compiler_params=_cparams(2),
        name="roll_state",
    )(old_t, *new_rows)
    return _time_major(out)


def _conv31_chunk(ext_ref, base, dw_ref, lanes):
    y = None
    for r in range(8):
        rows = CONV_R if r == 0 else CONV_R + 8
        z = None
        for a in range(5):
            j = 8 * a + r - 2
            if 0 <= j < CONF_K:
                term = dw_ref[pl.ds(j, 1), lanes] * ext_ref[pl.ds(base + 8 * a, rows), lanes]
                z = term if z is None else z + term
        zr = z[r:r + CONV_R]
        y = zr if y is None else y + zr
    return y


def _stage_a_kernel(xn_ref, w_ref, dw_ref, lng_ref, lnb_ref, hist_ref,
                    act_ref, stp_ref, new_ref, wb_ref, ext_ref, y_ref):
    i = pl.program_id(0)
    nblk = D_BR // LANES
    npast = CONF_K - 1

    @pl.when(i == 0)
    def _():
        moves = []
        for k in range(nblk):
            moves.append((w_ref, k * LANES, 2 * k * LANES, LANES))
            moves.append((w_ref, D_BR + k * LANES, (2 * k + 1) * LANES, LANES))
        _cast_blocks(moves, wb_ref)
        ext_ref[0:HALO_A, :] = jnp.zeros((HALO_A, D_BR), f32)

    lng = lng_ref[...]
    lnb = lnb_ref[...]

    def glu_block(p, k, rows):
        return p[rows, _blk(2 * k)] * _sigmoid(p[rows, _blk(2 * k + 1)])

    @pl.when(i < N_PT)
    def _():
        first = lax.rem(i, TILES_PER_SEQ) == 0
        ext_ref[0:HALO_A, :] = jnp.where(first, 0.0, ext_ref[0:HALO_A, :])
        p = _dot(xn_ref[...], wb_ref[...])
        for k in range(nblk):
            for c in range(TM // CONV_R):
                rows = slice(c * CONV_R, (c + 1) * CONV_R)
                ext_ref[HALO_A + c * CONV_R:HALO_A + (c + 1) * CONV_R, _blk(k)] = (
                    glu_block(p, k, rows))
            for c in range(TM // CONV_R):
                y_ref[c * CONV_R:(c + 1) * CONV_R, _blk(k)] = (
                    _conv31_chunk(ext_ref, c * CONV_R, dw_ref, _blk(k)))
        for c in range(TM // LN_R):
            rows = slice(c * LN_R, (c + 1) * LN_R)
            act_ref[rows, :] = _silu(_layernorm(y_ref[rows, :], lng, lnb)).astype(bf16)
        stp_ref[i // TILES_PER_SEQ] = ext_ref[HALO_A + TM - npast:HALO_A + TM, :]
        ext_ref[0:HALO_A, :] = ext_ref[TM:TM + HALO_A, :]

    @pl.when(i == N_PT)
    def _():
        p = _dot(xn_ref[0:M_S, :], wb_ref[...])
        glu = jnp.concatenate([glu_block(p, k, slice(0, M_S)) for k in range(nblk)], axis=1)
        rb = 16
        for q in range(DEC_BATCH // rb):
            y = [None] * DEC_SEQ
            for s in range(npast + DEC_SEQ):
                if s < npast:
                    slab = hist_ref[s, q * rb:(q + 1) * rb, :]
                else:
                    r0 = (s - npast) * DEC_BATCH + q * rb
                    slab = glu[r0:r0 + rb]
                for t in range(DEC_SEQ):
                    j = s - t
                    if 0 <= j < CONF_K:
                        term = dw_ref[pl.ds(j, 1), :] * slab
                        y[t] = term if y[t] is None else y[t] + term
            for t in range(DEC_SEQ):
                r0 = t * DEC_BATCH + q * rb
                act_ref[r0:r0 + rb, :] = _silu(_layernorm(y[t], lng, lnb)).astype(bf16)
        for t in range(DEC_SEQ):
            new_ref[:, t * D_BR:(t + 1) * D_BR] = glu[t * DEC_BATCH:(t + 1) * DEC_BATCH]


def _stage_a(xn, w_in, conf_dw, ln_g, ln_b, state, l):
    npast = CONF_K - 1
    return pl.pallas_call(
        _stage_a_kernel,
        grid=(N_TILES,),
        in_specs=[
            pl.BlockSpec((TM, D_MODEL), lambda i: (i, 0)),
            _once((None, D_MODEL, 2 * D_BR), lambda i: (l, 0, 0)),
            pl.BlockSpec((None, CONF_K, D_BR), lambda i: (l, 0, 0)),
            pl.BlockSpec((None, 1, D_BR), lambda i: (l, 0, 0)),
            pl.BlockSpec((None, 1, D_BR), lambda i: (l, 0, 0)),
            _once((None, npast, DEC_BATCH, D_BR), lambda i: (l, 0, 0, 0)),
        ],
        out_specs=[
            pl.BlockSpec((TM, D_BR), lambda i: (i, 0)),
            pl.BlockSpec((BATCH, npast, D_BR), lambda i: (0, 0, 0)),
            pl.BlockSpec((DEC_BATCH, DEC_SEQ * D_BR), lambda i: (0, 0)),
        ],
        out_shape=[
            jax.ShapeDtypeStruct((M_ALL, D_BR), bf16),
            jax.ShapeDtypeStruct((BATCH, npast, D_BR), f32),
            jax.ShapeDtypeStruct((DEC_BATCH, DEC_SEQ * D_BR), f32),
        ],
        scratch_shapes=[
            pltpu.VMEM((D_MODEL, 2 * D_BR), bf16),
            pltpu.VMEM((HALO_A + TM, D_BR), f32),
            pltpu.VMEM((TM, D_BR), f32),
        ],
        compiler_params=_cparams(1),
        name="branch_a",
    )(xn, w_in, conf_dw, ln_g.reshape(DEPTH, 1, D_BR), ln_b.reshape(DEPTH, 1, D_BR),
      _time_major(state))


def _stage_b_kernel(xn_ref, w_ref, lng_ref, lnb_ref, ws_ref, bt_ref, wss_ref, bss_ref,
                    act_ref, v_ref, wb_ref, tril_ref, bias_ref):
    i = pl.program_id(0)

    @pl.when(i == 0)
    def _():
        _cast_blocks([(w_ref, D_BR, 0, D_BR), (w_ref, 0, D_BR, D_BR)], wb_ref)
        row = lax.broadcasted_iota(jnp.int32, (CHUNK, CHUNK), 0)
        col = lax.broadcasted_iota(jnp.int32, (CHUNK, CHUNK), 1)
        for h in range(GMLP_HEADS):
            tril_ref[h] = jnp.where(row >= col, ws_ref[h], 0.0).astype(bf16)
            bias_ref[h] = jnp.broadcast_to(bt_ref[:, h:h + 1], (CHUNK, HEAD_DIM))

    lng = lng_ref[...]
    lnb = lnb_ref[...]

    @pl.when(i < N_PT)
    def _():
        p = _dot(xn_ref[...], wb_ref[...])
        for c in range(TM // CHUNK):
            rows = slice(c * CHUNK, (c + 1) * CHUNK)
            u = _gelu_tanh(p[rows, D_BR:])
            v = _layernorm(_gelu_tanh(p[rows, :D_BR]), lng, lnb).astype(bf16)
            for h in range(GMLP_HEADS):
                lanes = slice(h * HEAD_DIM, (h + 1) * HEAD_DIM)
                mixed = _dot(tril_ref[h], v[:, lanes]) + bias_ref[h]
                act_ref[rows, lanes] = (u[:, lanes] * mixed).astype(bf16)

    @pl.when(i == N_PT)
    def _():
        p = _dot(xn_ref[0:M_S, :], wb_ref[...])
        u = _gelu_tanh(p[:, D_BR:])
        v = _layernorm(_gelu_tanh(p[:, :D_BR]), lng, lnb)
        for t in range(DEC_SEQ):
            v_ref[:, t * D_BR:(t + 1) * D_BR] = v[t * DEC_BATCH:(t + 1) * DEC_BATCH]
        for t in range(DEC_SEQ):
            rows = slice(t * DEC_BATCH, (t + 1) * DEC_BATCH)
            for h in range(GMLP_HEADS):
                lanes = slice(h * HEAD_DIM, (h + 1) * HEAD_DIM)
                mixed = jnp.full((DEC_BATCH, HEAD_DIM), bss_ref[h * DEC_SEQ + t], f32)
                for s in range(t + 1):
                    coef = wss_ref[(h * DEC_SEQ + t) * DEC_SEQ + s]
                    mixed = mixed + coef * v[s * DEC_BATCH:(s + 1) * DEC_BATCH, lanes]
                act_ref[rows, lanes] = (u[rows, lanes] * mixed).astype(bf16)


def _stage_b(xn, w_in, ln_g, ln_b, gmlp_ws, gmlp_b, l):
    bias_t = jnp.swapaxes(gmlp_b, 1, 2)
    ws_small = gmlp_ws[l, :, :DEC_SEQ, :DEC_SEQ].reshape(-1)
    b_small = gmlp_b[l, :, :DEC_SEQ].reshape(-1)
    act, v = pl.pallas_call(
        _stage_b_kernel,
        grid=(N_TILES,),
        in_specs=[
            pl.BlockSpec((TM, D_MODEL), lambda i: (i, 0)),
            _once((None, D_MODEL, 2 * D_BR), lambda i: (l, 0, 1)),
            pl.BlockSpec((None, 1, D_BR), lambda i: (l, 0, 0)),
            pl.BlockSpec((None, 1, D_BR), lambda i: (l, 0, 0)),
            pl.BlockSpec((None, GMLP_HEADS, CHUNK, CHUNK), lambda i: (l, 0, 0, 0)),
            pl.BlockSpec((None, CHUNK, GMLP_HEADS), lambda i: (l, 0, 0)),
            pl.BlockSpec(memory_space=pltpu.SMEM),
            pl.BlockSpec(memory_space=pltpu.SMEM),
        ],
        out_specs=[
            pl.BlockSpec((TM, D_BR), lambda i: (i, 0)),
            pl.BlockSpec((DEC_BATCH, DEC_SEQ * D_BR), lambda i: (0, 0)),
        ],
        out_shape=[
            jax.ShapeDtypeStruct((M_ALL, D_BR), bf16),
            jax.ShapeDtypeStruct((DEC_BATCH, DEC_SEQ * D_BR), f32),
        ],
        scratch_shapes=[
            pltpu.VMEM((D_MODEL, 2 * D_BR), bf16),
            pltpu.VMEM((GMLP_HEADS, CHUNK, CHUNK), bf16),
            pltpu.VMEM((GMLP_HEADS, CHUNK, HEAD_DIM), f32),
        ],
        compiler_params=_cparams(1),
        name="branch_b",
    )(xn, w_in, ln_g.reshape(DEPTH, 1, D_BR), ln_b.reshape(DEPTH, 1, D_BR),
      gmlp_ws, bias_t, ws_small, b_small)
    return act, v.reshape(DEC_BATCH, DEC_SEQ, D_BR)


def _stage_c_kernel(xn_ref, w0_ref, w1_ref, w2_ref, dw_ref, hist_ref,
                    act_ref, stp_ref, sts_ref, wb_ref, ext_ref):
    i = pl.program_id(0)
    nblk = D_BR // LANES

    @pl.when(i == 0)
    def _():
        moves = []
        for k in range(nblk):
            for part, w_ref in enumerate((w0_ref, w1_ref, w2_ref)):
                moves.append((w_ref, k * LANES, (3 * k + part) * LANES, LANES))
        _cast_blocks(moves, wb_ref)
        ext_ref[0:HALO_S, :] = jnp.zeros((HALO_S, D_BR), f32)

    def taps(k):
        return dw_ref[0:1, _blk(k)], dw_ref[1:2, _blk(k)], dw_ref[2:3, _blk(k)]

    @pl.when(i < N_PT)
    def _():
        first = lax.rem(i, TILES_PER_SEQ) == 0
        ext_ref[0:HALO_S, :] = jnp.where(first, 0.0, ext_ref[0:HALO_S, :])
        p = _dot(xn_ref[...], wb_ref[...])
        for k in range(nblk):
            w0, w1, w2 = taps(k)
            for c in range(TM // EP_R):
                rows = slice(c * EP_R, (c + 1) * EP_R)
                r0 = HALO_S + c * EP_R
                ext_ref[r0:r0 + EP_R, _blk(k)] = p[rows, _blk(3 * k + 1)] * p[rows, _blk(3 * k + 2)]
                z = (w0 * ext_ref[r0 - 2:r0 - 2 + EP_R, _blk(k)]
                     + w1 * ext_ref[r0 - 1:r0 - 1 + EP_R, _blk(k)]
                     + w2 * ext_ref[r0:r0 + EP_R, _blk(k)])
                act_ref[rows, _blk(k)] = (p[rows, _blk(3 * k)] * z).astype(bf16)
        stp_ref[i // TILES_PER_SEQ] = ext_ref[HALO_S + TM - (SCONV_K - 1):HALO_S + TM, :]
        ext_ref[0:HALO_S, :] = ext_ref[TM:TM + HALO_S, :]

    @pl.when(i == N_PT)
    def _():
        p = _dot(xn_ref[0:M_S, :], wb_ref[...])
        for k in range(nblk):
            w0, w1, w2 = taps(k)
            s = p[:, _blk(3 * k + 1)] * p[:, _blk(3 * k + 2)]
            xp = [hist_ref[:, _blk(k)], hist_ref[:, D_BR + k * LANES:D_BR + (k + 1) * LANES]]
            xp += [s[t * DEC_BATCH:(t + 1) * DEC_BATCH] for t in range(DEC_SEQ)]
            for t in range(DEC_SEQ):
                z = w0 * xp[t] + w1 * xp[t + 1] + w2 * xp[t + 2]
                rows = slice(t * DEC_BATCH, (t + 1) * DEC_BATCH)
                act_ref[rows, _blk(k)] = (p[rows, _blk(3 * k)] * z).astype(bf16)
            sts_ref[:, _blk(k)] = xp[DEC_SEQ]
            sts_ref[:, D_BR + k * LANES:D_BR + (k + 1) * LANES] = xp[DEC_SEQ + 1]


def _stage_c(xn, w_in, sconv_dw, state, l):
    npast = SCONV_K - 1
    hist = state.reshape(DEPTH, DEC_BATCH, npast * D_BR)
    wspec = lambda cb: _once((None, D_MODEL, D_BR), lambda i: (l, 0, cb))
    act, stp, sts = pl.pallas_call(
        _stage_c_kernel,
        grid=(N_TILES,),
        in_specs=[
            pl.BlockSpec((TM, D_MODEL), lambda i: (i, 0)),
            wspec(4), wspec(5), wspec(6),
            pl.BlockSpec((None, SCONV_K, D_BR), lambda i: (l, 0, 0)),
            pl.BlockSpec((None, DEC_BATCH, npast * D_BR), lambda i: (l, 0, 0)),
        ],
        out_specs=[
            pl.BlockSpec((TM, D_BR), lambda i: (i, 0)),
            pl.BlockSpec((BATCH, npast, D_BR), lambda i: (0, 0, 0)),
            pl.BlockSpec((DEC_BATCH, npast * D_BR), lambda i: (0, 0)),
        ],
        out_shape=[
            jax.ShapeDtypeStruct((M_ALL, D_BR), bf16),
            jax.ShapeDtypeStruct((BATCH, npast, D_BR), f32),
            jax.ShapeDtypeStruct((DEC_BATCH, npast * D_BR), f32),
        ],
        scratch_shapes=[
            pltpu.VMEM((D_MODEL, 3 * D_BR), bf16),
            pltpu.VMEM((HALO_S + TM, D_BR), f32),
        ],
        compiler_params=_cparams(1),
        name="branch_c",
    )(xn, w_in, w_in, w_in, sconv_dw, hist)
    return act, stp, sts.reshape(DEC_BATCH, npast, D_BR)


def _stage_d_kernel(first_layer, *refs):
    if first_layer:
        (xp_ref, xs_ref, g_ref, w_ref, pw_ref, psc_ref, hist_ref,
         act_ref, stp_ref, new_ref, xn_ref, wb_ref, pwb_ref, ext_ref) = refs
    else:
        (xp_ref, g_ref, w_ref, pw_ref, psc_ref, hist_ref,
         act_ref, stp_ref, new_ref, xn_ref, wb_ref, pwb_ref, ext_ref) = refs
    i = pl.program_id(0)
    gdim = D_BR // len(POOL_WINDOWS)
    gain = g_ref[...]

    def put_xn(r, n, x):
        xn_ref[pl.ds(r, n), :] = _rms(x, gain).astype(bf16)

    @pl.when(i == 0)
    def _():
        _cast_rows(w_ref, wb_ref)
        pwb_ref[...] = pw_ref[...].astype(bf16)

    @pl.when(i < N_PT)
    def _():
        tis = i % TILES_PER_SEQ

        @pl.when(tis == 0)
        def _():
            ext_ref[0:HALO_P, :] = jnp.zeros((HALO_P, D_BR), f32)

        _norm_rows(xp_ref, put_xn, TM)
        ext_ref[HALO_P:HALO_P + TM, :] = _dot(xn_ref[...], wb_ref[...])
        pos1 = tis * TM + 1 + lax.broadcasted_iota(jnp.int32, (TM, gdim), 0)
        for g, win in enumerate(POOL_WINDOWS):
            lanes = slice(g * gdim, (g + 1) * gdim)
            cur = ext_ref[HALO_P:HALO_P + TM, lanes]
            tot = cur
            for k in range(1, win):
                tot = tot + ext_ref[HALO_P - k:HALO_P - k + TM, lanes]
            cnt = jnp.minimum(pos1, win).astype(f32)
            pm = (tot / cnt - cur).astype(bf16)
            act_ref[:, lanes] = (_dot(pm, pwb_ref[g]) * psc_ref[:, lanes]).astype(bf16)

        @pl.when(tis == TILES_PER_SEQ - 1)
        def _():
            stp_ref[i // TILES_PER_SEQ] = ext_ref[HALO_P + TM - POOL_PAST:HALO_P + TM, :]

        ext_ref[0:HALO_P, :] = ext_ref[TM:TM + HALO_P, :]

    @pl.when(i == N_PT)
    def _():
        if first_layer:
            for t in range(DEC_SEQ):
                xn_ref[t * DEC_BATCH:(t + 1) * DEC_BATCH, :] = (
                    _rms(xs_ref[:, t, :], gain).astype(bf16))
        else:
            _norm_rows(xp_ref, put_xn, M_S)
        p = _dot(xn_ref[0:M_S, :], wb_ref[...])

        def slab(s, lanes):
            if s < POOL_PAST:
                return hist_ref[s, :, lanes]
            r0 = (s - POOL_PAST) * DEC_BATCH
            return p[r0:r0 + DEC_BATCH, lanes]

        for t in range(DEC_SEQ):
            rows = slice(t * DEC_BATCH, (t + 1) * DEC_BATCH)
            for g, win in enumerate(POOL_WINDOWS):
                lanes = slice(g * gdim, (g + 1) * gdim)
                cur = slab(POOL_PAST + t, lanes)
                tot = cur
                for k in range(1, win):
                    tot = tot + slab(POOL_PAST + t - k, lanes)
                cnt = float(min(win, PAST_LEN + t + 1))
                pm = (tot / cnt - cur).astype(bf16)
                act_ref[rows, lanes] = (_dot(pm, pwb_ref[g]) * psc_ref[:, lanes]).astype(bf16)
        for t in range(DEC_SEQ):
            new_ref[:, t * D_BR:(t + 1) * D_BR] = p[t * DEC_BATCH:(t + 1) * DEC_BATCH]


def _stage_d(w_in, pool_w, pool_scale, state, norm_w, l, x=None, x_prompt=None, x_sample=None):
    ngroup = len(POOL_WINDOWS)
    gdim = D_BR // ngroup
    first_layer = x is None
    if first_layer:
        xs = [x_prompt.reshape(M_P, D_MODEL), x_sample]
        x_specs = [
            pl.BlockSpec((TM, D_MODEL), lambda i: (jnp.minimum(i, N_PT - 1), 0)),
            pl.BlockSpec((DEC_BATCH, DEC_SEQ, D_MODEL), lambda i: (0, 0, 0)),
        ]
    else:
        xs = [x]
        x_specs = [pl.BlockSpec((TM, D_MODEL), lambda i: (i, 0))]
    return pl.pallas_call(
        functools.partial(_stage_d_kernel, first_layer),
        grid=(N_TILES,),
        in_specs=x_specs + [
            pl.BlockSpec((None, 1, D_MODEL), lambda i: (l, 0, 0)),
            _once((None, D_MODEL, D_BR), lambda i: (l, 0, 7)),
            pl.BlockSpec((None, ngroup, gdim, gdim), lambda i: (l, 0, 0, 0)),
            pl.BlockSpec((None, 1, D_BR), lambda i: (l, 0, 0)),
            _once((None, POOL_PAST, DEC_BATCH, D_BR), lambda i: (l, 0, 0, 0)),
        ],
        out_specs=[
            pl.BlockSpec((TM, D_BR), lambda i: (i, 0)),
            pl.BlockSpec((BATCH, POOL_PAST, D_BR), lambda i: (0, 0, 0)),
            pl.BlockSpec((DEC_BATCH, DEC_SEQ * D_BR), lambda i: (0, 0)),
            pl.BlockSpec((TM, D_MODEL), lambda i: (i, 0)),
        ],
        out_shape=[
            jax.ShapeDtypeStruct((M_ALL, D_BR), bf16),
            jax.ShapeDtypeStruct((BATCH, POOL_PAST, D_BR), f32),
            jax.ShapeDtypeStruct((DEC_BATCH, DEC_SEQ * D_BR), f32),
            jax.ShapeDtypeStruct((M_ALL, D_MODEL), bf16),
        ],
        scratch_shapes=[
            pltpu.VMEM((D_MODEL, D_BR), bf16),
            pltpu.VMEM((ngroup, gdim, gdim), bf16),
            pltpu.VMEM((HALO_P + TM, D_BR), f32),
        ],
        compiler_params=_cparams(1),
        name="branch_d",
    )(*xs, norm_w.reshape(DEPTH, 1, D_MODEL), w_in, pool_w, pool_scale.reshape(DEPTH, 1, D_BR),
      _time_major(state))


def _stage_g_kernel(xn_ref, a0_ref, a1_ref, a2_ref, a3_ref,
                    g0_ref, g1_ref, g2_ref, g3_ref, wbr_ref,
                    out_ref, wgb_ref, wbb_ref):
    i = pl.program_id(1)

    @pl.when(i == 0)
    def _():
        for b, g_ref in enumerate((g0_ref, g1_ref, g2_ref, g3_ref)):
            _cast_rows(g_ref, wgb_ref, b * TN_G)
        wbb_ref[...] = wbr_ref[...].astype(bf16)

    def compute(rows):
        merged = None
        for b, a_ref in enumerate((a0_ref, a1_ref, a2_ref, a3_ref)):
            gate = _sigmoid(_dot(xn_ref[rows, :], wgb_ref[:, b * TN_G:(b + 1) * TN_G]))
            term = gate * _dot(a_ref[rows, :], wbb_ref[b])
            merged = term if merged is None else merged + term
        out_ref[rows, :] = merged.astype(bf16)

    @pl.when(i > 0)
    def _():
        compute(slice(0, TM))

    @pl.when(i == 0)
    def _():
        compute(slice(0, M_S))


def _stage_g(xn, acts, w_in, w_branch, l):
    gate_blk0 = GATE_COL0 // TN_G
    per_branch = D_MODEL // TN_G
    gspec = lambda b: pl.BlockSpec(
        (None, D_MODEL, TN_G), lambda c, i: (l, 0, gate_blk0 + b * per_branch + c))
    aspec = pl.BlockSpec((TM, D_BR), lambda c, i: (_sample_first(i), 0))
    return pl.pallas_call(
        _stage_g_kernel,
        grid=(D_MODEL // TN_G, N_TILES),
        in_specs=[
            pl.BlockSpec((TM, D_MODEL), lambda c, i: (_sample_first(i), 0)),
            aspec, aspec, aspec, aspec,
            gspec(0), gspec(1), gspec(2), gspec(3),
            pl.BlockSpec((None, N_BRANCH, D_BR, TN_G), lambda c, i: (l, 0, 0, c)),
        ],
        out_specs=pl.BlockSpec((TM, TN_G), lambda c, i: (_sample_first(i), c)),
        out_shape=jax.ShapeDtypeStruct((M_ALL, D_MODEL), bf16),
        scratch_shapes=[
            pltpu.VMEM((D_MODEL, N_BRANCH * TN_G), bf16),
            pltpu.VMEM((N_BRANCH, D_BR, TN_G), bf16),
        ],
        compiler_params=_cparams(2),
        name="gate_merge",
    )(xn, *acts, w_in, w_in, w_in, w_in, w_branch)


def _stage_res_kernel(lhs_ref, w_ref, x_ref, o_ref, wb_ref):
    i = pl.program_id(1)

    @pl.when(i == 0)
    def _():
        _cast_rows(w_ref, wb_ref)

    @pl.when(i < N_PT)
    def _():
        o_ref[...] = x_ref[...] + _dot(lhs_ref[...], wb_ref[...])

    @pl.when(i == N_PT)
    def _():
        o_ref[0:M_S, :] = x_ref[0:M_S, :] + _dot(lhs_ref[0:M_S, :], wb_ref[...])


def _stage_res(lhs, w, x, l, tn, name, single_buffer_w):
    k = lhs.shape[1]
    wshape = (None, k, tn)
    wmap = lambda c, i: (l, 0, c)
    wspec = _once(wshape, wmap) if single_buffer_w else pl.BlockSpec(wshape, wmap)
    return pl.pallas_call(
        _stage_res_kernel,
        grid=(D_MODEL // tn, N_TILES),
        in_specs=[
            pl.BlockSpec((TM, k), lambda c, i: (i, 0)),
            wspec,
            pl.BlockSpec((TM, tn), lambda c, i: (i, c)),
        ],
        out_specs=pl.BlockSpec((TM, tn), lambda c, i: (i, c)),
        out_shape=jax.ShapeDtypeStruct((M_ALL, D_MODEL), f32),
        scratch_shapes=[pltpu.VMEM((k, tn), bf16)],
        compiler_params=_cparams(2),
        name=name,
    )(lhs, w, x)


def _stage_o_kernel(first_layer, *refs):
    if first_layer:
        m_ref, w_ref, xp_ref, xs_ref, g_ref, x_ref, xn_ref, wb_ref = refs
    else:
        m_ref, w_ref, xin_ref, g_ref, x_ref, xn_ref, wb_ref = refs
    i = pl.program_id(0)

    @pl.when(i == 0)
    def _():
        _cast_rows(w_ref, wb_ref)

    g = g_ref[...]

    def finish(r0, nrows, resid_rows, proj):
        for c in range(nrows // LN_R):
            rows = slice(r0 + c * LN_R, r0 + (c + 1) * LN_R)
            x = resid_rows(slice(c * LN_R, (c + 1) * LN_R)) + proj[rows]
            x_ref[rows, :] = x
            xn_ref[rows, :] = _rms(x, g).astype(bf16)

    if first_layer:
        @pl.when(i < M_P // TM_O)
        def _():
            finish(0, TM_O, lambda r: xp_ref[r, :], _dot(m_ref[...], wb_ref[...]))

        @pl.when(i >= M_P // TM_O)
        def _():
            proj = _dot(m_ref[...], wb_ref[...])
            for t in range(DEC_SEQ):
                finish(t * DEC_BATCH, DEC_BATCH,
                       lambda r, t=t: xs_ref[r, t, :], proj)
    else:
        finish(0, TM_O, lambda r: xin_ref[r, :], _dot(m_ref[...], wb_ref[...]))


def _stage_o(merged, w_o, norm_w, l, x=None, x_prompt=None, x_sample=None):
    first_layer = x is None
    n_steps = M_ALL // TM_O
    row_blk = pl.BlockSpec((TM_O, D_MODEL), lambda i: (i, 0))
    if first_layer:
        n_p = M_P // TM_O
        resid = [x_prompt.reshape(M_P, D_MODEL), x_sample]
        resid_specs = [
            pl.BlockSpec((TM_O, D_MODEL), lambda i: (jnp.minimum(i, n_p - 1), 0)),
            pl.BlockSpec((DEC_BATCH, DEC_SEQ, D_MODEL), lambda i: (0, 0, 0)),
        ]
    else:
        resid = [x]
        resid_specs = [row_blk]
    return pl.pallas_call(
        functools.partial(_stage_o_kernel, first_layer),
        grid=(n_steps,),
        in_specs=[row_blk, _once((None, D_MODEL, D_MODEL), lambda i: (l, 0, 0))]
        + resid_specs + [pl.BlockSpec((None, 1, D_MODEL), lambda i: (l, 0, 0))],
        out_specs=[row_blk, row_blk],
        out_shape=[
            jax.ShapeDtypeStruct((M_ALL, D_MODEL), f32),
            jax.ShapeDtypeStruct((M_ALL, D_MODEL), bf16),
        ],
        scratch_shapes=[pltpu.VMEM((D_MODEL, D_MODEL), bf16)],
        compiler_params=_cparams(1),
        name="out_proj_norm",
    )(merged, w_o, *resid, norm_w.reshape(DEPTH, 1, D_MODEL))


def _stage_u_kernel(xn_ref, wg_ref, wv_ref, dwg_ref, dwv_ref,
                    hgh_ref, hvh_ref,
                    a_ref, stp_ref, sts_ref, wb_ref, ext_ref):
    i = pl.program_id(1)
    nblk = TK_U // U_BLK
    npast = FFN_K - 1

    def _blk(k):
        return slice(k * U_BLK, (k + 1) * U_BLK)

    @pl.when(i == 0)
    def _():
        moves = []
        for k in range(nblk):
            moves.append((wg_ref, k * U_BLK, 2 * k * U_BLK, U_BLK))
            moves.append((wv_ref, k * U_BLK, (2 * k + 1) * U_BLK, U_BLK))
        _cast_blocks(moves, wb_ref)
        ext_ref[0:HALO_S, :] = jnp.zeros((HALO_S, 2 * TK_U), f32)

    def taps(dw_ref, k):
        return dw_ref[0:1, _blk(k)], dw_ref[1:2, _blk(k)], dw_ref[2:3, _blk(k)]

    @pl.when(i > 0)
    def _():
        first = lax.rem(i - 1, TILES_PER_SEQ) == 0
        ext_ref[0:HALO_S, :] = jnp.where(first, 0.0, ext_ref[0:HALO_S, :])
        h = _dot(xn_ref[...], wb_ref[...])
        ext_ref[HALO_S:HALO_S + TM, :] = h
        for k in range(nblk):
            for c in range(TM // U_R):
                r0 = HALO_S + c * U_R

                def conv(dw_ref, lanes):
                    w0, w1, w2 = taps(dw_ref, k)
                    return (w0 * ext_ref[r0 - 2:r0 - 2 + U_R, lanes]
                            + w1 * ext_ref[r0 - 1:r0 - 1 + U_R, lanes]
                            + w2 * h[c * U_R:(c + 1) * U_R, lanes])

                hg = conv(dwg_ref, _blk(2 * k))
                hv = conv(dwv_ref, _blk(2 * k + 1))
                a_ref[c * U_R:(c + 1) * U_R, _blk(k)] = (_silu(hg) * hv).astype(bf16)
        b = (i - 1) // TILES_PER_SEQ
        last = slice(HALO_S + TM - npast, HALO_S + TM)
        for k in range(nblk):
            stp_ref[b, 0, :, _blk(k)] = ext_ref[last, _blk(2 * k)]
            stp_ref[b, 1, :, _blk(k)] = ext_ref[last, _blk(2 * k + 1)]
        ext_ref[0:HALO_S, :] = ext_ref[TM:TM + HALO_S, :]

    @pl.when(i == 0)
    def _():
        h = _dot(xn_ref[0:M_S, :], wb_ref[...])

        def conv_slabs(dw_ref, hist_ref, k, lanes):
            w0, w1, w2 = taps(dw_ref, k)
            xp = [hist_ref[:, r, _blk(k)] for r in range(npast)]
            xp += [h[t * DEC_BATCH:(t + 1) * DEC_BATCH, lanes] for t in range(DEC_SEQ)]
            return [w0 * xp[t] + w1 * xp[t + 1] + w2 * xp[t + 2] for t in range(DEC_SEQ)]

        for k in range(nblk):
            hg = conv_slabs(dwg_ref, hgh_ref, k, _blk(2 * k))
            hv = conv_slabs(dwv_ref, hvh_ref, k, _blk(2 * k + 1))
            for t in range(DEC_SEQ):
                a_ref[t * DEC_BATCH:(t + 1) * DEC_BATCH, _blk(k)] = (
                    _silu(hg[t]) * hv[t]).astype(bf16)
            for r in range(npast):
                t = DEC_SEQ - npast + r
                rows = slice(t * DEC_BATCH, (t + 1) * DEC_BATCH)
                sts_ref[r, 0, :, _blk(k)] = h[rows, _blk(2 * k)]
                sts_ref[r, 1, :, _blk(k)] = h[rows, _blk(2 * k + 1)]


def _stage_u(xn, ffn_up, ffn_dw, state, l):
    nk = D_FF // TK_U
    npast = FFN_K - 1
    wspec = lambda half: pl.BlockSpec(
        (None, D_MODEL, TK_U), lambda k, i: (l, 0, half * nk + k))
    dspec = lambda half: pl.BlockSpec(
        (None, FFN_K, TK_U), lambda k, i: (l, 0, half * nk + k))
    hspec = lambda half: pl.BlockSpec(
        (None, DEC_BATCH, npast, TK_U), lambda k, i: (l, 0, 0, half * nk + k))
    a, stp, sts = pl.pallas_call(
        _stage_u_kernel,
        grid=(nk, N_TILES),
        in_specs=[
            pl.BlockSpec((TM, D_MODEL), lambda k, i: (_sample_first(i), 0)),
            wspec(0), wspec(1), dspec(0), dspec(1),
            hspec(0), hspec(1),
        ],
        out_specs=[
            pl.BlockSpec((TM, TK_U), lambda k, i: (_sample_first(i), k)),
            pl.BlockSpec((BATCH, 2, npast, TK_U), lambda k, i: (0, 0, 0, k)),
            pl.BlockSpec((npast, 2, DEC_BATCH, TK_U), lambda k, i: (0, 0, 0, k)),
        ],
        out_shape=[
            jax.ShapeDtypeStruct((M_ALL, D_FF), bf16),
            jax.ShapeDtypeStruct((BATCH, 2, npast, D_FF), f32),
            jax.ShapeDtypeStruct((npast, 2, DEC_BATCH, D_FF), f32),
        ],
        scratch_shapes=[
            pltpu.VMEM((D_MODEL, 2 * TK_U), bf16),
            pltpu.VMEM((HALO_S + TM, 2 * TK_U), f32),
        ],
        compiler_params=_cparams(2),
        name="ffn_up",
    )(xn, ffn_up, ffn_up, ffn_dw, ffn_dw, state, state)
    stp = jnp.transpose(stp, (0, 2, 1, 3)).reshape(BATCH, npast, 2 * D_FF)
    return a, stp, sts


def _ffn_state_kernel(*refs):
    in_refs, out_ref = refs[:-1], refs[-1]
    l = pl.program_id(0)
    for d, in_ref in enumerate(in_refs):
        @pl.when(l == d)
        def _(in_ref=in_ref):
            for r in range(FFN_K - 1):
                out_ref[:, r, :] = in_ref[r]


def _ffn_state(per_layer):
    npast = FFN_K - 1
    ncb = D_FF // FFN_ST_W
    spec = pl.BlockSpec((npast, None, DEC_BATCH, FFN_ST_W), lambda l, h, c: (0, h, 0, c))
    return pl.pallas_call(
        _ffn_state_kernel,
        grid=(DEPTH, 2, ncb),
        in_specs=[spec] * DEPTH,
        out_specs=pl.BlockSpec((None, DEC_BATCH, npast, FFN_ST_W),
                               lambda l, h, c: (l, 0, 0, h * ncb + c)),
        out_shape=jax.ShapeDtypeStruct((DEPTH, DEC_BATCH, npast, 2 * D_FF), f32),
        compiler_params=_cparams(3),
        name="ffn_state",
    )(*per_layer)


def kernel(x_prompt, x_sample, state_conf_conv, state_sconv, state_pool, state_ffn_conv,
           norm_mix, w_in, conf_dw, conf_ln_g, conf_ln_b, gmlp_ln_g, gmlp_ln_b, gmlp_ws,
           gmlp_b, sconv_dw, pool_w, pool_scale, w_branch, w_o, norm_ffn, ffn_up, ffn_dw,
           ffn_down, norm_final):
    conf_p, conf_new, sconv_p, sconv_s, pool_p, pool_new, ffn_p, ffn_new, v_s = (
        [] for _ in range(9))
    x = None
    for l in range(DEPTH):
        if l == 0:
            act_d, st_p, new, xn = _stage_d(w_in, pool_w, pool_scale, state_pool, norm_mix, l,
                                            x_prompt=x_prompt, x_sample=x_sample)
        else:
            act_d, st_p, new, xn = _stage_d(w_in, pool_w, pool_scale, state_pool, norm_mix, l, x=x)
        pool_p.append(st_p)
        pool_new.append(new)
        act_a, st_p, new = _stage_a(xn, w_in, conf_dw, conf_ln_g, conf_ln_b, state_conf_conv, l)
        conf_p.append(st_p)
        conf_new.append(new)
        act_b, v = _stage_b(xn, w_in, gmlp_ln_g, gmlp_ln_b, gmlp_ws, gmlp_b, l)
        v_s.append(v)
        act_c, st_p, st_s = _stage_c(xn, w_in, sconv_dw, state_sconv, l)
        sconv_p.append(st_p)
        sconv_s.append(st_s)
        merged = _stage_g(xn, (act_a, act_b, act_c, act_d), w_in, w_branch, l)
        if l == 0:
            x, xn = _stage_o(merged, w_o, norm_ffn, l, x_prompt=x_prompt, x_sample=x_sample)
        else:
            x, xn = _stage_o(merged, w_o, norm_ffn, l, x=x)
        a, st_p, new = _stage_u(xn, ffn_up, ffn_dw, state_ffn_conv, l)
        ffn_p.append(st_p)
        ffn_new.append(new)
        x = _stage_res(a, ffn_down, x, l, TN_D, "ffn_down", True)
    y_prompt, y_sample = _final_norm(x, norm_final)
    conf_s = _roll_state(state_conf_conv, conf_new)
    pool_s = _roll_state(state_pool, pool_new)
    ffn_s = _ffn_state(ffn_new)
    st = jnp.stack
    return (y_prompt, y_sample, st(conf_p), conf_s, st(sconv_p), st(sconv_s),
            st(pool_p), pool_s, st(ffn_p), ffn_s, st(v_s))
```

```python
import functools

import jax
import jax.numpy as jnp
from jax import lax
from jax.experimental import pallas as pl
from jax.experimental.pallas import tpu as pltpu

D_MODEL = 2048
BATCH = 4
SEQ = 2048
DEPTH = 2
DEC_BATCH = 128
DEC_SEQ = 4
PAST_LEN = 16384
D_BR = 512
N_BRANCH = 4
CONF_K = 31
GMLP_HEADS = 4
HEAD_DIM = D_BR // GMLP_HEADS
CHUNK = 128
SCONV_K = 3
POOL_WINDOWS = (2, 4, 8, 16)
POOL_PAST = 15
D_FF = 5632
FFN_K = 3
EPS = 1e-6
GATE_COL0 = 8 * D_BR

TM = 1024
M_P = BATCH * SEQ
M_S = DEC_BATCH * DEC_SEQ
M_ALL = M_P + M_S
N_PT = M_P // TM
TILES_PER_SEQ = SEQ // TM
N_TILES = N_PT + 1

LANES = 128
HALO_A = 32
HALO_P = 16
HALO_S = 8
CONV_R = 128
EP_R = 128
LN_R = 64
ROLL_B = 128
FFN_ST_W = 2816
U_R = 1024
U_BLK = 512

TN_G = 256
TM_O = 512
TK_U = 512
TN_D = 512

VMEM_LIMIT = 60000 * 1024

f32 = jnp.float32
bf16 = jnp.bfloat16


def _cparams(n_axes):
    return pltpu.CompilerParams(
        dimension_semantics=("arbitrary",) * n_axes, vmem_limit_bytes=VMEM_LIMIT)


def _once(block_shape, index_map):
    return pl.BlockSpec(block_shape, index_map, pipeline_mode=pl.Buffered(1))


def _rms(x, g):
    return x * lax.rsqrt(jnp.mean(x * x, axis=-1, keepdims=True) + EPS) * g


def _layernorm(x, g, b):
    mu = jnp.mean(x, axis=-1, keepdims=True)
    d = x - mu
    var = jnp.mean(d * d, axis=-1, keepdims=True)
    return d * lax.rsqrt(var + EPS) * g + b


def _sigmoid(x):
    return 1.0 / (1.0 + jnp.exp(-x))


def _silu(x):
    h = 0.5 * x
    return h + h * jnp.tanh(h)


def _gelu_tanh(x):
    c = 0.7978845608028654
    return 0.5 * x * (1.0 + jnp.tanh(c * (x + 0.044715 * (x * x * x))))


def _cast_blocks(moves, wb_ref, rows_per_step=256):
    k = wb_ref.shape[0]

    def body(c, carry):
        r = pl.multiple_of(c * rows_per_step, rows_per_step)
        for w_ref, src, dst, n in moves:
            wb_ref[pl.ds(r, rows_per_step), dst:dst + n] = (
                w_ref[pl.ds(r, rows_per_step), src:src + n].astype(bf16))
        return carry

    lax.fori_loop(0, k // rows_per_step, body, 0)


def _cast_rows(w_ref, wb_ref, col0=0):
    _cast_blocks([(w_ref, 0, col0, w_ref.shape[1])], wb_ref)


def _dot(a, b):
    return jnp.dot(a, b, preferred_element_type=f32)


def _blk(k):
    return slice(k * LANES, (k + 1) * LANES)


def _sample_first(i):
    return lax.rem(i + N_PT, N_TILES)


def _norm_rows(src_ref, put, rows, chunk=256):
    def body(c, carry):
        r = pl.multiple_of(c * chunk, chunk)
        put(r, chunk, src_ref[pl.ds(r, chunk), :])
        return carry

    lax.fori_loop(0, rows // chunk, body, 0)


def _final_norm_kernel(x_ref, g_ref, yp_ref, ys_ref):
    i = pl.program_id(0)
    g = g_ref[...]

    @pl.when(i < N_PT)
    def _():
        def put(r, n, x):
            yp_ref[pl.ds(r, n), :] = _rms(x, g)
        _norm_rows(x_ref, put, TM)

    @pl.when(i == N_PT)
    def _():
        for t in range(DEC_SEQ):
            x = x_ref[t * DEC_BATCH:(t + 1) * DEC_BATCH, :]
            ys_ref[:, t, :] = _rms(x, g)


def _final_norm(x, norm_w):
    yp, ys = pl.pallas_call(
        _final_norm_kernel,
        grid=(N_TILES,),
        in_specs=[
            pl.BlockSpec((TM, D_MODEL), lambda i: (i, 0)),
            pl.BlockSpec((1, D_MODEL), lambda i: (0, 0)),
        ],
        out_specs=[
            pl.BlockSpec((TM, D_MODEL), lambda i: (jnp.minimum(i, N_PT - 1), 0)),
            pl.BlockSpec((DEC_BATCH, DEC_SEQ, D_MODEL), lambda i: (0, 0, 0)),
        ],
        out_shape=[
            jax.ShapeDtypeStruct((M_P, D_MODEL), f32),
            jax.ShapeDtypeStruct((DEC_BATCH, DEC_SEQ, D_MODEL), f32),
        ],
        compiler_params=_cparams(1),
        name="final_norm",
    )(x, norm_w.reshape(1, D_MODEL))
    return yp.reshape(BATCH, SEQ, D_MODEL), ys


def _time_major(state):
    return jnp.transpose(state, (0, 2, 1, 3))


def _roll_state_kernel(old_ref, *refs):
    new_refs, out_ref = refs[:-1], refs[-1]
    l = pl.program_id(0)
    npast, width = old_ref.shape[0], old_ref.shape[2]
    keep = npast - DEC_SEQ
    for s in range(keep):
        out_ref[s] = old_ref[DEC_SEQ + s]
    for d, new_ref in enumerate(new_refs):
        @pl.when(l == d)
        def _(new_ref=new_ref):
            for t in range(DEC_SEQ):
                out_ref[keep + t] = new_ref[:, t * width:(t + 1) * width]


def _roll_state(old, new_rows):
    old_t = _time_major(old)
    npast, width = old_t.shape[1], old_t.shape[3]
    blk = pl.BlockSpec((None, npast, ROLL_B, width), lambda l, j: (l, 0, j, 0))
    nspec = pl.BlockSpec((ROLL_B, DEC_SEQ * width), lambda l, j: (j, 0))
    out = pl.pallas_call(
        _roll_state_kernel,
        grid=(DEPTH, DEC_BATCH // ROLL_B),
        in_specs=[blk] + [nspec] * DEPTH,
        out_specs=blk,
        out_shape=jax.ShapeDtypeStruct(old_t.shape, f32),
        compiler_params=_cparams(2),
        name="roll_state",
    )(old_t, *new_rows)
    return _time_major(out)


def _conv31_chunk(ext_ref, base, dw_ref, lanes):
    y = None
    for r in range(8):
        rows = CONV_R if r == 0 else CONV_R + 8
        z = None
        for a in range(5):
            j = 8 * a + r - 2
            if 0 <= j < CONF_K:
                term = dw_ref[pl.ds(j, 1), lanes] * ext_ref[pl.ds(base + 8 * a, rows), lanes]
                z = term if z is None else z + term
        zr = z[r:r + CONV_R]
        y = zr if y is None else y + zr
    return y


def _stage_a_kernel(xn_ref, w_ref, dw_ref, lng_ref, lnb_ref, hist_ref,
                    act_ref, stp_ref, new_ref, wb_ref, ext_ref, y_ref):
    i = pl.program_id(0)
    nblk = D_BR // LANES
    npast = CONF_K - 1

    @pl.when(i == 0)
    def _():
        moves = []
        for k in range(nblk):
            moves.append((w_ref, k * LANES, 2 * k * LANES, LANES))
            moves.append((w_ref, D_BR + k * LANES, (2 * k + 1) * LANES, LANES))
        _cast_blocks(moves, wb_ref)
        ext_ref[0:HALO_A, :] = jnp.zeros((HALO_A, D_BR), f32)

    lng = lng_ref[...]
    lnb = lnb_ref[...]

    def glu_block(p, k, rows):
        return p[rows, _blk(2 * k)] * _sigmoid(p[rows, _blk(2 * k + 1)])

    @pl.when(i < N_PT)
    def _():
        first = lax.rem(i, TILES_PER_SEQ) == 0
        ext_ref[0:HALO_A, :] = jnp.where(first, 0.0, ext_ref[0:HALO_A, :])
        p = _dot(xn_ref[...], wb_ref[...])
        for k in range(nblk):
            for c in range(TM // CONV_R):
                rows = slice(c * CONV_R, (c + 1) * CONV_R)
                ext_ref[HALO_A + c * CONV_R:HALO_A + (c + 1) * CONV_R, _blk(k)] = (
                    glu_block(p, k, rows))
            for c in range(TM // CONV_R):
                y_ref[c * CONV_R:(c + 1) * CONV_R, _blk(k)] = (
                    _conv31_chunk(ext_ref, c * CONV_R, dw_ref, _blk(k)))
        for c in range(TM // LN_R):
            rows = slice(c * LN_R, (c + 1) * LN_R)
            act_ref[rows, :] = _silu(_layernorm(y_ref[rows, :], lng, lnb)).astype(bf16)
        stp_ref[i // TILES_PER_SEQ] = ext_ref[HALO_A + TM - npast:HALO_A + TM, :]
        ext_ref[0:HALO_A, :] = ext_ref[TM:TM + HALO_A, :]

    @pl.when(i == N_PT)
    def _():
        p = _dot(xn_ref[0:M_S, :], wb_ref[...])
        glu = jnp.concatenate([glu_block(p, k, slice(0, M_S)) for k in range(nblk)], axis=1)
        rb = 16
        for q in range(DEC_BATCH // rb):
            y = [None] * DEC_SEQ
            for s in range(npast + DEC_SEQ):
                if s < npast:
                    slab = hist_ref[s, q * rb:(q + 1) * rb, :]
                else:
                    r0 = (s - npast) * DEC_BATCH + q * rb
                    slab = glu[r0:r0 + rb]
                for t in range(DEC_SEQ):
                    j = s - t
                    if 0 <= j < CONF_K:
                        term = dw_ref[pl.ds(j, 1), :] * slab
                        y[t] = term if y[t] is None else y[t] + term
            for t in range(DEC_SEQ):
                r0 = t * DEC_BATCH + q * rb
                act_ref[r0:r0 + rb, :] = _silu(_layernorm(y[t], lng, lnb)).astype(bf16)
        for t in range(DEC_SEQ):
            new_ref[:, t * D_BR:(t + 1) * D_BR] = glu[t * DEC_BATCH:(t + 1) * DEC_BATCH]


def _stage_a(xn, w_in, conf_dw, ln_g, ln_b, state, l):
    npast = CONF_K - 1
    return pl.pallas_call(
        _stage_a_kernel,
        grid=(N_TILES,),
        in_specs=[
            pl.BlockSpec((TM, D_MODEL), lambda i: (i, 0)),
            _once((None, D_MODEL, 2 * D_BR), lambda i: (l, 0, 0)),
            pl.BlockSpec((None, CONF_K, D_BR), lambda i: (l, 0, 0)),
            pl.BlockSpec((None, 1, D_BR), lambda i: (l, 0, 0)),
            pl.BlockSpec((None, 1, D_BR), lambda i: (l, 0, 0)),
            _once((None, npast, DEC_BATCH, D_BR), lambda i: (l, 0, 0, 0)),
        ],
        out_specs=[
            pl.BlockSpec((TM, D_BR), lambda i: (i, 0)),
            pl.BlockSpec((BATCH, npast, D_BR), lambda i: (0, 0, 0)),
            pl.BlockSpec((DEC_BATCH, DEC_SEQ * D_BR), lambda i: (0, 0)),
        ],
        out_shape=[
            jax.ShapeDtypeStruct((M_ALL, D_BR), bf16),
            jax.ShapeDtypeStruct((BATCH, npast, D_BR), f32),
            jax.ShapeDtypeStruct((DEC_BATCH, DEC_SEQ * D_BR), f32),
        ],
        scratch_shapes=[
            pltpu.VMEM((D_MODEL, 2 * D_BR), bf16),
            pltpu.VMEM((HALO_A + TM, D_BR), f32),
            pltpu.VMEM((TM, D_BR), f32),
        ],
        compiler_params=_cparams(1),
        name="branch_a",
    )(xn, w_in, conf_dw, ln_g.reshape(DEPTH, 1, D_BR), ln_b.reshape(DEPTH, 1, D_BR),
      _time_major(state))


def _stage_b_kernel(xn_ref, w_ref, lng_ref, lnb_ref, ws_ref, bt_ref, wss_ref, bss_ref,
                    act_ref, v_ref, wb_ref, tril_ref, bias_ref):
    i = pl.program_id(0)

    @pl.when(i == 0)
    def _():
        _cast_blocks([(w_ref, D_BR, 0, D_BR), (w_ref, 0, D_BR, D_BR)], wb_ref)
        row = lax.broadcasted_iota(jnp.int32, (CHUNK, CHUNK), 0)
        col = lax.broadcasted_iota(jnp.int32, (CHUNK, CHUNK), 1)
        for h in range(GMLP_HEADS):
            tril_ref[h] = jnp.where(row >= col, ws_ref[h], 0.0).astype(bf16)
            bias_ref[h] = jnp.broadcast_to(bt_ref[:, h:h + 1], (CHUNK, HEAD_DIM))

    lng = lng_ref[...]
    lnb = lnb_ref[...]

    @pl.when(i < N_PT)
    def _():
        p = _dot(xn_ref[...], wb_ref[...])
        for c in range(TM // CHUNK):
            rows = slice(c * CHUNK, (c + 1) * CHUNK)
            u = _gelu_tanh(p[rows, D_BR:])
            v = _layernorm(_gelu_tanh(p[rows, :D_BR]), lng, lnb).astype(bf16)
            for h in range(GMLP_HEADS):
                lanes = slice(h * HEAD_DIM, (h + 1) * HEAD_DIM)
                mixed = _dot(tril_ref[h], v[:, lanes]) + bias_ref[h]
                act_ref[rows, lanes] = (u[:, lanes] * mixed).astype(bf16)

    @pl.when(i == N_PT)
    def _():
        p = _dot(xn_ref[0:M_S, :], wb_ref[...])
        u = _gelu_tanh(p[:, D_BR:])
        v = _layernorm(_gelu_tanh(p[:, :D_BR]), lng, lnb)
        for t in range(DEC_SEQ):
            v_ref[:, t * D_BR:(t + 1) * D_BR] = v[t * DEC_BATCH:(t + 1) * DEC_BATCH]
        for t in range(DEC_SEQ):
            rows = slice(t * DEC_BATCH, (t + 1) * DEC_BATCH)
            for h in range(GMLP_HEADS):
                lanes = slice(h * HEAD_DIM, (h + 1) * HEAD_DIM)
                mixed = jnp.full((DEC_BATCH, HEAD_DIM), bss_ref[h * DEC_SEQ + t], f32)
                for s in range(t + 1):
                    coef = wss_ref[(h * DEC_SEQ + t) * DEC_SEQ + s]
                    mixed = mixed + coef * v[s * DEC_BATCH:(s + 1) * DEC_BATCH, lanes]
                act_ref[rows, lanes] = (u[rows, lanes] * mixed).astype(bf16)


def _stage_b(xn, w_in, ln_g, ln_b, gmlp_ws, gmlp_b, l):
    bias_t = jnp.swapaxes(gmlp_b, 1, 2)
    ws_small = gmlp_ws[l, :, :DEC_SEQ, :DEC_SEQ].reshape(-1)
    b_small = gmlp_b[l, :, :DEC_SEQ].reshape(-1)
    act, v = pl.pallas_call(
        _stage_b_kernel,
        grid=(N_TILES,),
        in_specs=[
            pl.BlockSpec((TM, D_MODEL), lambda i: (i, 0)),
            _once((None, D_MODEL, 2 * D_BR), lambda i: (l, 0, 1)),
            pl.BlockSpec((None, 1, D_BR), lambda i: (l, 0, 0)),
            pl.BlockSpec((None, 1, D_BR), lambda i: (l, 0, 0)),
            pl.BlockSpec((None, GMLP_HEADS, CHUNK, CHUNK), lambda i: (l, 0, 0, 0)),
            pl.BlockSpec((None, CHUNK, GMLP_HEADS), lambda i: (l, 0, 0)),
            pl.BlockSpec(memory_space=pltpu.SMEM),
            pl.BlockSpec(memory_space=pltpu.SMEM),
        ],
        out_specs=[
            pl.BlockSpec((TM, D_BR), lambda i: (i, 0)),
            pl.BlockSpec((DEC_BATCH, DEC_SEQ * D_BR), lambda i: (0, 0)),
        ],
        out_shape=[
            jax.ShapeDtypeStruct((M_ALL, D_BR), bf16),
            jax.ShapeDtypeStruct((DEC_BATCH, DEC_SEQ * D_BR), f32),
        ],
        scratch_shapes=[
            pltpu.VMEM((D_MODEL, 2 * D_BR), bf16),
            pltpu.VMEM((GMLP_HEADS, CHUNK, CHUNK), bf16),
            pltpu.VMEM((GMLP_HEADS, CHUNK, HEAD_DIM), f32),
        ],
        compiler_params=_cparams(1),
        name="branch_b",
    )(xn, w_in, ln_g.reshape(DEPTH, 1, D_BR), ln_b.reshape(DEPTH, 1, D_BR),
      gmlp_ws, bias_t, ws_small, b_small)
    return act, v.reshape(DEC_BATCH, DEC_SEQ, D_BR)


def _stage_c_kernel(xn_ref, w0_ref, w1_ref, w2_ref, dw_ref, hist_ref,
                    act_ref, stp_ref, sts_ref, wb_ref, ext_ref):
    i = pl.program_id(0)
    nblk = D_BR // LANES

    @pl.when(i == 0)
    def _():
        moves = []
        for k in range(nblk):
            for part, w_ref in enumerate((w0_ref, w1_ref, w2_ref)):
                moves.append((w_ref, k * LANES, (3 * k + part) * LANES, LANES))
        _cast_blocks(moves, wb_ref)
        ext_ref[0:HALO_S, :] = jnp.zeros((HALO_S, D_BR), f32)

    def taps(k):
        return dw_ref[0:1, _blk(k)], dw_ref[1:2, _blk(k)], dw_ref[2:3, _blk(k)]

    @pl.when(i < N_PT)
    def _():
        first = lax.rem(i, TILES_PER_SEQ) == 0
        ext_ref[0:HALO_S, :] = jnp.where(first, 0.0, ext_ref[0:HALO_S, :])
        p = _dot(xn_ref[...], wb_ref[...])
        for k in range(nblk):
            w0, w1, w2 = taps(k)
            for c in range(TM // EP_R):
                rows = slice(c * EP_R, (c + 1) * EP_R)
                r0 = HALO_S + c * EP_R
                ext_ref[r0:r0 + EP_R, _blk(k)] = p[rows, _blk(3 * k + 1)] * p[rows, _blk(3 * k + 2)]
                z = (w0 * ext_ref[r0 - 2:r0 - 2 + EP_R, _blk(k)]
                     + w1 * ext_ref[r0 - 1:r0 - 1 + EP_R, _blk(k)]
                     + w2 * ext_ref[r0:r0 + EP_R, _blk(k)])
                act_ref[rows, _blk(k)] = (p[rows, _blk(3 * k)] * z).astype(bf16)
        stp_ref[i // TILES_PER_SEQ] = ext_ref[HALO_S + TM - (SCONV_K - 1):HALO_S + TM, :]
        ext_ref[0:HALO_S, :] = ext_ref[TM:TM + HALO_S, :]

    @pl.when(i == N_PT)
    def _():
        p = _dot(xn_ref[0:M_S, :], wb_ref[...])
        for k in range(nblk):
            w0, w1, w2 = taps(k)
            s = p[:, _blk(3 * k + 1)] * p[:, _blk(3 * k + 2)]
            xp = [hist_ref[:, _blk(k)], hist_ref[:, D_BR + k * LANES:D_BR + (k + 1) * LANES]]
            xp += [s[t * DEC_BATCH:(t + 1) * DEC_BATCH] for t in range(DEC_SEQ)]
            for t in range(DEC_SEQ):
                z = w0 * xp[t] + w1 * xp[t + 1] + w2 * xp[t + 2]
                rows = slice(t * DEC_BATCH, (t + 1) * DEC_BATCH)
                act_ref[rows, _blk(k)] = (p[rows, _blk(3 * k)] * z).astype(bf16)
            sts_ref[:, _blk(k)] = xp[DEC_SEQ]
            sts_ref[:, D_BR + k * LANES:D_BR + (k + 1) * LANES] = xp[DEC_SEQ + 1]


def _stage_c(xn, w_in, sconv_dw, state, l):
    npast = SCONV_K - 1
    hist = state.reshape(DEPTH, DEC_BATCH, npast * D_BR)
    wspec = lambda cb: _once((None, D_MODEL, D_BR), lambda i: (l, 0, cb))
    act, stp, sts = pl.pallas_call(
        _stage_c_kernel,
        grid=(N_TILES,),
        in_specs=[
            pl.BlockSpec((TM, D_MODEL), lambda i: (i, 0)),
            wspec(4), wspec(5), wspec(6),
            pl.BlockSpec((None, SCONV_K, D_BR), lambda i: (l, 0, 0)),
            pl.BlockSpec((None, DEC_BATCH, npast * D_BR), lambda i: (l, 0, 0)),
        ],
        out_specs=[
            pl.BlockSpec((TM, D_BR), lambda i: (i, 0)),
            pl.BlockSpec((BATCH, npast, D_BR), lambda i: (0, 0, 0)),
            pl.BlockSpec((DEC_BATCH, npast * D_BR), lambda i: (0, 0)),
        ],
        out_shape=[
            jax.ShapeDtypeStruct((M_ALL, D_BR), bf16),
            jax.ShapeDtypeStruct((BATCH, npast, D_BR), f32),
            jax.ShapeDtypeStruct((DEC_BATCH, npast * D_BR), f32),
        ],
        scratch_shapes=[
            pltpu.VMEM((D_MODEL, 3 * D_BR), bf16),
            pltpu.VMEM((HALO_S + TM, D_BR), f32),
        ],
        compiler_params=_cparams(1),
        name="branch_c",
    )(xn, w_in, w_in, w_in, sconv_dw, hist)
    return act, stp, sts.reshape(DEC_BATCH, npast, D_BR)


def _stage_d_kernel(first_layer, *refs):
    if first_layer:
        (xp_ref, xs_ref, g_ref, w_ref, pw_ref, psc_ref, hist_ref,
         act_ref, stp_ref, new_ref, xn_ref, wb_ref, pwb_ref, ext_ref) = refs
    else:
        (xp_ref, g_ref, w_ref, pw_ref, psc_ref, hist_ref,
         act_ref, stp_ref, new_ref, xn_ref, wb_ref, pwb_ref, ext_ref) = refs
    i = pl.program_id(0)
    gdim = D_BR // len(POOL_WINDOWS)
    gain = g_ref[...]

    def put_xn(r, n, x):
        xn_ref[pl.ds(r, n), :] = _rms(x, gain).astype(bf16)

    @pl.when(i == 0)
    def _():
        _cast_rows(w_ref, wb_ref)
        pwb_ref[...] = pw_ref[...].astype(bf16)

    @pl.when(i < N_PT)
    def _():
        tis = i % TILES_PER_SEQ

        @pl.when(tis == 0)
        def _():
            ext_ref[0:HALO_P, :] = jnp.zeros((HALO_P, D_BR), f32)

        _norm_rows(xp_ref, put_xn, TM)
        ext_ref[HALO_P:HALO_P + TM, :] = _dot(xn_ref[...], wb_ref[...])
        pos1 = tis * TM + 1 + lax.broadcasted_iota(jnp.int32, (TM, gdim), 0)
        for g, win in enumerate(POOL_WINDOWS):
            lanes = slice(g * gdim, (g + 1) * gdim)
            cur = ext_ref[HALO_P:HALO_P + TM, lanes]
            tot = cur
            for k in range(1, win):
                tot = tot + ext_ref[HALO_P - k:HALO_P - k + TM, lanes]
            cnt = jnp.minimum(pos1, win).astype(f32)
            pm = (tot / cnt - cur).astype(bf16)
            act_ref[:, lanes] = (_dot(pm, pwb_ref[g]) * psc_ref[:, lanes]).astype(bf16)

        @pl.when(tis == TILES_PER_SEQ - 1)
        def _():
            stp_ref[i // TILES_PER_SEQ] = ext_ref[HALO_P + TM - POOL_PAST:HALO_P + TM, :]

        ext_ref[0:HALO_P, :] = ext_ref[TM:TM + HALO_P, :]

    @pl.when(i == N_PT)
    def _():
        if first_layer:
            for t in range(DEC_SEQ):
                xn_ref[t * DEC_BATCH:(t + 1) * DEC_BATCH, :] = (
                    _rms(xs_ref[:, t, :], gain).astype(bf16))
        else:
            _norm_rows(xp_ref, put_xn, M_S)
        p = _dot(xn_ref[0:M_S, :], wb_ref[...])

        def slab(s, lanes):
            if s < POOL_PAST:
                return hist_ref[s, :, lanes]
            r0 = (s - POOL_PAST) * DEC_BATCH
            return p[r0:r0 + DEC_BATCH, lanes]

        for t in range(DEC_SEQ):
            rows = slice(t * DEC_BATCH, (t + 1) * DEC_BATCH)
            for g, win in enumerate(POOL_WINDOWS):
                lanes = slice(g * gdim, (g + 1) * gdim)
                cur = slab(POOL_PAST + t, lanes)
                tot = cur
                for k in range(1, win):
                    tot = tot + slab(POOL_PAST + t - k, lanes)
                cnt = float(min(win, PAST_LEN + t + 1))
                pm = (tot / cnt - cur).astype(bf16)
                act_ref[rows, lanes] = (_dot(pm, pwb_ref[g]) * psc_ref[:, lanes]).astype(bf16)
        for t in range(DEC_SEQ):
            new_ref[:, t * D_BR:(t + 1) * D_BR] = p[t * DEC_BATCH:(t + 1) * DEC_BATCH]


def _stage_d(w_in, pool_w, pool_scale, state, norm_w, l, x=None, x_prompt=None, x_sample=None):
    ngroup = len(POOL_WINDOWS)
    gdim = D_BR // ngroup
    first_layer = x is None
    if first_layer:
        xs = [x_prompt.reshape(M_P, D_MODEL), x_sample]
        x_specs = [
            pl.BlockSpec((TM, D_MODEL), lambda i: (jnp.minimum(i, N_PT - 1), 0)),
            pl.BlockSpec((DEC_BATCH, DEC_SEQ, D_MODEL), lambda i: (0, 0, 0)),
        ]
    else:
        xs = [x]
        x_specs = [pl.BlockSpec((TM, D_MODEL), lambda i: (i, 0))]
    return pl.pallas_call(
        functools.partial(_stage_d_kernel, first_layer),
        grid=(N_TILES,),
        in_specs=x_specs + [
            pl.BlockSpec((None, 1, D_MODEL), lambda i: (l, 0, 0)),
            _once((None, D_MODEL, D_BR), lambda i: (l, 0, 7)),
            pl.BlockSpec((None, ngroup, gdim, gdim), lambda i: (l, 0, 0, 0)),
            pl.BlockSpec((None, 1, D_BR), lambda i: (l, 0, 0)),
            _once((None, POOL_PAST, DEC_BATCH, D_BR), lambda i: (l, 0, 0, 0)),
        ],
        out_specs=[
            pl.BlockSpec((TM, D_BR), lambda i: (i, 0)),
            pl.BlockSpec((BATCH, POOL_PAST, D_BR), lambda i: (0, 0, 0)),
            pl.BlockSpec((DEC_BATCH, DEC_SEQ * D_BR), lambda i: (0, 0)),
            pl.BlockSpec((TM, D_MODEL), lambda i: (i, 0)),
        ],
        out_shape=[
            jax.ShapeDtypeStruct((M_ALL, D_BR), bf16),
            jax.ShapeDtypeStruct((BATCH, POOL_PAST, D_BR), f32),
            jax.ShapeDtypeStruct((DEC_BATCH, DEC_SEQ * D_BR), f32),
            jax.ShapeDtypeStruct((M_ALL, D_MODEL), bf16),
        ],
        scratch_shapes=[
            pltpu.VMEM((D_MODEL, D_BR), bf16),
            pltpu.VMEM((ngroup, gdim, gdim), bf16),
            pltpu.VMEM((HALO_P + TM, D_BR), f32),
        ],
        compiler_params=_cparams(1),
        name="branch_d",
    )(*xs, norm_w.reshape(DEPTH, 1, D_MODEL), w_in, pool_w, pool_scale.reshape(DEPTH, 1, D_BR),
      _time_major(state))


def _stage_g_kernel(xn_ref, a0_ref, a1_ref, a2_ref, a3_ref,
                    g0_ref, g1_ref, g2_ref, g3_ref, wbr_ref,
                    out_ref, wgb_ref, wbb_ref):
    i = pl.program_id(1)

    @pl.when(i == 0)
    def _():
        for b, g_ref in enumerate((g0_ref, g1_ref, g2_ref, g3_ref)):
            _cast_rows(g_ref, wgb_ref, b * TN_G)
        wbb_ref[...] = wbr_ref[...].astype(bf16)

    def compute(rows):
        merged = None
        for b, a_ref in enumerate((a0_ref, a1_ref, a2_ref, a3_ref)):
            gate = _sigmoid(_dot(xn_ref[rows, :], wgb_ref[:, b * TN_G:(b + 1) * TN_G]))
            term = gate * _dot(a_ref[rows, :], wbb_ref[b])
            merged = term if merged is None else merged + term
        out_ref[rows, :] = merged.astype(bf16)

    @pl.when(i > 0)
    def _():
        compute(slice(0, TM))

    @pl.when(i == 0)
    def _():
        compute(slice(0, M_S))


def _stage_g(xn, acts, w_in, w_branch, l):
    gate_blk0 = GATE_COL0 // TN_G
    per_branch = D_MODEL // TN_G
    gspec = lambda b: pl.BlockSpec(
        (None, D_MODEL, TN_G), lambda c, i: (l, 0, gate_blk0 + b * per_branch + c))
    aspec = pl.BlockSpec((TM, D_BR), lambda c, i: (_sample_first(i), 0))
    return pl.pallas_call(
        _stage_g_kernel,
        grid=(D_MODEL // TN_G, N_TILES),
        in_specs=[
            pl.BlockSpec((TM, D_MODEL), lambda c, i: (_sample_first(i), 0)),
            aspec, aspec, aspec, aspec,
            gspec(0), gspec(1), gspec(2), gspec(3),
            pl.BlockSpec((None, N_BRANCH, D_BR, TN_G), lambda c, i: (l, 0, 0, c)),
        ],
        out_specs=pl.BlockSpec((TM, TN_G), lambda c, i: (_sample_first(i), c)),
        out_shape=jax.ShapeDtypeStruct((M_ALL, D_MODEL), bf16),
        scratch_shapes=[
            pltpu.VMEM((D_MODEL, N_BRANCH * TN_G), bf16),
            pltpu.VMEM((N_BRANCH, D_BR, TN_G), bf16),
        ],
        compiler_params=_cparams(2),
        name="gate_merge",
    )(xn, *acts, w_in, w_in, w_in, w_in, w_branch)


def _stage_res_kernel(lhs_ref, w_ref, x_ref, o_ref, wb_ref):
    i = pl.program_id(1)

    @pl.when(i == 0)
    def _():
        _cast_rows(w_ref, wb_ref)

    @pl.when(i > 0)
    def _():
        o_ref[...] = x_ref[...] + _dot(lhs_ref[...], wb_ref[...])

    @pl.when(i == 0)
    def _():
        o_ref[0:M_S, :] = x_ref[0:M_S, :] + _dot(lhs_ref[0:M_S, :], wb_ref[...])


def _stage_res(lhs, w, x, l, tn, name, single_buffer_w):
    k = lhs.shape[1]
    wshape = (None, k, tn)
    wmap = lambda c, i: (l, 0, c)
    wspec = _once(wshape, wmap) if single_buffer_w else pl.BlockSpec(wshape, wmap)
    return pl.pallas_call(
        _stage_res_kernel,
        grid=(D_MODEL // tn, N_TILES),
        in_specs=[
            pl.BlockSpec((TM, k), lambda c, i: (_sample_first(i), 0)),
            wspec,
            pl.BlockSpec((TM, tn), lambda c, i: (_sample_first(i), c)),
        ],
        out_specs=pl.BlockSpec((TM, tn), lambda c, i: (_sample_first(i), c)),
        out_shape=jax.ShapeDtypeStruct((M_ALL, D_MODEL), f32),
        scratch_shapes=[pltpu.VMEM((k, tn), bf16)],
        compiler_params=_cparams(2),
        name=name,
    )(lhs, w, x)


def _stage_o_kernel(first_layer, *refs):
    if first_layer:
        m_ref, w_ref, xp_ref, xs_ref, g_ref, x_ref, xn_ref, wb_ref = refs
    else:
        m_ref, w_ref, xin_ref, g_ref, x_ref, xn_ref, wb_ref = refs
    i = pl.program_id(0)

    @pl.when(i == 0)
    def _():
        _cast_rows(w_ref, wb_ref)

    g = g_ref[...]

    def finish(r0, nrows, resid_rows, proj):
        for c in range(nrows // LN_R):
            rows = slice(r0 + c * LN_R, r0 + (c + 1) * LN_R)
            x = resid_rows(slice(c * LN_R, (c + 1) * LN_R)) + proj[rows]
            x_ref[rows, :] = x
            xn_ref[rows, :] = _rms(x, g).astype(bf16)

    if first_layer:
        @pl.when(i < M_P // TM_O)
        def _():
            finish(0, TM_O, lambda r: xp_ref[r, :], _dot(m_ref[...], wb_ref[...]))

        @pl.when(i >= M_P // TM_O)
        def _():
            proj = _dot(m_ref[...], wb_ref[...])
            for t in range(DEC_SEQ):
                finish(t * DEC_BATCH, DEC_BATCH,
                       lambda r, t=t: xs_ref[r, t, :], proj)
    else:
        finish(0, TM_O, lambda r: xin_ref[r, :], _dot(m_ref[...], wb_ref[...]))


def _stage_o(merged, w_o, norm_w, l, x=None, x_prompt=None, x_sample=None):
    first_layer = x is None
    n_steps = M_ALL // TM_O
    row_blk = pl.BlockSpec((TM_O, D_MODEL), lambda i: (i, 0))
    if first_layer:
        n_p = M_P // TM_O
        resid = [x_prompt.reshape(M_P, D_MODEL), x_sample]
        resid_specs = [
            pl.BlockSpec((TM_O, D_MODEL), lambda i: (jnp.minimum(i, n_p - 1), 0)),
            pl.BlockSpec((DEC_BATCH, DEC_SEQ, D_MODEL), lambda i: (0, 0, 0)),
        ]
    else:
        resid = [x]
        resid_specs = [row_blk]
    return pl.pallas_call(
        functools.partial(_stage_o_kernel, first_layer),
        grid=(n_steps,),
        in_specs=[row_blk, _once((None, D_MODEL, D_MODEL), lambda i: (l, 0, 0))]
        + resid_specs + [pl.BlockSpec((None, 1, D_MODEL), lambda i: (l, 0, 0))],
        out_specs=[row_blk, row_blk],
        out_shape=[
            jax.ShapeDtypeStruct((M_ALL, D_MODEL), f32),
            jax.ShapeDtypeStruct((M_ALL, D_MODEL), bf16),
        ],
        scratch_shapes=[pltpu.VMEM((D_MODEL, D_MODEL), bf16)],
        compiler_params=_cparams(1),
        name="out_proj_norm",
    )(merged, w_o, *resid, norm_w.reshape(DEPTH, 1, D_MODEL))


def _stage_u_kernel(xn_ref, wg_ref, wv_ref, dwg_ref, dwv_ref,
                    hgh_ref, hvh_ref,
                    a_ref, stp_ref, sts_ref, wb_ref, ext_ref):
    i = pl.program_id(1)
    nblk = TK_U // U_BLK
    npast = FFN_K - 1

    def _blk(k):
        return slice(k * U_BLK, (k + 1) * U_BLK)

    @pl.when(i == 0)
    def _():
        moves = []
        for k in range(nblk):
            moves.append((wg_ref, k * U_BLK, 2 * k * U_BLK, U_BLK))
            moves.append((wv_ref, k * U_BLK, (2 * k + 1) * U_BLK, U_BLK))
        _cast_blocks(moves, wb_ref)
        ext_ref[0:HALO_S, :] = jnp.zeros((HALO_S, 2 * TK_U), f32)

    def taps(dw_ref, k):
        return dw_ref[0:1, _blk(k)], dw_ref[1:2, _blk(k)], dw_ref[2:3, _blk(k)]

    @pl.when(i > 0)
    def _():
        first = lax.rem(i - 1, TILES_PER_SEQ) == 0
        ext_ref[0:HALO_S, :] = jnp.where(first, 0.0, ext_ref[0:HALO_S, :])
        h = _dot(xn_ref[...], wb_ref[...])
        ext_ref[HALO_S:HALO_S + TM, :] = h
        for k in range(nblk):
            for c in range(TM // U_R):
                r0 = HALO_S + c * U_R

                def conv(dw_ref, lanes):
                    w0, w1, w2 = taps(dw_ref, k)
                    return (w0 * ext_ref[r0 - 2:r0 - 2 + U_R, lanes]
                            + w1 * ext_ref[r0 - 1:r0 - 1 + U_R, lanes]
                            + w2 * h[c * U_R:(c + 1) * U_R, lanes])

                hg = conv(dwg_ref, _blk(2 * k))
                hv = conv(dwv_ref, _blk(2 * k + 1))
                a_ref[c * U_R:(c + 1) * U_R, _blk(k)] = (_silu(hg) * hv).astype(bf16)
        b = (i - 1) // TILES_PER_SEQ
        last = slice(HALO_S + TM - npast, HALO_S + TM)
        for k in range(nblk):
            stp_ref[b, 0, :, _blk(k)] = ext_ref[last, _blk(2 * k)]
            stp_ref[b, 1, :, _blk(k)] = ext_ref[last, _blk(2 * k + 1)]
        ext_ref[0:HALO_S, :] = ext_ref[TM:TM + HALO_S, :]

    @pl.when(i == 0)
    def _():
        h = _dot(xn_ref[0:M_S, :], wb_ref[...])

        def conv_slabs(dw_ref, hist_ref, k, lanes):
            w0, w1, w2 = taps(dw_ref, k)
            xp = [hist_ref[:, r, _blk(k)] for r in range(npast)]
            xp += [h[t * DEC_BATCH:(t + 1) * DEC_BATCH, lanes] for t in range(DEC_SEQ)]
            return [w0 * xp[t] + w1 * xp[t + 1] + w2 * xp[t + 2] for t in range(DEC_SEQ)]

        for k in range(nblk):
            hg = conv_slabs(dwg_ref, hgh_ref, k, _blk(2 * k))
            hv = conv_slabs(dwv_ref, hvh_ref, k, _blk(2 * k + 1))
            for t in range(DEC_SEQ):
                a_ref[t * DEC_BATCH:(t + 1) * DEC_BATCH, _blk(k)] = (
                    _silu(hg[t]) * hv[t]).astype(bf16)
            for r in range(npast):
                t = DEC_SEQ - npast + r
                rows = slice(t * DEC_BATCH, (t + 1) * DEC_BATCH)
                sts_ref[r, 0, :, _blk(k)] = h[rows, _blk(2 * k)]
                sts_ref[r, 1, :, _blk(k)] = h[rows, _blk(2 * k + 1)]


def _stage_u(xn, ffn_up, ffn_dw, state, l):
    nk = D_FF // TK_U
    npast = FFN_K - 1
    wspec = lambda half: pl.BlockSpec(
        (None, D_MODEL, TK_U), lambda k, i: (l, 0, half * nk + k))
    dspec = lambda half: pl.BlockSpec(
        (None, FFN_K, TK_U), lambda k, i: (l, 0, half * nk + k))
    hspec = lambda half: pl.BlockSpec(
        (None, DEC_BATCH, npast, TK_U), lambda k, i: (l, 0, 0, half * nk + k))
    a, stp, sts = pl.pallas_call(
        _stage_u_kernel,
        grid=(nk, N_TILES),
        in_specs=[
            pl.BlockSpec((TM, D_MODEL), lambda k, i: (_sample_first(i), 0)),
            wspec(0), wspec(1), dspec(0), dspec(1),
            hspec(0), hspec(1),
        ],
        out_specs=[
            pl.BlockSpec((TM, TK_U), lambda k, i: (_sample_first(i), k)),
            pl.BlockSpec((BATCH, 2, npast, TK_U), lambda k, i: (0, 0, 0, k)),
            pl.BlockSpec((npast, 2, DEC_BATCH, TK_U), lambda k, i: (0, 0, 0, k)),
        ],
        out_shape=[
            jax.ShapeDtypeStruct((M_ALL, D_FF), bf16),
            jax.ShapeDtypeStruct((BATCH, 2, npast, D_FF), f32),
            jax.ShapeDtypeStruct((npast, 2, DEC_BATCH, D_FF), f32),
        ],
        scratch_shapes=[
            pltpu.VMEM((D_MODEL, 2 * TK_U), bf16),
            pltpu.VMEM((HALO_S + TM, 2 * TK_U), f32),
        ],
        compiler_params=_cparams(2),
        name="ffn_up",
    )(xn, ffn_up, ffn_up, ffn_dw, ffn_dw, state, state)
    stp = jnp.transpose(stp, (0, 2, 1, 3)).reshape(BATCH, npast, 2 * D_FF)
    return a, stp, sts


def _ffn_state_kernel(*refs):
    in_refs, out_ref = refs[:-1], refs[-1]
    l = pl.program_id(0)
    for d, in_ref in enumerate(in_refs):
        @pl.when(l == d)
        def _(in_ref=in_ref):
            for r in range(FFN_K - 1):
                out_ref[:, r, :] = in_ref[r]


def _ffn_state(per_layer):
    npast = FFN_K - 1
    ncb = D_FF // FFN_ST_W
    spec = pl.BlockSpec((npast, None, DEC_BATCH, FFN_ST_W), lambda l, h, c: (0, h, 0, c))
    return pl.pallas_call(
        _ffn_state_kernel,
        grid=(DEPTH, 2, ncb),
        in_specs=[spec] * DEPTH,
        out_specs=pl.BlockSpec((None, DEC_BATCH, npast, FFN_ST_W),
                               lambda l, h, c: (l, 0, 0, h * ncb + c)),
        out_shape=jax.ShapeDtypeStruct((DEPTH, DEC_BATCH, npast, 2 * D_FF), f32),
        compiler_params=_cparams(3),
        name="ffn_state",
    )(*per_layer)


def kernel(x_prompt, x_sample, state_conf_conv, state_sconv, state_pool, state_ffn_conv,
           norm_mix, w_in, conf_dw, conf_ln_g, conf_ln_b, gmlp_ln_g, gmlp_ln_b, gmlp_ws,
           gmlp_b, sconv_dw, pool_w, pool_scale, w_branch, w_o, norm_ffn, ffn_up, ffn_dw,
           ffn_down, norm_final):
    conf_p, conf_new, sconv_p, sconv_s, pool_p, pool_new, ffn_p, ffn_new, v_s = (
        [] for _ in range(9))
    x = None
    for l in range(DEPTH):
        if l == 0:
            act_d, st_p, new, xn = _stage_d(w_in, pool_w, pool_scale, state_pool, norm_mix, l,
                                            x_prompt=x_prompt, x_sample=x_sample)
        else:
            act_d, st_p, new, xn = _stage_d(w_in, pool_w, pool_scale, state_pool, norm_mix, l, x=x)
        pool_p.append(st_p)
        pool_new.append(new)
        act_a, st_p, new = _stage_a(xn, w_in, conf_dw, conf_ln_g, conf_ln_b, state_conf_conv, l)
        conf_p.append(st_p)
        conf_new.append(new)
        act_b, v = _stage_b(xn, w_in, gmlp_ln_g, gmlp_ln_b, gmlp_ws, gmlp_b, l)
        v_s.append(v)
        act_c, st_p, st_s = _stage_c(xn, w_in, sconv_dw, state_sconv, l)
        sconv_p.append(st_p)
        sconv_s.append(st_s)
        merged = _stage_g(xn, (act_a, act_b, act_c, act_d), w_in, w_branch, l)
        if l == 0:
            x, xn = _stage_o(merged, w_o, norm_ffn, l, x_prompt=x_prompt, x_sample=x_sample)
        else:
            x, xn = _stage_o(merged, w_o, norm_ffn, l, x=x)
        a, st_p, new = _stage_u(xn, ffn_up, ffn_dw, state_ffn_conv, l)
        ffn_p.append(st_p)
        ffn_new.append(new)
        x = _stage_res(a, ffn_down, x, l, TN_D, "ffn_down", False)
    y_prompt, y_sample = _final_norm(x, norm_final)
    conf_s = _roll_state(state_conf_conv, conf_new)
    pool_s = _roll_state(state_pool, pool_new)
    ffn_s = _ffn_state(ffn_new)
    st = jnp.stack
    return (y_prompt, y_sample, st(conf_p), conf_s, st(sconv_p), st(sconv_s),
            st(pool_p), pool_s, st(ffn_p), ffn_s, st(v_s))
```

```python
import functools

import jax
import jax.numpy as jnp
from jax import lax
from jax.experimental import pallas as pl
from jax.experimental.pallas import tpu as pltpu

D_MODEL = 2048
BATCH = 4
SEQ = 2048
DEPTH = 2
DEC_BATCH = 128
DEC_SEQ = 4
PAST_LEN = 16384
D_BR = 512
N_BRANCH = 4
CONF_K = 31
GMLP_HEADS = 4
HEAD_DIM = D_BR // GMLP_HEADS
CHUNK = 128
SCONV_K = 3
POOL_WINDOWS = (2, 4, 8, 16)
POOL_PAST = 15
D_FF = 5632
FFN_K = 3
EPS = 1e-6
GATE_COL0 = 8 * D_BR

TM = 1024
M_P = BATCH * SEQ
M_S = DEC_BATCH * DEC_SEQ
M_ALL = M_P + M_S
N_PT = M_P // TM
TILES_PER_SEQ = SEQ // TM
N_TILES = N_PT + 1

LANES = 128
HALO_A = 32
HALO_P = 16
HALO_S = 8
CONV_R = 128
EP_R = 128
LN_R = 64
ROLL_B = 128
FFN_ST_W = 2816
U_R = 1024
U_BLK = 512

TN_G = 256
TM_O = 512
O_SLOTS = 3
TK_U = 512
TN_D = 512

VMEM_LIMIT = 60000 * 1024

f32 = jnp.float32
bf16 = jnp.bfloat16


def _cparams(n_axes):
    return pltpu.CompilerParams(
        dimension_semantics=("arbitrary",) * n_axes, vmem_limit_bytes=VMEM_LIMIT)


def _once(block_shape, index_map):
    return pl.BlockSpec(block_shape, index_map, pipeline_mode=pl.Buffered(1))


def _rms(x, g):
    return x * lax.rsqrt(jnp.mean(x * x, axis=-1, keepdims=True) + EPS) * g


def _layernorm(x, g, b):
    mu = jnp.mean(x, axis=-1, keepdims=True)
    d = x - mu
    var = jnp.mean(d * d, axis=-1, keepdims=True)
    return d * lax.rsqrt(var + EPS) * g + b


def _sigmoid(x):
    return 1.0 / (1.0 + jnp.exp(-x))


def _silu(x):
    h = 0.5 * x
    return h + h * jnp.tanh(h)


def _gelu_tanh(x):
    c = 0.7978845608028654
    return 0.5 * x * (1.0 + jnp.tanh(c * (x + 0.044715 * (x * x * x))))


def _cast_blocks(moves, wb_ref, rows_per_step=256):
    k = wb_ref.shape[0]

    def body(c, carry):
        r = pl.multiple_of(c * rows_per_step, rows_per_step)
        for w_ref, src, dst, n in moves:
            wb_ref[pl.ds(r, rows_per_step), dst:dst + n] = (
                w_ref[pl.ds(r, rows_per_step), src:src + n].astype(bf16))
        return carry

    lax.fori_loop(0, k // rows_per_step, body, 0)


def _cast_rows(w_ref, wb_ref, col0=0):
    _cast_blocks([(w_ref, 0, col0, w_ref.shape[1])], wb_ref)


def _dot(a, b):
    return jnp.dot(a, b, preferred_element_type=f32)


def _blk(k):
    return slice(k * LANES, (k + 1) * LANES)


def _sample_first(i):
    return lax.rem(i + N_PT, N_TILES)


def _norm_rows(src_ref, put, rows, chunk=256):
    def body(c, carry):
        r = pl.multiple_of(c * chunk, chunk)
        put(r, chunk, src_ref[pl.ds(r, chunk), :])
        return carry

    lax.fori_loop(0, rows // chunk, body, 0)


def _final_norm_kernel(x_ref, g_ref, yp_ref, ys_ref):
    i = pl.program_id(0)
    g = g_ref[...]

    @pl.when(i < N_PT)
    def _():
        def put(r, n, x):
            yp_ref[pl.ds(r, n), :] = _rms(x, g)
        _norm_rows(x_ref, put, TM)

    @pl.when(i == N_PT)
    def _():
        for t in range(DEC_SEQ):
            x = x_ref[t * DEC_BATCH:(t + 1) * DEC_BATCH, :]
            ys_ref[:, t, :] = _rms(x, g)


def _final_norm(x, norm_w):
    yp, ys = pl.pallas_call(
        _final_norm_kernel,
        grid=(N_TILES,),
        in_specs=[
            pl.BlockSpec((TM, D_MODEL), lambda i: (i, 0)),
            pl.BlockSpec((1, D_MODEL), lambda i: (0, 0)),
        ],
        out_specs=[
            pl.BlockSpec((TM, D_MODEL), lambda i: (jnp.minimum(i, N_PT - 1), 0)),
            pl.BlockSpec((DEC_BATCH, DEC_SEQ, D_MODEL), lambda i: (0, 0, 0)),
        ],
        out_shape=[
            jax.ShapeDtypeStruct((M_P, D_MODEL), f32),
            jax.ShapeDtypeStruct((DEC_BATCH, DEC_SEQ, D_MODEL), f32),
        ],
        compiler_params=_cparams(1),
        name="final_norm",
    )(x, norm_w.reshape(1, D_MODEL))
    return yp.reshape(BATCH, SEQ, D_MODEL), ys


def _time_major(state):
    return jnp.transpose(state, (0, 2, 1, 3))


def _roll_state_kernel(old_ref, *refs):
    new_refs, out_ref = refs[:-1], refs[-1]
    l = pl.program_id(0)
    npast, width = old_ref.shape[0], old_ref.shape[2]
    keep = npast - DEC_SEQ
    for s in range(keep):
        out_ref[s] = old_ref[DEC_SEQ + s]
    for d, new_ref in enumerate(new_refs):
        @pl.when(l == d)
        def _(new_ref=new_ref):
            for t in range(DEC_SEQ):
                out_ref[keep + t] = new_ref[:, t * width:(t + 1) * width]


def _roll_state(old, new_rows):
    old_t = _time_major(old)
    npast, width = old_t.shape[1], old_t.shape[3]
    blk = pl.BlockSpec((None, npast, ROLL_B, width), lambda l, j: (l, 0, j, 0))
    nspec = pl.BlockSpec((ROLL_B, DEC_SEQ * width), lambda l, j: (j, 0))
    out = pl.pallas_call(
        _roll_state_kernel,
        grid=(DEPTH, DEC_BATCH // ROLL_B),
        in_specs=[blk] + [nspec] * DEPTH,
        out_specs=blk,
        out_shape=jax.ShapeDtypeStruct(old_t.shape, f32),
        compiler_params=_cparams(2),
        name="roll_state",
    )(old_t, *new_rows)
    return _time_major(out)


def _conv31_chunk(ext_ref, base, dw_ref, lanes):
    y = None
    for r in range(8):
        rows = CONV_R if r == 0 else CONV_R + 8
        z = None
        for a in range(5):
            j = 8 * a + r - 2
            if 0 <= j < CONF_K:
                term = dw_ref[pl.ds(j, 1), lanes] * ext_ref[pl.ds(base + 8 * a, rows), lanes]
                z = term if z is None else z + term
        zr = z[r:r + CONV_R]
        y = zr if y is None else y + zr
    return y


def _stage_a_kernel(xn_ref, w_ref, dw_ref, lng_ref, lnb_ref, hist_ref,
                    act_ref, stp_ref, new_ref, wb_ref, ext_ref, y_ref):
    i = pl.program_id(0)
    nblk = D_BR // LANES
    npast = CONF_K - 1

    @pl.when(i == 0)
    def _():
        moves = []
        for k in range(nblk):
            moves.append((w_ref, k * LANES, 2 * k * LANES, LANES))
            moves.append((w_ref, D_BR + k * LANES, (2 * k + 1) * LANES, LANES))
        _cast_blocks(moves, wb_ref)
        ext_ref[0:HALO_A, :] = jnp.zeros((HALO_A, D_BR), f32)

    lng = lng_ref[...]
    lnb = lnb_ref[...]

    def glu_block(p, k, rows):
        return p[rows, _blk(2 * k)] * _sigmoid(p[rows, _blk(2 * k + 1)])

    @pl.when(i < N_PT)
    def _():
        first = lax.rem(i, TILES_PER_SEQ) == 0
        ext_ref[0:HALO_A, :] = jnp.where(first, 0.0, ext_ref[0:HALO_A, :])
        p = _dot(xn_ref[...], wb_ref[...])
        for k in range(nblk):
            for c in range(TM // CONV_R):
                rows = slice(c * CONV_R, (c + 1) * CONV_R)
                ext_ref[HALO_A + c * CONV_R:HALO_A + (c + 1) * CONV_R, _blk(k)] = (
                    glu_block(p, k, rows))
            for c in range(TM // CONV_R):
                y_ref[c * CONV_R:(c + 1) * CONV_R, _blk(k)] = (
                    _conv31_chunk(ext_ref, c * CONV_R, dw_ref, _blk(k)))
        for c in range(TM // LN_R):
            rows = slice(c * LN_R, (c + 1) * LN_R)
            act_ref[rows, :] = _silu(_layernorm(y_ref[rows, :], lng, lnb)).astype(bf16)
        stp_ref[i // TILES_PER_SEQ] = ext_ref[HALO_A + TM - npast:HALO_A + TM, :]
        ext_ref[0:HALO_A, :] = ext_ref[TM:TM + HALO_A, :]

    @pl.when(i == N_PT)
    def _():
        p = _dot(xn_ref[0:M_S, :], wb_ref[...])
        glu = jnp.concatenate([glu_block(p, k, slice(0, M_S)) for k in range(nblk)], axis=1)
        rb = 16
        for q in range(DEC_BATCH // rb):
            y = [None] * DEC_SEQ
            for s in range(npast + DEC_SEQ):
                if s < npast:
                    slab = hist_ref[s, q * rb:(q + 1) * rb, :]
                else:
                    r0 = (s - npast) * DEC_BATCH + q * rb
                    slab = glu[r0:r0 + rb]
                for t in range(DEC_SEQ):
                    j = s - t
                    if 0 <= j < CONF_K:
                        term = dw_ref[pl.ds(j, 1), :] * slab
                        y[t] = term if y[t] is None else y[t] + term
            for t in range(DEC_SEQ):
                r0 = t * DEC_BATCH + q * rb
                act_ref[r0:r0 + rb, :] = _silu(_layernorm(y[t], lng, lnb)).astype(bf16)
        for t in range(DEC_SEQ):
            new_ref[:, t * D_BR:(t + 1) * D_BR] = glu[t * DEC_BATCH:(t + 1) * DEC_BATCH]


def _stage_a(xn, w_in, conf_dw, ln_g, ln_b, state, l):
    npast = CONF_K - 1
    return pl.pallas_call(
        _stage_a_kernel,
        grid=(N_TILES,),
        in_specs=[
            pl.BlockSpec((TM, D_MODEL), lambda i: (i, 0)),
            _once((None, D_MODEL, 2 * D_BR), lambda i: (l, 0, 0)),
            pl.BlockSpec((None, CONF_K, D_BR), lambda i: (l, 0, 0)),
            pl.BlockSpec((None, 1, D_BR), lambda i: (l, 0, 0)),
            pl.BlockSpec((None, 1, D_BR), lambda i: (l, 0, 0)),
            _once((None, npast, DEC_BATCH, D_BR), lambda i: (l, 0, 0, 0)),
        ],
        out_specs=[
            pl.BlockSpec((TM, D_BR), lambda i: (i, 0)),
            pl.BlockSpec((BATCH, npast, D_BR), lambda i: (0, 0, 0)),
            pl.BlockSpec((DEC_BATCH, DEC_SEQ * D_BR), lambda i: (0, 0)),
        ],
        out_shape=[
            jax.ShapeDtypeStruct((M_ALL, D_BR), bf16),
            jax.ShapeDtypeStruct((BATCH, npast, D_BR), f32),
            jax.ShapeDtypeStruct((DEC_BATCH, DEC_SEQ * D_BR), f32),
        ],
        scratch_shapes=[
            pltpu.VMEM((D_MODEL, 2 * D_BR), bf16),
            pltpu.VMEM((HALO_A + TM, D_BR), f32),
            pltpu.VMEM((TM, D_BR), f32),
        ],
        compiler_params=_cparams(1),
        name="branch_a",
    )(xn, w_in, conf_dw, ln_g.reshape(DEPTH, 1, D_BR), ln_b.reshape(DEPTH, 1, D_BR),
      _time_major(state))


def _stage_b_kernel(xn_ref, w_ref, lng_ref, lnb_ref, ws_ref, bt_ref, wss_ref, bss_ref,
                    act_ref, v_ref, wb_ref, tril_ref, bias_ref):
    i = pl.program_id(0)

    @pl.when(i == 0)
    def _():
        _cast_blocks([(w_ref, D_BR, 0, D_BR), (w_ref, 0, D_BR, D_BR)], wb_ref)
        row = lax.broadcasted_iota(jnp.int32, (CHUNK, CHUNK), 0)
        col = lax.broadcasted_iota(jnp.int32, (CHUNK, CHUNK), 1)
        for h in range(GMLP_HEADS):
            tril_ref[h] = jnp.where(row >= col, ws_ref[h], 0.0).astype(bf16)
            bias_ref[h] = jnp.broadcast_to(bt_ref[:, h:h + 1], (CHUNK, HEAD_DIM))

    lng = lng_ref[...]
    lnb = lnb_ref[...]

    @pl.when(i < N_PT)
    def _():
        p = _dot(xn_ref[...], wb_ref[...])
        for c in range(TM // CHUNK):
            rows = slice(c * CHUNK, (c + 1) * CHUNK)
            u = _gelu_tanh(p[rows, D_BR:])
            v = _layernorm(_gelu_tanh(p[rows, :D_BR]), lng, lnb).astype(bf16)
            for h in range(GMLP_HEADS):
                lanes = slice(h * HEAD_DIM, (h + 1) * HEAD_DIM)
                mixed = _dot(tril_ref[h], v[:, lanes]) + bias_ref[h]
                act_ref[rows, lanes] = (u[:, lanes] * mixed).astype(bf16)

    @pl.when(i == N_PT)
    def _():
        p = _dot(xn_ref[0:M_S, :], wb_ref[...])
        u = _gelu_tanh(p[:, D_BR:])
        v = _layernorm(_gelu_tanh(p[:, :D_BR]), lng, lnb)
        for t in range(DEC_SEQ):
            v_ref[:, t * D_BR:(t + 1) * D_BR] = v[t * DEC_BATCH:(t + 1) * DEC_BATCH]
        for t in range(DEC_SEQ):
            rows = slice(t * DEC_BATCH, (t + 1) * DEC_BATCH)
            for h in range(GMLP_HEADS):
                lanes = slice(h * HEAD_DIM, (h + 1) * HEAD_DIM)
                mixed = jnp.full((DEC_BATCH, HEAD_DIM), bss_ref[h * DEC_SEQ + t], f32)
                for s in range(t + 1):
                    coef = wss_ref[(h * DEC_SEQ + t) * DEC_SEQ + s]
                    mixed = mixed + coef * v[s * DEC_BATCH:(s + 1) * DEC_BATCH, lanes]
                act_ref[rows, lanes] = (u[rows, lanes] * mixed).astype(bf16)


def _stage_b(xn, w_in, ln_g, ln_b, gmlp_ws, gmlp_b, l):
    bias_t = jnp.swapaxes(gmlp_b, 1, 2)
    ws_small = gmlp_ws[l, :, :DEC_SEQ, :DEC_SEQ].reshape(-1)
    b_small = gmlp_b[l, :, :DEC_SEQ].reshape(-1)
    act, v = pl.pallas_call(
        _stage_b_kernel,
        grid=(N_TILES,),
        in_specs=[
            pl.BlockSpec((TM, D_MODEL), lambda i: (i, 0)),
            _once((None, D_MODEL, 2 * D_BR), lambda i: (l, 0, 1)),
            pl.BlockSpec((None, 1, D_BR), lambda i: (l, 0, 0)),
            pl.BlockSpec((None, 1, D_BR), lambda i: (l, 0, 0)),
            pl.BlockSpec((None, GMLP_HEADS, CHUNK, CHUNK), lambda i: (l, 0, 0, 0)),
            pl.BlockSpec((None, CHUNK, GMLP_HEADS), lambda i: (l, 0, 0)),
            pl.BlockSpec(memory_space=pltpu.SMEM),
            pl.BlockSpec(memory_space=pltpu.SMEM),
        ],
        out_specs=[
            pl.BlockSpec((TM, D_BR), lambda i: (i, 0)),
            pl.BlockSpec((DEC_BATCH, DEC_SEQ * D_BR), lambda i: (0, 0)),
        ],
        out_shape=[
            jax.ShapeDtypeStruct((M_ALL, D_BR), bf16),
            jax.ShapeDtypeStruct((DEC_BATCH, DEC_SEQ * D_BR), f32),
        ],
        scratch_shapes=[
            pltpu.VMEM((D_MODEL, 2 * D_BR), bf16),
            pltpu.VMEM((GMLP_HEADS, CHUNK, CHUNK), bf16),
            pltpu.VMEM((GMLP_HEADS, CHUNK, HEAD_DIM), f32),
        ],
        compiler_params=_cparams(1),
        name="branch_b",
    )(xn, w_in, ln_g.reshape(DEPTH, 1, D_BR), ln_b.reshape(DEPTH, 1, D_BR),
      gmlp_ws, bias_t, ws_small, b_small)
    return act, v.reshape(DEC_BATCH, DEC_SEQ, D_BR)


def _stage_c_kernel(xn_ref, w0_ref, w1_ref, w2_ref, dw_ref, hist_ref,
                    act_ref, stp_ref, sts_ref, wb_ref, ext_ref):
    i = pl.program_id(0)
    nblk = D_BR // LANES

    @pl.when(i == 0)
    def _():
        moves = []
        for k in range(nblk):
            for part, w_ref in enumerate((w0_ref, w1_ref, w2_ref)):
                moves.append((w_ref, k * LANES, (3 * k + part) * LANES, LANES))
        _cast_blocks(moves, wb_ref)
        ext_ref[0:HALO_S, :] = jnp.zeros((HALO_S, D_BR), f32)

    def taps(k):
        return dw_ref[0:1, _blk(k)], dw_ref[1:2, _blk(k)], dw_ref[2:3, _blk(k)]

    @pl.when(i < N_PT)
    def _():
        first = lax.rem(i, TILES_PER_SEQ) == 0
        ext_ref[0:HALO_S, :] = jnp.where(first, 0.0, ext_ref[0:HALO_S, :])
        p = _dot(xn_ref[...], wb_ref[...])
        for k in range(nblk):
            w0, w1, w2 = taps(k)
            for c in range(TM // EP_R):
                rows = slice(c * EP_R, (c + 1) * EP_R)
                r0 = HALO_S + c * EP_R
                ext_ref[r0:r0 + EP_R, _blk(k)] = p[rows, _blk(3 * k + 1)] * p[rows, _blk(3 * k + 2)]
                z = (w0 * ext_ref[r0 - 2:r0 - 2 + EP_R, _blk(k)]
                     + w1 * ext_ref[r0 - 1:r0 - 1 + EP_R, _blk(k)]
                     + w2 * ext_ref[r0:r0 + EP_R, _blk(k)])
                act_ref[rows, _blk(k)] = (p[rows, _blk(3 * k)] * z).astype(bf16)
        stp_ref[i // TILES_PER_SEQ] = ext_ref[HALO_S + TM - (SCONV_K - 1):HALO_S + TM, :]
        ext_ref[0:HALO_S, :] = ext_ref[TM:TM + HALO_S, :]

    @pl.when(i == N_PT)
    def _():
        p = _dot(xn_ref[0:M_S, :], wb_ref[...])
        for k in range(nblk):
            w0, w1, w2 = taps(k)
            s = p[:, _blk(3 * k + 1)] * p[:, _blk(3 * k + 2)]
            xp = [hist_ref[:, _blk(k)], hist_ref[:, D_BR + k * LANES:D_BR + (k + 1) * LANES]]
            xp += [s[t * DEC_BATCH:(t + 1) * DEC_BATCH] for t in range(DEC_SEQ)]
            for t in range(DEC_SEQ):
                z = w0 * xp[t] + w1 * xp[t + 1] + w2 * xp[t + 2]
                rows = slice(t * DEC_BATCH, (t + 1) * DEC_BATCH)
                act_ref[rows, _blk(k)] = (p[rows, _blk(3 * k)] * z).astype(bf16)
            sts_ref[:, _blk(k)] = xp[DEC_SEQ]
            sts_ref[:, D_BR + k * LANES:D_BR + (k + 1) * LANES] = xp[DEC_SEQ + 1]


def _stage_c(xn, w_in, sconv_dw, state, l):
    npast = SCONV_K - 1
    hist = state.reshape(DEPTH, DEC_BATCH, npast * D_BR)
    wspec = lambda cb: _once((None, D_MODEL, D_BR), lambda i: (l, 0, cb))
    act, stp, sts = pl.pallas_call(
        _stage_c_kernel,
        grid=(N_TILES,),
        in_specs=[
            pl.BlockSpec((TM, D_MODEL), lambda i: (i, 0)),
            wspec(4), wspec(5), wspec(6),
            pl.BlockSpec((None, SCONV_K, D_BR), lambda i: (l, 0, 0)),
            pl.BlockSpec((None, DEC_BATCH, npast * D_BR), lambda i: (l, 0, 0)),
        ],
        out_specs=[
            pl.BlockSpec((TM, D_BR), lambda i: (i, 0)),
            pl.BlockSpec((BATCH, npast, D_BR), lambda i: (0, 0, 0)),
            pl.BlockSpec((DEC_BATCH, npast * D_BR), lambda i: (0, 0)),
        ],
        out_shape=[
            jax.ShapeDtypeStruct((M_ALL, D_BR), bf16),
            jax.ShapeDtypeStruct((BATCH, npast, D_BR), f32),
            jax.ShapeDtypeStruct((DEC_BATCH, npast * D_BR), f32),
        ],
        scratch_shapes=[
            pltpu.VMEM((D_MODEL, 3 * D_BR), bf16),
            pltpu.VMEM((HALO_S + TM, D_BR), f32),
        ],
        compiler_params=_cparams(1),
        name="branch_c",
    )(xn, w_in, w_in, w_in, sconv_dw, hist)
    return act, stp, sts.reshape(DEC_BATCH, npast, D_BR)


def _stage_d_kernel(first_layer, *refs):
    if first_layer:
        (xp_ref, xs_ref, g_ref, w_ref, pw_ref, psc_ref, hist_ref,
         act_ref, stp_ref, new_ref, xn_ref, wb_ref, pwb_ref, ext_ref) = refs
    else:
        (xp_ref, g_ref, w_ref, pw_ref, psc_ref, hist_ref,
         act_ref, stp_ref, new_ref, xn_ref, wb_ref, pwb_ref, ext_ref) = refs
    i = pl.program_id(0)
    gdim = D_BR // len(POOL_WINDOWS)
    gain = g_ref[...]

    def put_xn(r, n, x):
        xn_ref[pl.ds(r, n), :] = _rms(x, gain).astype(bf16)

    @pl.when(i == 0)
    def _():
        _cast_rows(w_ref, wb_ref)
        pwb_ref[...] = pw_ref[...].astype(bf16)

    @pl.when(i < N_PT)
    def _():
        tis = i % TILES_PER_SEQ

        @pl.when(tis == 0)
        def _():
            ext_ref[0:HALO_P, :] = jnp.zeros((HALO_P, D_BR), f32)

        _norm_rows(xp_ref, put_xn, TM)
        ext_ref[HALO_P:HALO_P + TM, :] = _dot(xn_ref[...], wb_ref[...])
        pos1 = tis * TM + 1 + lax.broadcasted_iota(jnp.int32, (TM, gdim), 0)
        for g, win in enumerate(POOL_WINDOWS):
            lanes = slice(g * gdim, (g + 1) * gdim)
            cur = ext_ref[HALO_P:HALO_P + TM, lanes]
            tot = cur
            for k in range(1, win):
                tot = tot + ext_ref[HALO_P - k:HALO_P - k + TM, lanes]
            cnt = jnp.minimum(pos1, win).astype(f32)
            pm = (tot / cnt - cur).astype(bf16)
            act_ref[:, lanes] = (_dot(pm, pwb_ref[g]) * psc_ref[:, lanes]).astype(bf16)

        @pl.when(tis == TILES_PER_SEQ - 1)
        def _():
            stp_ref[i // TILES_PER_SEQ] = ext_ref[HALO_P + TM - POOL_PAST:HALO_P + TM, :]

        ext_ref[0:HALO_P, :] = ext_ref[TM:TM + HALO_P, :]

    @pl.when(i == N_PT)
    def _():
        if first_layer:
            for t in range(DEC_SEQ):
                xn_ref[t * DEC_BATCH:(t + 1) * DEC_BATCH, :] = (
                    _rms(xs_ref[:, t, :], gain).astype(bf16))
        else:
            _norm_rows(xp_ref, put_xn, M_S)
        p = _dot(xn_ref[0:M_S, :], wb_ref[...])

        def slab(s, lanes):
            if s < POOL_PAST:
                return hist_ref[s, :, lanes]
            r0 = (s - POOL_PAST) * DEC_BATCH
            return p[r0:r0 + DEC_BATCH, lanes]

        for t in range(DEC_SEQ):
            rows = slice(t * DEC_BATCH, (t + 1) * DEC_BATCH)
            for g, win in enumerate(POOL_WINDOWS):
                lanes = slice(g * gdim, (g + 1) * gdim)
                cur = slab(POOL_PAST + t, lanes)
                tot = cur
                for k in range(1, win):
                    tot = tot + slab(POOL_PAST + t - k, lanes)
                cnt = float(min(win, PAST_LEN + t + 1))
                pm = (tot / cnt - cur).astype(bf16)
                act_ref[rows, lanes] = (_dot(pm, pwb_ref[g]) * psc_ref[:, lanes]).astype(bf16)
        for t in range(DEC_SEQ):
            new_ref[:, t * D_BR:(t + 1) * D_BR] = p[t * DEC_BATCH:(t + 1) * DEC_BATCH]


def _stage_d(w_in, pool_w, pool_scale, state, norm_w, l, x=None, x_prompt=None, x_sample=None):
    ngroup = len(POOL_WINDOWS)
    gdim = D_BR // ngroup
    first_layer = x is None
    if first_layer:
        xs = [x_prompt.reshape(M_P, D_MODEL), x_sample]
        x_specs = [
            pl.BlockSpec((TM, D_MODEL), lambda i: (jnp.minimum(i, N_PT - 1), 0)),
            pl.BlockSpec((DEC_BATCH, DEC_SEQ, D_MODEL), lambda i: (0, 0, 0)),
        ]
    else:
        xs = [x]
        x_specs = [pl.BlockSpec((TM, D_MODEL), lambda i: (i, 0))]
    return pl.pallas_call(
        functools.partial(_stage_d_kernel, first_layer),
        grid=(N_TILES,),
        in_specs=x_specs + [
            pl.BlockSpec((None, 1, D_MODEL), lambda i: (l, 0, 0)),
            _once((None, D_MODEL, D_BR), lambda i: (l, 0, 7)),
            pl.BlockSpec((None, ngroup, gdim, gdim), lambda i: (l, 0, 0, 0)),
            pl.BlockSpec((None, 1, D_BR), lambda i: (l, 0, 0)),
            _once((None, POOL_PAST, DEC_BATCH, D_BR), lambda i: (l, 0, 0, 0)),
        ],
        out_specs=[
            pl.BlockSpec((TM, D_BR), lambda i: (i, 0)),
            pl.BlockSpec((BATCH, POOL_PAST, D_BR), lambda i: (0, 0, 0)),
            pl.BlockSpec((DEC_BATCH, DEC_SEQ * D_BR), lambda i: (0, 0)),
            pl.BlockSpec((TM, D_MODEL), lambda i: (i, 0)),
        ],
        out_shape=[
            jax.ShapeDtypeStruct((M_ALL, D_BR), bf16),
            jax.ShapeDtypeStruct((BATCH, POOL_PAST, D_BR), f32),
            jax.ShapeDtypeStruct((DEC_BATCH, DEC_SEQ * D_BR), f32),
            jax.ShapeDtypeStruct((M_ALL, D_MODEL), bf16),
        ],
        scratch_shapes=[
            pltpu.VMEM((D_MODEL, D_BR), bf16),
            pltpu.VMEM((ngroup, gdim, gdim), bf16),
            pltpu.VMEM((HALO_P + TM, D_BR), f32),
        ],
        compiler_params=_cparams(1),
        name="branch_d",
    )(*xs, norm_w.reshape(DEPTH, 1, D_MODEL), w_in, pool_w, pool_scale.reshape(DEPTH, 1, D_BR),
      _time_major(state))


def _stage_g_kernel(xn_ref, a0_ref, a1_ref, a2_ref, a3_ref,
                    g0_ref, g1_ref, g2_ref, g3_ref, wbr_ref,
                    out_ref, wgb_ref, wbb_ref):
    i = pl.program_id(1)

    @pl.when(i == 0)
    def _():
        for b, g_ref in enumerate((g0_ref, g1_ref, g2_ref, g3_ref)):
            _cast_rows(g_ref, wgb_ref, b * TN_G)
        wbb_ref[...] = wbr_ref[...].astype(bf16)

    def compute(rows):
        merged = None
        for b, a_ref in enumerate((a0_ref, a1_ref, a2_ref, a3_ref)):
            gate = _sigmoid(_dot(xn_ref[rows, :], wgb_ref[:, b * TN_G:(b + 1) * TN_G]))
            term = gate * _dot(a_ref[rows, :], wbb_ref[b])
            merged = term if merged is None else merged + term
        out_ref[rows, :] = merged.astype(bf16)

    @pl.when(i > 0)
    def _():
        compute(slice(0, TM))

    @pl.when(i == 0)
    def _():
        compute(slice(0, M_S))


def _stage_g(xn, acts, w_in, w_branch, l):
    gate_blk0 = GATE_COL0 // TN_G
    per_branch = D_MODEL // TN_G
    gspec = lambda b: pl.BlockSpec(
        (None, D_MODEL, TN_G), lambda c, i: (l, 0, gate_blk0 + b * per_branch + c))
    aspec = pl.BlockSpec((TM, D_BR), lambda c, i: (_sample_first(i), 0))
    return pl.pallas_call(
        _stage_g_kernel,
        grid=(D_MODEL // TN_G, N_TILES),
        in_specs=[
            pl.BlockSpec((TM, D_MODEL), lambda c, i: (_sample_first(i), 0)),
            aspec, aspec, aspec, aspec,
            gspec(0), gspec(1), gspec(2), gspec(3),
            pl.BlockSpec((None, N_BRANCH, D_BR, TN_G), lambda c, i: (l, 0, 0, c)),
        ],
        out_specs=pl.BlockSpec((TM, TN_G), lambda c, i: (_sample_first(i), c)),
        out_shape=jax.ShapeDtypeStruct((M_ALL, D_MODEL), bf16),
        scratch_shapes=[
            pltpu.VMEM((D_MODEL, N_BRANCH * TN_G), bf16),
            pltpu.VMEM((N_BRANCH, D_BR, TN_G), bf16),
        ],
        compiler_params=_cparams(2),
        name="gate_merge",
    )(xn, *acts, w_in, w_in, w_in, w_in, w_branch)


def _stage_res_kernel(lhs_ref, w_ref, x_ref, o_ref, wb_ref):
    i = pl.program_id(1)

    @pl.when(i == 0)
    def _():
        _cast_rows(w_ref, wb_ref)

    @pl.when(i > 0)
    def _():
        o_ref[...] = x_ref[...] + _dot(lhs_ref[...], wb_ref[...])

    @pl.when(i == 0)
    def _():
        o_ref[0:M_S, :] = x_ref[0:M_S, :] + _dot(lhs_ref[0:M_S, :], wb_ref[...])


def _stage_res(lhs, w, x, l, tn, name, single_buffer_w):
    k = lhs.shape[1]
    wshape = (None, k, tn)
    wmap = lambda c, i: (l, 0, c)
    wspec = _once(wshape, wmap) if single_buffer_w else pl.BlockSpec(wshape, wmap)
    return pl.pallas_call(
        _stage_res_kernel,
        grid=(D_MODEL // tn, N_TILES),
        in_specs=[
            pl.BlockSpec((TM, k), lambda c, i: (_sample_first(i), 0)),
            wspec,
            pl.BlockSpec((TM, tn), lambda c, i: (_sample_first(i), c)),
        ],
        out_specs=pl.BlockSpec((TM, tn), lambda c, i: (_sample_first(i), c)),
        out_shape=jax.ShapeDtypeStruct((M_ALL, D_MODEL), f32),
        scratch_shapes=[pltpu.VMEM((k, tn), bf16)],
        compiler_params=_cparams(2),
        name=name,
    )(lhs, w, x)


def _stage_o_kernel(first_layer, *refs):
    i = pl.program_id(0)
    if first_layer:
        m_ref, w_ref, xp_ref, xs_ref, g_ref, x_ref, xn_ref, wb_ref = refs
    else:
        m_ref, w_ref, x_hbm, g_ref, x_ref, xn_ref, wb_ref, xbuf, sem = refs
        n_x = M_ALL // TM_O
        slot = lax.rem(i, O_SLOTS)

        def fetch(step):
            r0 = pl.multiple_of(step * TM_O, TM_O)
            s = lax.rem(step, O_SLOTS)
            return pltpu.make_async_copy(x_hbm.at[pl.ds(r0, TM_O), :], xbuf.at[s], sem.at[s])

        @pl.when(i == 0)
        def _():
            for s in range(O_SLOTS - 1):
                fetch(s).start()

        @pl.when(i + (O_SLOTS - 1) < n_x)
        def _():
            fetch(i + (O_SLOTS - 1)).start()

    @pl.when(i == 0)
    def _():
        _cast_rows(w_ref, wb_ref)

    g = g_ref[...]

    def finish(r0, nrows, resid_rows, proj):
        for c in range(nrows // LN_R):
            rows = slice(r0 + c * LN_R, r0 + (c + 1) * LN_R)
            x = resid_rows(slice(c * LN_R, (c + 1) * LN_R)) + proj[rows]
            x_ref[rows, :] = x
            xn_ref[rows, :] = _rms(x, g).astype(bf16)

    if first_layer:
        @pl.when(i < M_P // TM_O)
        def _():
            finish(0, TM_O, lambda r: xp_ref[r, :], _dot(m_ref[...], wb_ref[...]))

        @pl.when(i >= M_P // TM_O)
        def _():
            proj = _dot(m_ref[...], wb_ref[...])
            for t in range(DEC_SEQ):
                finish(t * DEC_BATCH, DEC_BATCH,
                       lambda r, t=t: xs_ref[r, t, :], proj)
    else:
        fetch(i).wait()
        finish(0, TM_O, lambda r: xbuf[slot, r, :], _dot(m_ref[...], wb_ref[...]))


def _stage_o(merged, w_o, norm_w, l, x=None, x_prompt=None, x_sample=None):
    first_layer = x is None
    n_steps = M_ALL // TM_O
    row_blk = pl.BlockSpec((TM_O, D_MODEL), lambda i: (i, 0))
    if first_layer:
        n_p = M_P // TM_O
        resid = [x_prompt.reshape(M_P, D_MODEL), x_sample]
        resid_specs = [
            pl.BlockSpec((TM_O, D_MODEL), lambda i: (jnp.minimum(i, n_p - 1), 0)),
            pl.BlockSpec((DEC_BATCH, DEC_SEQ, D_MODEL), lambda i: (0, 0, 0)),
        ]
        ring = []
    else:
        resid = [x]
        resid_specs = [pl.BlockSpec(memory_space=pl.ANY)]
        ring = [pltpu.VMEM((O_SLOTS, TM_O, D_MODEL), f32),
                pltpu.SemaphoreType.DMA((O_SLOTS,))]
    return pl.pallas_call(
        functools.partial(_stage_o_kernel, first_layer),
        grid=(n_steps,),
        in_specs=[row_blk, _once((None, D_MODEL, D_MODEL), lambda i: (l, 0, 0))]
        + resid_specs + [pl.BlockSpec((None, 1, D_MODEL), lambda i: (l, 0, 0))],
        out_specs=[row_blk, row_blk],
        out_shape=[
            jax.ShapeDtypeStruct((M_ALL, D_MODEL), f32),
            jax.ShapeDtypeStruct((M_ALL, D_MODEL), bf16),
        ],
        scratch_shapes=[pltpu.VMEM((D_MODEL, D_MODEL), bf16)] + ring,
        compiler_params=_cparams(1),
        name="out_proj_norm",
    )(merged, w_o, *resid, norm_w.reshape(DEPTH, 1, D_MODEL))


def _stage_u_kernel(xn_ref, wg_ref, wv_ref, dwg_ref, dwv_ref,
                    hgh_ref, hvh_ref,
                    a_ref, stp_ref, sts_ref, wb_ref, ext_ref):
    i = pl.program_id(1)
    nblk = TK_U // U_BLK
    npast = FFN_K - 1

    def _blk(k):
        return slice(k * U_BLK, (k + 1) * U_BLK)

    @pl.when(i == 0)
    def _():
        moves = []
        for k in range(nblk):
            moves.append((wg_ref, k * U_BLK, 2 * k * U_BLK, U_BLK))
            moves.append((wv_ref, k * U_BLK, (2 * k + 1) * U_BLK, U_BLK))
        _cast_blocks(moves, wb_ref)
        ext_ref[0:HALO_S, :] = jnp.zeros((HALO_S, 2 * TK_U), f32)

    def taps(dw_ref, k):
        return dw_ref[0:1, _blk(k)], dw_ref[1:2, _blk(k)], dw_ref[2:3, _blk(k)]

    @pl.when(i > 0)
    def _():
        first = lax.rem(i - 1, TILES_PER_SEQ) == 0
        ext_ref[0:HALO_S, :] = jnp.where(first, 0.0, ext_ref[0:HALO_S, :])
        h = _dot(xn_ref[...], wb_ref[...])
        ext_ref[HALO_S:HALO_S + TM, :] = h
        for k in range(nblk):
            for c in range(TM // U_R):
                r0 = HALO_S + c * U_R

                def conv(dw_ref, lanes):
                    w0, w1, w2 = taps(dw_ref, k)
                    return (w0 * ext_ref[r0 - 2:r0 - 2 + U_R, lanes]
                            + w1 * ext_ref[r0 - 1:r0 - 1 + U_R, lanes]
                            + w2 * h[c * U_R:(c + 1) * U_R, lanes])

                hg = conv(dwg_ref, _blk(2 * k))
                hv = conv(dwv_ref, _blk(2 * k + 1))
                a_ref[c * U_R:(c + 1) * U_R, _blk(k)] = (_silu(hg) * hv).astype(bf16)
        b = (i - 1) // TILES_PER_SEQ
        last = slice(HALO_S + TM - npast, HALO_S + TM)
        for k in range(nblk):
            stp_ref[b, 0, :, _blk(k)] = ext_ref[last, _blk(2 * k)]
            stp_ref[b, 1, :, _blk(k)] = ext_ref[last, _blk(2 * k + 1)]
        ext_ref[0:HALO_S, :] = ext_ref[TM:TM + HALO_S, :]

    @pl.when(i == 0)
    def _():
        h = _dot(xn_ref[0:M_S, :], wb_ref[...])

        def conv_slabs(dw_ref, hist_ref, k, lanes):
            w0, w1, w2 = taps(dw_ref, k)
            xp = [hist_ref[:, r, _blk(k)] for r in range(npast)]
            xp += [h[t * DEC_BATCH:(t + 1) * DEC_BATCH, lanes] for t in range(DEC_SEQ)]
            return [w0 * xp[t] + w1 * xp[t + 1] + w2 * xp[t + 2] for t in range(DEC_SEQ)]

        for k in range(nblk):
            hg = conv_slabs(dwg_ref, hgh_ref, k, _blk(2 * k))
            hv = conv_slabs(dwv_ref, hvh_ref, k, _blk(2 * k + 1))
            for t in range(DEC_SEQ):
                a_ref[t * DEC_BATCH:(t + 1) * DEC_BATCH, _blk(k)] = (
                    _silu(hg[t]) * hv[t]).astype(bf16)
            for r in range(npast):
                t = DEC_SEQ - npast + r
                rows = slice(t * DEC_BATCH, (t + 1) * DEC_BATCH)
                sts_ref[r, 0, :, _blk(k)] = h[rows, _blk(2 * k)]
                sts_ref[r, 1, :, _blk(k)] = h[rows, _blk(2 * k + 1)]


def _stage_u(xn, ffn_up, ffn_dw, state, l):
    nk = D_FF // TK_U
    npast = FFN_K - 1
    wspec = lambda half: pl.BlockSpec(
        (None, D_MODEL, TK_U), lambda k, i: (l, 0, half * nk + k))
    dspec = lambda half: pl.BlockSpec(
        (None, FFN_K, TK_U), lambda k, i: (l, 0, half * nk + k))
    hspec = lambda half: pl.BlockSpec(
        (None, DEC_BATCH, npast, TK_U), lambda k, i: (l, 0, 0, half * nk + k))
    a, stp, sts = pl.pallas_call(
        _stage_u_kernel,
        grid=(nk, N_TILES),
        in_specs=[
            pl.BlockSpec((TM, D_MODEL), lambda k, i: (_sample_first(i), 0)),
            wspec(0), wspec(1), dspec(0), dspec(1),
            hspec(0), hspec(1),
        ],
        out_specs=[
            pl.BlockSpec((TM, TK_U), lambda k, i: (_sample_first(i), k)),
            pl.BlockSpec((BATCH, 2, npast, TK_U), lambda k, i: (0, 0, 0, k)),
            pl.BlockSpec((npast, 2, DEC_BATCH, TK_U), lambda k, i: (0, 0, 0, k)),
        ],
        out_shape=[
            jax.ShapeDtypeStruct((M_ALL, D_FF), bf16),
            jax.ShapeDtypeStruct((BATCH, 2, npast, D_FF), f32),
            jax.ShapeDtypeStruct((npast, 2, DEC_BATCH, D_FF), f32),
        ],
        scratch_shapes=[
            pltpu.VMEM((D_MODEL, 2 * TK_U), bf16),
            pltpu.VMEM((HALO_S + TM, 2 * TK_U), f32),
        ],
        compiler_params=_cparams(2),
        name="ffn_up",
    )(xn, ffn_up, ffn_up, ffn_dw, ffn_dw, state, state)
    stp = jnp.transpose(stp, (0, 2, 1, 3)).reshape(BATCH, npast, 2 * D_FF)
    return a, stp, sts


def _ffn_state_kernel(*refs):
    in_refs, out_ref = refs[:-1], refs[-1]
    l = pl.program_id(0)
    for d, in_ref in enumerate(in_refs):
        @pl.when(l == d)
        def _(in_ref=in_ref):
            for r in range(FFN_K - 1):
                out_ref[:, r, :] = in_ref[r]


def _ffn_state(per_layer):
    npast = FFN_K - 1
    ncb = D_FF // FFN_ST_W
    spec = pl.BlockSpec((npast, None, DEC_BATCH, FFN_ST_W), lambda l, h, c: (0, h, 0, c))
    return pl.pallas_call(
        _ffn_state_kernel,
        grid=(DEPTH, 2, ncb),
        in_specs=[spec] * DEPTH,
        out_specs=pl.BlockSpec((None, DEC_BATCH, npast, FFN_ST_W),
                               lambda l, h, c: (l, 0, 0, h * ncb + c)),
        out_shape=jax.ShapeDtypeStruct((DEPTH, DEC_BATCH, npast, 2 * D_FF), f32),
        compiler_params=_cparams(3),
        name="ffn_state",
    )(*per_layer)


def kernel(x_prompt, x_sample, state_conf_conv, state_sconv, state_pool, state_ffn_conv,
           norm_mix, w_in, conf_dw, conf_ln_g, conf_ln_b, gmlp_ln_g, gmlp_ln_b, gmlp_ws,
           gmlp_b, sconv_dw, pool_w, pool_scale, w_branch, w_o, norm_ffn, ffn_up, ffn_dw,
           ffn_down, norm_final):
    conf_p, conf_new, sconv_p, sconv_s, pool_p, pool_new, ffn_p, ffn_new, v_s = (
        [] for _ in range(9))
    x = None
    for l in range(DEPTH):
        if l == 0:
            act_d, st_p, new, xn = _stage_d(w_in, pool_w, pool_scale, state_pool, norm_mix, l,
                                            x_prompt=x_prompt, x_sample=x_sample)
        else:
            act_d, st_p, new, xn = _stage_d(w_in, pool_w, pool_scale, state_pool, norm_mix, l, x=x)
        pool_p.append(st_p)
        pool_new.append(new)
        act_a, st_p, new = _stage_a(xn, w_in, conf_dw, conf_ln_g, conf_ln_b, state_conf_conv, l)
        conf_p.append(st_p)
        conf_new.append(new)
        act_b, v = _stage_b(xn, w_in, gmlp_ln_g, gmlp_ln_b, gmlp_ws, gmlp_b, l)
        v_s.append(v)
        act_c, st_p, st_s = _stage_c(xn, w_in, sconv_dw, state_sconv, l)
        sconv_p.append(st_p)
        sconv_s.append(st_s)
        merged = _stage_g(xn, (act_a, act_b, act_c, act_d), w_in, w_branch, l)
        if l == 0:
            x, xn = _stage_o(merged, w_o, norm_ffn, l, x_prompt=x_prompt, x_sample=x_sample)
        else:
            x, xn = _stage_o(merged, w_o, norm_ffn, l, x=x)
        a, st_p, new = _stage_u(xn, ffn_up, ffn_dw, state_ffn_conv, l)
        ffn_p.append(st_p)
        ffn_new.append(new)
        x = _stage_res(a, ffn_down, x, l, TN_D, "ffn_down", False)
    y_prompt, y_sample = _final_norm(x, norm_final)
    conf_s = _roll_state(state_conf_conv, conf_new)
    pool_s = _roll_state(state_pool, pool_new)
    ffn_s = _ffn_state(ffn_new)
    st = jnp.stack
    return (y_prompt, y_sample, st(conf_p), conf_s, st(sconv_p), st(sconv_s),
            st(pool_p), pool_s, st(ffn_p), ffn_s, st(v_s))
```

```python
import functools

import jax
import jax.numpy as jnp
from jax import lax
from jax.experimental import pallas as pl
from jax.experimental.pallas import tpu as pltpu

D_MODEL = 2048
BATCH = 4
SEQ = 2048
DEPTH = 2
DEC_BATCH = 128
DEC_SEQ = 4
PAST_LEN = 16384
D_BR = 512
N_BRANCH = 4
CONF_K = 31
GMLP_HEADS = 4
HEAD_DIM = D_BR // GMLP_HEADS
CHUNK = 128
SCONV_K = 3
POOL_WINDOWS = (2, 4, 8, 16)
POOL_PAST = 15
D_FF = 5632
FFN_K = 3
EPS = 1e-6
GATE_COL0 = 8 * D_BR

TM = 1024
M_P = BATCH * SEQ
M_S = DEC_BATCH * DEC_SEQ
M_ALL = M_P + M_S
N_PT = M_P // TM
TILES_PER_SEQ = SEQ // TM
N_TILES = N_PT + 1

LANES = 128
HALO_A = 32
HALO_P = 16
HALO_S = 8
CONV_R = 128
EP_R = 128
LN_R = 64
ROLL_B = 128
FFN_ST_W = 2816
U_R = 1024
U_BLK = 512

TN_G = 256
TM_O = 512
O_SLOTS = 3
TK_U = 512
TN_D = 512

VMEM_LIMIT = 60000 * 1024

f32 = jnp.float32
bf16 = jnp.bfloat16


def _cparams(n_axes):
    return pltpu.CompilerParams(
        dimension_semantics=("arbitrary",) * n_axes, vmem_limit_bytes=VMEM_LIMIT)


def _once(block_shape, index_map):
    return pl.BlockSpec(block_shape, index_map, pipeline_mode=pl.Buffered(1))


def _rms(x, g):
    return x * lax.rsqrt(jnp.mean(x * x, axis=-1, keepdims=True) + EPS) * g


def _layernorm(x, g, b):
    mu = jnp.mean(x, axis=-1, keepdims=True)
    d = x - mu
    var = jnp.mean(d * d, axis=-1, keepdims=True)
    return d * lax.rsqrt(var + EPS) * g + b


def _sigmoid(x):
    return 1.0 / (1.0 + jnp.exp(-x))


def _silu(x):
    h = 0.5 * x
    return h + h * jnp.tanh(h)


def _gelu_tanh(x):
    c = 0.7978845608028654
    return 0.5 * x * (1.0 + jnp.tanh(c * (x + 0.044715 * (x * x * x))))


def _cast_blocks(moves, wb_ref, rows_per_step=256):
    k = wb_ref.shape[0]

    def body(c, carry):
        r = pl.multiple_of(c * rows_per_step, rows_per_step)
        for w_ref, src, dst, n in moves:
            wb_ref[pl.ds(r, rows_per_step), dst:dst + n] = (
                w_ref[pl.ds(r, rows_per_step), src:src + n].astype(bf16))
        return carry

    lax.fori_loop(0, k // rows_per_step, body, 0)


def _cast_rows(w_ref, wb_ref, col0=0):
    _cast_blocks([(w_ref, 0, col0, w_ref.shape[1])], wb_ref)


def _dot(a, b):
    return jnp.dot(a, b, preferred_element_type=f32)


def _blk(k):
    return slice(k * LANES, (k + 1) * LANES)


def _sample_first(i):
    return lax.rem(i + N_PT, N_TILES)


def _norm_rows(src_ref, put, rows, chunk=256):
    def body(c, carry):
        r = pl.multiple_of(c * chunk, chunk)
        put(r, chunk, src_ref[pl.ds(r, chunk), :])
        return carry

    lax.fori_loop(0, rows // chunk, body, 0)


def _final_norm_kernel(x_ref, g_ref, yp_ref, ys_ref):
    i = pl.program_id(0)
    g = g_ref[...]

    @pl.when(i < N_PT)
    def _():
        def put(r, n, x):
            yp_ref[pl.ds(r, n), :] = _rms(x, g)
        _norm_rows(x_ref, put, TM)

    @pl.when(i == N_PT)
    def _():
        for t in range(DEC_SEQ):
            x = x_ref[t * DEC_BATCH:(t + 1) * DEC_BATCH, :]
            ys_ref[:, t, :] = _rms(x, g)


def _final_norm(x, norm_w):
    yp, ys = pl.pallas_call(
        _final_norm_kernel,
        grid=(N_TILES,),
        in_specs=[
            pl.BlockSpec((TM, D_MODEL), lambda i: (i, 0)),
            pl.BlockSpec((1, D_MODEL), lambda i: (0, 0)),
        ],
        out_specs=[
            pl.BlockSpec((TM, D_MODEL), lambda i: (jnp.minimum(i, N_PT - 1), 0)),
            pl.BlockSpec((DEC_BATCH, DEC_SEQ, D_MODEL), lambda i: (0, 0, 0)),
        ],
        out_shape=[
            jax.ShapeDtypeStruct((M_P, D_MODEL), f32),
            jax.ShapeDtypeStruct((DEC_BATCH, DEC_SEQ, D_MODEL), f32),
        ],
        compiler_params=_cparams(1),
        name="final_norm",
    )(x, norm_w.reshape(1, D_MODEL))
    return yp.reshape(BATCH, SEQ, D_MODEL), ys


def _time_major(state):
    return jnp.transpose(state, (0, 2, 1, 3))


def _roll_state_kernel(old_ref, *refs):
    new_refs, out_ref = refs[:-1], refs[-1]
    l = pl.program_id(0)
    npast, width = old_ref.shape[0], old_ref.shape[2]
    keep = npast - DEC_SEQ
    for s in range(keep):
        out_ref[s] = old_ref[DEC_SEQ + s]
    for d, new_ref in enumerate(new_refs):
        @pl.when(l == d)
        def _(new_ref=new_ref):
            for t in range(DEC_SEQ):
                out_ref[keep + t] = new_ref[:, t * width:(t + 1) * width]


def _roll_state(old, new_rows):
    old_t = _time_major(old)
    npast, width = old_t.shape[1], old_t.shape[3]
    blk = pl.BlockSpec((None, npast, ROLL_B, width), lambda l, j: (l, 0, j, 0))
    nspec = pl.BlockSpec((ROLL_B, DEC_SEQ * width), lambda l, j: (j, 0))
    out = pl.pallas_call(
        _roll_state_kernel,
        grid=(DEPTH, DEC_BATCH // ROLL_B),
        in_specs=[blk] + [nspec] * DEPTH,
        out_specs=blk,
        out_shape=jax.ShapeDtypeStruct(old_t.shape, f32),
        compiler_params=_cparams(2),
        name="roll_state",
    )(old_t, *new_rows)
    return _time_major(out)


def _conv31_chunk(ext_ref, base, dw_ref, lanes):
    y = None
    for r in range(8):
        rows = CONV_R if r == 0 else CONV_R + 8
        z = None
        for a in range(5):
            j = 8 * a + r - 2
            if 0 <= j < CONF_K:
                term = dw_ref[pl.ds(j, 1), lanes] * ext_ref[pl.ds(base + 8 * a, rows), lanes]
                z = term if z is None else z + term
        zr = z[r:r + CONV_R]
        y = zr if y is None else y + zr
    return y


def _stage_a_kernel(xn_ref, w_ref, dw_ref, lng_ref, lnb_ref, hist_ref,
                    act_ref, stp_ref, new_ref, wb_ref, ext_ref, y_ref):
    i = pl.program_id(0)
    nblk = D_BR // LANES
    npast = CONF_K - 1

    @pl.when(i == 0)
    def _():
        moves = []
        for k in range(nblk):
            moves.append((w_ref, k * LANES, 2 * k * LANES, LANES))
            moves.append((w_ref, D_BR + k * LANES, (2 * k + 1) * LANES, LANES))
        _cast_blocks(moves, wb_ref)
        ext_ref[0:HALO_A, :] = jnp.zeros((HALO_A, D_BR), f32)

    lng = lng_ref[...]
    lnb = lnb_ref[...]

    def glu_block(p, k, rows):
        return p[rows, _blk(2 * k)] * _sigmoid(p[rows, _blk(2 * k + 1)])

    @pl.when(i < N_PT)
    def _():
        first = lax.rem(i, TILES_PER_SEQ) == 0
        ext_ref[0:HALO_A, :] = jnp.where(first, 0.0, ext_ref[0:HALO_A, :])
        p = _dot(xn_ref[...], wb_ref[...])
        for k in range(nblk):
            for c in range(TM // CONV_R):
                rows = slice(c * CONV_R, (c + 1) * CONV_R)
                ext_ref[HALO_A + c * CONV_R:HALO_A + (c + 1) * CONV_R, _blk(k)] = (
                    glu_block(p, k, rows))
            for c in range(TM // CONV_R):
                y_ref[c * CONV_R:(c + 1) * CONV_R, _blk(k)] = (
                    _conv31_chunk(ext_ref, c * CONV_R, dw_ref, _blk(k)))
        for c in range(TM // LN_R):
            rows = slice(c * LN_R, (c + 1) * LN_R)
            act_ref[rows, :] = _silu(_layernorm(y_ref[rows, :], lng, lnb)).astype(bf16)
        stp_ref[i // TILES_PER_SEQ] = ext_ref[HALO_A + TM - npast:HALO_A + TM, :]
        ext_ref[0:HALO_A, :] = ext_ref[TM:TM + HALO_A, :]

    @pl.when(i == N_PT)
    def _():
        p = _dot(xn_ref[0:M_S, :], wb_ref[...])
        glu = jnp.concatenate([glu_block(p, k, slice(0, M_S)) for k in range(nblk)], axis=1)
        rb = 16
        for q in range(DEC_BATCH // rb):
            y = [None] * DEC_SEQ
            for s in range(npast + DEC_SEQ):
                if s < npast:
                    slab = hist_ref[s, q * rb:(q + 1) * rb, :]
                else:
                    r0 = (s - npast) * DEC_BATCH + q * rb
                    slab = glu[r0:r0 + rb]
                for t in range(DEC_SEQ):
                    j = s - t
                    if 0 <= j < CONF_K:
                        term = dw_ref[pl.ds(j, 1), :] * slab
                        y[t] = term if y[t] is None else y[t] + term
            for t in range(DEC_SEQ):
                r0 = t * DEC_BATCH + q * rb
                act_ref[r0:r0 + rb, :] = _silu(_layernorm(y[t], lng, lnb)).astype(bf16)
        for t in range(DEC_SEQ):
            new_ref[:, t * D_BR:(t + 1) * D_BR] = glu[t * DEC_BATCH:(t + 1) * DEC_BATCH]


def _stage_a(xn, w_in, conf_dw, ln_g, ln_b, state, l):
    npast = CONF_K - 1
    return pl.pallas_call(
        _stage_a_kernel,
        grid=(N_TILES,),
        in_specs=[
            pl.BlockSpec((TM, D_MODEL), lambda i: (i, 0)),
            _once((None, D_MODEL, 2 * D_BR), lambda i: (l, 0, 0)),
            pl.BlockSpec((None, CONF_K, D_BR), lambda i: (l, 0, 0)),
            pl.BlockSpec((None, 1, D_BR), lambda i: (l, 0, 0)),
            pl.BlockSpec((None, 1, D_BR), lambda i: (l, 0, 0)),
            _once((None, npast, DEC_BATCH, D_BR), lambda i: (l, 0, 0, 0)),
        ],
        out_specs=[
            pl.BlockSpec((TM, D_BR), lambda i: (i, 0)),
            pl.BlockSpec((BATCH, npast, D_BR), lambda i: (0, 0, 0)),
            pl.BlockSpec((DEC_BATCH, DEC_SEQ * D_BR), lambda i: (0, 0)),
        ],
        out_shape=[
            jax.ShapeDtypeStruct((M_ALL, D_BR), bf16),
            jax.ShapeDtypeStruct((BATCH, npast, D_BR), f32),
            jax.ShapeDtypeStruct((DEC_BATCH, DEC_SEQ * D_BR), f32),
        ],
        scratch_shapes=[
            pltpu.VMEM((D_MODEL, 2 * D_BR), bf16),
            pltpu.VMEM((HALO_A + TM, D_BR), f32),
            pltpu.VMEM((TM, D_BR), f32),
        ],
        compiler_params=_cparams(1),
        name="branch_a",
    )(xn, w_in, conf_dw, ln_g.reshape(DEPTH, 1, D_BR), ln_b.reshape(DEPTH, 1, D_BR),
      _time_major(state))


def _stage_b_kernel(xn_ref, w_ref, lng_ref, lnb_ref, ws_ref, bt_ref, wss_ref, bss_ref,
                    act_ref, v_ref, wb_ref, tril_ref, bias_ref):
    i = pl.program_id(0)

    @pl.when(i == 0)
    def _():
        _cast_blocks([(w_ref, D_BR, 0, D_BR), (w_ref, 0, D_BR, D_BR)], wb_ref)
        row = lax.broadcasted_iota(jnp.int32, (CHUNK, CHUNK), 0)
        col = lax.broadcasted_iota(jnp.int32, (CHUNK, CHUNK), 1)
        for h in range(GMLP_HEADS):
            tril_ref[h] = jnp.where(row >= col, ws_ref[h], 0.0).astype(bf16)
            bias_ref[h] = jnp.broadcast_to(bt_ref[:, h:h + 1], (CHUNK, HEAD_DIM))

    lng = lng_ref[...]
    lnb = lnb_ref[...]

    @pl.when(i < N_PT)
    def _():
        p = _dot(xn_ref[...], wb_ref[...])
        for c in range(TM // CHUNK):
            rows = slice(c * CHUNK, (c + 1) * CHUNK)
            u = _gelu_tanh(p[rows, D_BR:])
            v = _layernorm(_gelu_tanh(p[rows, :D_BR]), lng, lnb).astype(bf16)
            for h in range(GMLP_HEADS):
                lanes = slice(h * HEAD_DIM, (h + 1) * HEAD_DIM)
                mixed = _dot(tril_ref[h], v[:, lanes]) + bias_ref[h]
                act_ref[rows, lanes] = (u[:, lanes] * mixed).astype(bf16)

    @pl.when(i == N_PT)
    def _():
        p = _dot(xn_ref[0:M_S, :], wb_ref[...])
        u = _gelu_tanh(p[:, D_BR:])
        v = _layernorm(_gelu_tanh(p[:, :D_BR]), lng, lnb)
        for t in range(DEC_SEQ):
            v_ref[:, t * D_BR:(t + 1) * D_BR] = v[t * DEC_BATCH:(t + 1) * DEC_BATCH]
        for t in range(DEC_SEQ):
            rows = slice(t * DEC_BATCH, (t + 1) * DEC_BATCH)
            for h in range(GMLP_HEADS):
                lanes = slice(h * HEAD_DIM, (h + 1) * HEAD_DIM)
                mixed = jnp.full((DEC_BATCH, HEAD_DIM), bss_ref[h * DEC_SEQ + t], f32)
                for s in range(t + 1):
                    coef = wss_ref[(h * DEC_SEQ + t) * DEC_SEQ + s]
                    mixed = mixed + coef * v[s * DEC_BATCH:(s + 1) * DEC_BATCH, lanes]
                act_ref[rows, lanes] = (u[rows, lanes] * mixed).astype(bf16)


def _stage_b(xn, w_in, ln_g, ln_b, gmlp_ws, gmlp_b, l):
    bias_t = jnp.swapaxes(gmlp_b, 1, 2)
    ws_small = gmlp_ws[l, :, :DEC_SEQ, :DEC_SEQ].reshape(-1)
    b_small = gmlp_b[l, :, :DEC_SEQ].reshape(-1)
    act, v = pl.pallas_call(
        _stage_b_kernel,
        grid=(N_TILES,),
        in_specs=[
            pl.BlockSpec((TM, D_MODEL), lambda i: (i, 0)),
            _once((None, D_MODEL, 2 * D_BR), lambda i: (l, 0, 1)),
            pl.BlockSpec((None, 1, D_BR), lambda i: (l, 0, 0)),
            pl.BlockSpec((None, 1, D_BR), lambda i: (l, 0, 0)),
            pl.BlockSpec((None, GMLP_HEADS, CHUNK, CHUNK), lambda i: (l, 0, 0, 0)),
            pl.BlockSpec((None, CHUNK, GMLP_HEADS), lambda i: (l, 0, 0)),
            pl.BlockSpec(memory_space=pltpu.SMEM),
            pl.BlockSpec(memory_space=pltpu.SMEM),
        ],
        out_specs=[
            pl.BlockSpec((TM, D_BR), lambda i: (i, 0)),
            pl.BlockSpec((DEC_BATCH, DEC_SEQ * D_BR), lambda i: (0, 0)),
        ],
        out_shape=[
            jax.ShapeDtypeStruct((M_ALL, D_BR), bf16),
            jax.ShapeDtypeStruct((DEC_BATCH, DEC_SEQ * D_BR), f32),
        ],
        scratch_shapes=[
            pltpu.VMEM((D_MODEL, 2 * D_BR), bf16),
            pltpu.VMEM((GMLP_HEADS, CHUNK, CHUNK), bf16),
            pltpu.VMEM((GMLP_HEADS, CHUNK, HEAD_DIM), f32),
        ],
        compiler_params=_cparams(1),
        name="branch_b",
    )(xn, w_in, ln_g.reshape(DEPTH, 1, D_BR), ln_b.reshape(DEPTH, 1, D_BR),
      gmlp_ws, bias_t, ws_small, b_small)
    return act, v.reshape(DEC_BATCH, DEC_SEQ, D_BR)


def _stage_c_kernel(xn_ref, w0_ref, w1_ref, w2_ref, dw_ref, hist_ref,
                    act_ref, stp_ref, sts_ref, wb_ref, ext_ref):
    i = pl.program_id(0)
    nblk = D_BR // LANES

    @pl.when(i == 0)
    def _():
        moves = []
        for k in range(nblk):
            for part, w_ref in enumerate((w0_ref, w1_ref, w2_ref)):
                moves.append((w_ref, k * LANES, (3 * k + part) * LANES, LANES))
        _cast_blocks(moves, wb_ref)
        ext_ref[0:HALO_S, :] = jnp.zeros((HALO_S, D_BR), f32)

    def taps(k):
        return dw_ref[0:1, _blk(k)], dw_ref[1:2, _blk(k)], dw_ref[2:3, _blk(k)]

    @pl.when(i < N_PT)
    def _():
        first = lax.rem(i, TILES_PER_SEQ) == 0
        ext_ref[0:HALO_S, :] = jnp.where(first, 0.0, ext_ref[0:HALO_S, :])
        p = _dot(xn_ref[...], wb_ref[...])
        for k in range(nblk):
            w0, w1, w2 = taps(k)
            for c in range(TM // EP_R):
                rows = slice(c * EP_R, (c + 1) * EP_R)
                r0 = HALO_S + c * EP_R
                ext_ref[r0:r0 + EP_R, _blk(k)] = p[rows, _blk(3 * k + 1)] * p[rows, _blk(3 * k + 2)]
                z = (w0 * ext_ref[r0 - 2:r0 - 2 + EP_R, _blk(k)]
                     + w1 * ext_ref[r0 - 1:r0 - 1 + EP_R, _blk(k)]
                     + w2 * ext_ref[r0:r0 + EP_R, _blk(k)])
                act_ref[rows, _blk(k)] = (p[rows, _blk(3 * k)] * z).astype(bf16)
        stp_ref[i // TILES_PER_SEQ] = ext_ref[HALO_S + TM - (SCONV_K - 1):HALO_S + TM, :]
        ext_ref[0:HALO_S, :] = ext_ref[TM:TM + HALO_S, :]

    @pl.when(i == N_PT)
    def _():
        p = _dot(xn_ref[0:M_S, :], wb_ref[...])
        for k in range(nblk):
            w0, w1, w2 = taps(k)
            s = p[:, _blk(3 * k + 1)] * p[:, _blk(3 * k + 2)]
            xp = [hist_ref[:, _blk(k)], hist_ref[:, D_BR + k * LANES:D_BR + (k + 1) * LANES]]
            xp += [s[t * DEC_BATCH:(t + 1) * DEC_BATCH] for t in range(DEC_SEQ)]
            for t in range(DEC_SEQ):
                z = w0 * xp[t] + w1 * xp[t + 1] + w2 * xp[t + 2]
                rows = slice(t * DEC_BATCH, (t + 1) * DEC_BATCH)
                act_ref[rows, _blk(k)] = (p[rows, _blk(3 * k)] * z).astype(bf16)
            sts_ref[:, _blk(k)] = xp[DEC_SEQ]
            sts_ref[:, D_BR + k * LANES:D_BR + (k + 1) * LANES] = xp[DEC_SEQ + 1]


def _stage_c(xn, w_in, sconv_dw, state, l):
    npast = SCONV_K - 1
    hist = state.reshape(DEPTH, DEC_BATCH, npast * D_BR)
    wspec = lambda cb: _once((None, D_MODEL, D_BR), lambda i: (l, 0, cb))
    act, stp, sts = pl.pallas_call(
        _stage_c_kernel,
        grid=(N_TILES,),
        in_specs=[
            pl.BlockSpec((TM, D_MODEL), lambda i: (i, 0)),
            wspec(4), wspec(5), wspec(6),
            pl.BlockSpec((None, SCONV_K, D_BR), lambda i: (l, 0, 0)),
            pl.BlockSpec((None, DEC_BATCH, npast * D_BR), lambda i: (l, 0, 0)),
        ],
        out_specs=[
            pl.BlockSpec((TM, D_BR), lambda i: (i, 0)),
            pl.BlockSpec((BATCH, npast, D_BR), lambda i: (0, 0, 0)),
            pl.BlockSpec((DEC_BATCH, npast * D_BR), lambda i: (0, 0)),
        ],
        out_shape=[
            jax.ShapeDtypeStruct((M_ALL, D_BR), bf16),
            jax.ShapeDtypeStruct((BATCH, npast, D_BR), f32),
            jax.ShapeDtypeStruct((DEC_BATCH, npast * D_BR), f32),
        ],
        scratch_shapes=[
            pltpu.VMEM((D_MODEL, 3 * D_BR), bf16),
            pltpu.VMEM((HALO_S + TM, D_BR), f32),
        ],
        compiler_params=_cparams(1),
        name="branch_c",
    )(xn, w_in, w_in, w_in, sconv_dw, hist)
    return act, stp, sts.reshape(DEC_BATCH, npast, D_BR)


def _stage_d_kernel(first_layer, *refs):
    if first_layer:
        (xp_ref, xs_ref, g_ref, w_ref, pw_ref, psc_ref, hist_ref,
         act_ref, stp_ref, new_ref, xn_ref, wb_ref, pwb_ref, ext_ref) = refs
    else:
        (xp_ref, g_ref, w_ref, pw_ref, psc_ref, hist_ref,
         act_ref, stp_ref, new_ref, xn_ref, wb_ref, pwb_ref, ext_ref) = refs
    i = pl.program_id(0)
    gdim = D_BR // len(POOL_WINDOWS)
    gain = g_ref[...]

    def put_xn(r, n, x):
        xn_ref[pl.ds(r, n), :] = _rms(x, gain).astype(bf16)

    @pl.when(i == 0)
    def _():
        _cast_rows(w_ref, wb_ref)
        pwb_ref[...] = pw_ref[...].astype(bf16)

    @pl.when(i < N_PT)
    def _():
        tis = i % TILES_PER_SEQ

        @pl.when(tis == 0)
        def _():
            ext_ref[0:HALO_P, :] = jnp.zeros((HALO_P, D_BR), f32)

        _norm_rows(xp_ref, put_xn, TM)
        ext_ref[HALO_P:HALO_P + TM, :] = _dot(xn_ref[...], wb_ref[...])
        pos1 = tis * TM + 1 + lax.broadcasted_iota(jnp.int32, (TM, gdim), 0)
        for g, win in enumerate(POOL_WINDOWS):
            lanes = slice(g * gdim, (g + 1) * gdim)
            cur = ext_ref[HALO_P:HALO_P + TM, lanes]
            tot = cur
            for k in range(1, win):
                tot = tot + ext_ref[HALO_P - k:HALO_P - k + TM, lanes]
            cnt = jnp.minimum(pos1, win).astype(f32)
            pm = (tot / cnt - cur).astype(bf16)
            act_ref[:, lanes] = (_dot(pm, pwb_ref[g]) * psc_ref[:, lanes]).astype(bf16)

        @pl.when(tis == TILES_PER_SEQ - 1)
        def _():
            stp_ref[i // TILES_PER_SEQ] = ext_ref[HALO_P + TM - POOL_PAST:HALO_P + TM, :]

        ext_ref[0:HALO_P, :] = ext_ref[TM:TM + HALO_P, :]

    @pl.when(i == N_PT)
    def _():
        if first_layer:
            for t in range(DEC_SEQ):
                xn_ref[t * DEC_BATCH:(t + 1) * DEC_BATCH, :] = (
                    _rms(xs_ref[:, t, :], gain).astype(bf16))
        else:
            _norm_rows(xp_ref, put_xn, M_S)
        p = _dot(xn_ref[0:M_S, :], wb_ref[...])

        def slab(s, lanes):
            if s < POOL_PAST:
                return hist_ref[s, :, lanes]
            r0 = (s - POOL_PAST) * DEC_BATCH
            return p[r0:r0 + DEC_BATCH, lanes]

        for t in range(DEC_SEQ):
            rows = slice(t * DEC_BATCH, (t + 1) * DEC_BATCH)
            for g, win in enumerate(POOL_WINDOWS):
                lanes = slice(g * gdim, (g + 1) * gdim)
                cur = slab(POOL_PAST + t, lanes)
                tot = cur
                for k in range(1, win):
                    tot = tot + slab(POOL_PAST + t - k, lanes)
                cnt = float(min(win, PAST_LEN + t + 1))
                pm = (tot / cnt - cur).astype(bf16)
                act_ref[rows, lanes] = (_dot(pm, pwb_ref[g]) * psc_ref[:, lanes]).astype(bf16)
        for t in range(DEC_SEQ):
            new_ref[:, t * D_BR:(t + 1) * D_BR] = p[t * DEC_BATCH:(t + 1) * DEC_BATCH]


def _stage_d(w_in, pool_w, pool_scale, state, norm_w, l, x=None, x_prompt=None, x_sample=None):
    ngroup = len(POOL_WINDOWS)
    gdim = D_BR // ngroup
    first_layer = x is None
    if first_layer:
        xs = [x_prompt.reshape(M_P, D_MODEL), x_sample]
        x_specs = [
            pl.BlockSpec((TM, D_MODEL), lambda i: (jnp.minimum(i, N_PT - 1), 0)),
            pl.BlockSpec((DEC_BATCH, DEC_SEQ, D_MODEL), lambda i: (0, 0, 0)),
        ]
    else:
        xs = [x]
        x_specs = [pl.BlockSpec((TM, D_MODEL), lambda i: (i, 0))]
    return pl.pallas_call(
        functools.partial(_stage_d_kernel, first_layer),
        grid=(N_TILES,),
        in_specs=x_specs + [
            pl.BlockSpec((None, 1, D_MODEL), lambda i: (l, 0, 0)),
            _once((None, D_MODEL, D_BR), lambda i: (l, 0, 7)),
            pl.BlockSpec((None, ngroup, gdim, gdim), lambda i: (l, 0, 0, 0)),
            pl.BlockSpec((None, 1, D_BR), lambda i: (l, 0, 0)),
            _once((None, POOL_PAST, DEC_BATCH, D_BR), lambda i: (l, 0, 0, 0)),
        ],
        out_specs=[
            pl.BlockSpec((TM, D_BR), lambda i: (i, 0)),
            pl.BlockSpec((BATCH, POOL_PAST, D_BR), lambda i: (0, 0, 0)),
            pl.BlockSpec((DEC_BATCH, DEC_SEQ * D_BR), lambda i: (0, 0)),
            pl.BlockSpec((TM, D_MODEL), lambda i: (i, 0)),
        ],
        out_shape=[
            jax.ShapeDtypeStruct((M_ALL, D_BR), bf16),
            jax.ShapeDtypeStruct((BATCH, POOL_PAST, D_BR), f32),
            jax.ShapeDtypeStruct((DEC_BATCH, DEC_SEQ * D_BR), f32),
            jax.ShapeDtypeStruct((M_ALL, D_MODEL), bf16),
        ],
        scratch_shapes=[
            pltpu.VMEM((D_MODEL, D_BR), bf16),
            pltpu.VMEM((ngroup, gdim, gdim), bf16),
            pltpu.VMEM((HALO_P + TM, D_BR), f32),
        ],
        compiler_params=_cparams(1),
        name="branch_d",
    )(*xs, norm_w.reshape(DEPTH, 1, D_MODEL), w_in, pool_w, pool_scale.reshape(DEPTH, 1, D_BR),
      _time_major(state))


def _stage_g_kernel(xn_ref, a0_ref, a1_ref, a2_ref, a3_ref,
                    g0_ref, g1_ref, g2_ref, g3_ref, wbr_ref,
                    out_ref, wgb_ref, wbb_ref):
    i = pl.program_id(1)

    @pl.when(i == 0)
    def _():
        for b, g_ref in enumerate((g0_ref, g1_ref, g2_ref, g3_ref)):
            _cast_rows(g_ref, wgb_ref, b * TN_G)
        wbb_ref[...] = wbr_ref[...].astype(bf16)

    def compute(rows):
        merged = None
        for b, a_ref in enumerate((a0_ref, a1_ref, a2_ref, a3_ref)):
            gate = _sigmoid(_dot(xn_ref[rows, :], wgb_ref[:, b * TN_G:(b + 1) * TN_G]))
            term = gate * _dot(a_ref[rows, :], wbb_ref[b])
            merged = term if merged is None else merged + term
        out_ref[rows, :] = merged.astype(bf16)

    @pl.when(i > 0)
    def _():
        compute(slice(0, TM))

    @pl.when(i == 0)
    def _():
        compute(slice(0, M_S))


def _stage_g(xn, acts, w_in, w_branch, l):
    gate_blk0 = GATE_COL0 // TN_G
    per_branch = D_MODEL // TN_G
    gspec = lambda b: pl.BlockSpec(
        (None, D_MODEL, TN_G), lambda c, i: (l, 0, gate_blk0 + b * per_branch + c))
    aspec = pl.BlockSpec((TM, D_BR), lambda c, i: (_sample_first(i), 0))
    return pl.pallas_call(
        _stage_g_kernel,
        grid=(D_MODEL // TN_G, N_TILES),
        in_specs=[
            pl.BlockSpec((TM, D_MODEL), lambda c, i: (_sample_first(i), 0)),
            aspec, aspec, aspec, aspec,
            gspec(0), gspec(1), gspec(2), gspec(3),
            pl.BlockSpec((None, N_BRANCH, D_BR, TN_G), lambda c, i: (l, 0, 0, c)),
        ],
        out_specs=pl.BlockSpec((TM, TN_G), lambda c, i: (_sample_first(i), c)),
        out_shape=jax.ShapeDtypeStruct((M_ALL, D_MODEL), bf16),
        scratch_shapes=[
            pltpu.VMEM((D_MODEL, N_BRANCH * TN_G), bf16),
            pltpu.VMEM((N_BRANCH, D_BR, TN_G), bf16),
        ],
        compiler_params=_cparams(2),
        name="gate_merge",
    )(xn, *acts, w_in, w_in, w_in, w_in, w_branch)


def _stage_res_kernel(lhs_ref, w_ref, x_ref, o_ref, wb_ref):
    i = pl.program_id(1)

    @pl.when(i == 0)
    def _():
        _cast_rows(w_ref, wb_ref)

    @pl.when(i > 0)
    def _():
        o_ref[...] = x_ref[...] + _dot(lhs_ref[...], wb_ref[...])

    @pl.when(i == 0)
    def _():
        o_ref[0:M_S, :] = x_ref[0:M_S, :] + _dot(lhs_ref[0:M_S, :], wb_ref[...])


def _stage_res(lhs, w, x, l, tn, name, single_buffer_w):
    k = lhs.shape[1]
    wshape = (None, k, tn)
    wmap = lambda c, i: (l, 0, c)
    wspec = _once(wshape, wmap) if single_buffer_w else pl.BlockSpec(wshape, wmap)
    return pl.pallas_call(
        _stage_res_kernel,
        grid=(D_MODEL // tn, N_TILES),
        in_specs=[
            pl.BlockSpec((TM, k), lambda c, i: (_sample_first(i), 0)),
            wspec,
            pl.BlockSpec((TM, tn), lambda c, i: (_sample_first(i), c)),
        ],
        out_specs=pl.BlockSpec((TM, tn), lambda c, i: (_sample_first(i), c)),
        out_shape=jax.ShapeDtypeStruct((M_ALL, D_MODEL), f32),
        scratch_shapes=[pltpu.VMEM((k, tn), bf16)],
        compiler_params=_cparams(2),
        name=name,
    )(lhs, w, x)


def _stage_o_kernel(first_layer, *refs):
    if first_layer:
        m_ref, w_ref, xp_ref, xs_ref, g_ref, x_ref, xn_ref, wb_ref = refs
    else:
        m_hbm, w_ref, xin_ref, g_ref, x_ref, xn_ref, wb_ref, mbuf, sem = refs
    i = pl.program_id(0)

    if not first_layer:
        slot = lax.rem(i, O_SLOTS)

        def fetch(step):
            r0 = pl.multiple_of(step * TM_O, TM_O)
            s = lax.rem(step, O_SLOTS)
            return pltpu.make_async_copy(m_hbm.at[pl.ds(r0, TM_O), :], mbuf.at[s], sem.at[s])

        @pl.when(i == 0)
        def _():
            for s in range(O_SLOTS - 1):
                fetch(s).start()

        @pl.when(i + (O_SLOTS - 1) < M_ALL // TM_O)
        def _():
            fetch(i + (O_SLOTS - 1)).start()

    @pl.when(i == 0)
    def _():
        _cast_rows(w_ref, wb_ref)

    g = g_ref[...]

    def finish(r0, nrows, resid_rows, proj):
        for c in range(nrows // LN_R):
            rows = slice(r0 + c * LN_R, r0 + (c + 1) * LN_R)
            x = resid_rows(slice(c * LN_R, (c + 1) * LN_R)) + proj[rows]
            x_ref[rows, :] = x
            xn_ref[rows, :] = _rms(x, g).astype(bf16)

    if first_layer:
        @pl.when(i < M_P // TM_O)
        def _():
            finish(0, TM_O, lambda r: xp_ref[r, :], _dot(m_ref[...], wb_ref[...]))

        @pl.when(i >= M_P // TM_O)
        def _():
            proj = _dot(m_ref[...], wb_ref[...])
            for t in range(DEC_SEQ):
                finish(t * DEC_BATCH, DEC_BATCH,
                       lambda r, t=t: xs_ref[r, t, :], proj)
    else:
        fetch(i).wait()
        finish(0, TM_O, lambda r: xin_ref[r, :], _dot(mbuf[slot], wb_ref[...]))


def _stage_o(merged, w_o, norm_w, l, x=None, x_prompt=None, x_sample=None):
    first_layer = x is None
    n_steps = M_ALL // TM_O
    row_blk = pl.BlockSpec((TM_O, D_MODEL), lambda i: (i, 0))
    if first_layer:
        n_p = M_P // TM_O
        resid = [x_prompt.reshape(M_P, D_MODEL), x_sample]
        resid_specs = [
            pl.BlockSpec((TM_O, D_MODEL), lambda i: (jnp.minimum(i, n_p - 1), 0)),
            pl.BlockSpec((DEC_BATCH, DEC_SEQ, D_MODEL), lambda i: (0, 0, 0)),
        ]
        m_spec, ring = row_blk, []
    else:
        resid = [x]
        resid_specs = [row_blk]
        m_spec = pl.BlockSpec(memory_space=pl.ANY)
        ring = [pltpu.VMEM((O_SLOTS, TM_O, D_MODEL), bf16),
                pltpu.SemaphoreType.DMA((O_SLOTS,))]
    return pl.pallas_call(
        functools.partial(_stage_o_kernel, first_layer),
        grid=(n_steps,),
        in_specs=[m_spec, _once((None, D_MODEL, D_MODEL), lambda i: (l, 0, 0))]
        + resid_specs + [pl.BlockSpec((None, 1, D_MODEL), lambda i: (l, 0, 0))],
        out_specs=[row_blk, row_blk],
        out_shape=[
            jax.ShapeDtypeStruct((M_ALL, D_MODEL), f32),
            jax.ShapeDtypeStruct((M_ALL, D_MODEL), bf16),
        ],
        scratch_shapes=[pltpu.VMEM((D_MODEL, D_MODEL), bf16)] + ring,
        compiler_params=_cparams(1),
        name="out_proj_norm",
    )(merged, w_o, *resid, norm_w.reshape(DEPTH, 1, D_MODEL))


def _stage_u_kernel(xn_ref, wg_ref, wv_ref, dwg_ref, dwv_ref,
                    hgh_ref, hvh_ref,
                    a_ref, stp_ref, sts_ref, wb_ref, ext_ref):
    i = pl.program_id(1)
    nblk = TK_U // U_BLK
    npast = FFN_K - 1

    def _blk(k):
        return slice(k * U_BLK, (k + 1) * U_BLK)

    @pl.when(i == 0)
    def _():
        moves = []
        for k in range(nblk):
            moves.append((wg_ref, k * U_BLK, 2 * k * U_BLK, U_BLK))
            moves.append((wv_ref, k * U_BLK, (2 * k + 1) * U_BLK, U_BLK))
        _cast_blocks(moves, wb_ref)
        ext_ref[0:HALO_S, :] = jnp.zeros((HALO_S, 2 * TK_U), f32)

    def taps(dw_ref, k):
        return dw_ref[0:1, _blk(k)], dw_ref[1:2, _blk(k)], dw_ref[2:3, _blk(k)]

    @pl.when(i > 0)
    def _():
        first = lax.rem(i - 1, TILES_PER_SEQ) == 0
        ext_ref[0:HALO_S, :] = jnp.where(first, 0.0, ext_ref[0:HALO_S, :])
        h = _dot(xn_ref[...], wb_ref[...])
        ext_ref[HALO_S:HALO_S + TM, :] = h
        for k in range(nblk):
            for c in range(TM // U_R):
                r0 = HALO_S + c * U_R

                def conv(dw_ref, lanes):
                    w0, w1, w2 = taps(dw_ref, k)
                    return (w0 * ext_ref[r0 - 2:r0 - 2 + U_R, lanes]
                            + w1 * ext_ref[r0 - 1:r0 - 1 + U_R, lanes]
                            + w2 * h[c * U_R:(c + 1) * U_R, lanes])

                hg = conv(dwg_ref, _blk(2 * k))
                hv = conv(dwv_ref, _blk(2 * k + 1))
                a_ref[c * U_R:(c + 1) * U_R, _blk(k)] = (_silu(hg) * hv).astype(bf16)
        b = (i - 1) // TILES_PER_SEQ
        last = slice(HALO_S + TM - npast, HALO_S + TM)
        for k in range(nblk):
            stp_ref[b, 0, :, _blk(k)] = ext_ref[last, _blk(2 * k)]
            stp_ref[b, 1, :, _blk(k)] = ext_ref[last, _blk(2 * k + 1)]
        ext_ref[0:HALO_S, :] = ext_ref[TM:TM + HALO_S, :]

    @pl.when(i == 0)
    def _():
        h = _dot(xn_ref[0:M_S, :], wb_ref[...])

        def conv_slabs(dw_ref, hist_ref, k, lanes):
            w0, w1, w2 = taps(dw_ref, k)
            xp = [hist_ref[:, r, _blk(k)] for r in range(npast)]
            xp += [h[t * DEC_BATCH:(t + 1) * DEC_BATCH, lanes] for t in range(DEC_SEQ)]
            return [w0 * xp[t] + w1 * xp[t + 1] + w2 * xp[t + 2] for t in range(DEC_SEQ)]

        for k in range(nblk):
            hg = conv_slabs(dwg_ref, hgh_ref, k, _blk(2 * k))
            hv = conv_slabs(dwv_ref, hvh_ref, k, _blk(2 * k + 1))
            for t in range(DEC_SEQ):
                a_ref[t * DEC_BATCH:(t + 1) * DEC_BATCH, _blk(k)] = (
                    _silu(hg[t]) * hv[t]).astype(bf16)
            for r in range(npast):
                t = DEC_SEQ - npast + r
                rows = slice(t * DEC_BATCH, (t + 1) * DEC_BATCH)
                sts_ref[r, 0, :, _blk(k)] = h[rows, _blk(2 * k)]
                sts_ref[r, 1, :, _blk(k)] = h[rows, _blk(2 * k + 1)]


def _stage_u(xn, ffn_up, ffn_dw, state, l):
    nk = D_FF // TK_U
    npast = FFN_K - 1
    wspec = lambda half: pl.BlockSpec(
        (None, D_MODEL, TK_U), lambda k, i: (l, 0, half * nk + k))
    dspec = lambda half: pl.BlockSpec(
        (None, FFN_K, TK_U), lambda k, i: (l, 0, half * nk + k))
    hspec = lambda half: pl.BlockSpec(
        (None, DEC_BATCH, npast, TK_U), lambda k, i: (l, 0, 0, half * nk + k))
    a, stp, sts = pl.pallas_call(
        _stage_u_kernel,
        grid=(nk, N_TILES),
        in_specs=[
            pl.BlockSpec((TM, D_MODEL), lambda k, i: (_sample_first(i), 0)),
            wspec(0), wspec(1), dspec(0), dspec(1),
            hspec(0), hspec(1),
        ],
        out_specs=[
            pl.BlockSpec((TM, TK_U), lambda k, i: (_sample_first(i), k)),
            pl.BlockSpec((BATCH, 2, npast, TK_U), lambda k, i: (0, 0, 0, k)),
            pl.BlockSpec((npast, 2, DEC_BATCH, TK_U), lambda k, i: (0, 0, 0, k)),
        ],
        out_shape=[
            jax.ShapeDtypeStruct((M_ALL, D_FF), bf16),
            jax.ShapeDtypeStruct((BATCH, 2, npast, D_FF), f32),
            jax.ShapeDtypeStruct((npast, 2, DEC_BATCH, D_FF), f32),
        ],
        scratch_shapes=[
            pltpu.VMEM((D_MODEL, 2 * TK_U), bf16),
            pltpu.VMEM((HALO_S + TM, 2 * TK_U), f32),
        ],
        compiler_params=_cparams(2),
        name="ffn_up",
    )(xn, ffn_up, ffn_up, ffn_dw, ffn_dw, state, state)
    stp = jnp.transpose(stp, (0, 2, 1, 3)).reshape(BATCH, npast, 2 * D_FF)
    return a, stp, sts


def _ffn_state_kernel(*refs):
    in_refs, out_ref = refs[:-1], refs[-1]
    l = pl.program_id(0)
    for d, in_ref in enumerate(in_refs):
        @pl.when(l == d)
        def _(in_ref=in_ref):
            for r in range(FFN_K - 1):
                out_ref[:, r, :] = in_ref[r]


def _ffn_state(per_layer):
    npast = FFN_K - 1
    ncb = D_FF // FFN_ST_W
    spec = pl.BlockSpec((npast, None, DEC_BATCH, FFN_ST_W), lambda l, h, c: (0, h, 0, c))
    return pl.pallas_call(
        _ffn_state_kernel,
        grid=(DEPTH, 2, ncb),
        in_specs=[spec] * DEPTH,
        out_specs=pl.BlockSpec((None, DEC_BATCH, npast, FFN_ST_W),
                               lambda l, h, c: (l, 0, 0, h * ncb + c)),
        out_shape=jax.ShapeDtypeStruct((DEPTH, DEC_BATCH, npast, 2 * D_FF), f32),
        compiler_params=_cparams(3),
        name="ffn_state",
    )(*per_layer)


def kernel(x_prompt, x_sample, state_conf_conv, state_sconv, state_pool, state_ffn_conv,
           norm_mix, w_in, conf_dw, conf_ln_g, conf_ln_b, gmlp_ln_g, gmlp_ln_b, gmlp_ws,
           gmlp_b, sconv_dw, pool_w, pool_scale, w_branch, w_o, norm_ffn, ffn_up, ffn_dw,
           ffn_down, norm_final):
    conf_p, conf_new, sconv_p, sconv_s, pool_p, pool_new, ffn_p, ffn_new, v_s = (
        [] for _ in range(9))
    x = None
    for l in range(DEPTH):
        if l == 0:
            act_d, st_p, new, xn = _stage_d(w_in, pool_w, pool_scale, state_pool, norm_mix, l,
                                            x_prompt=x_prompt, x_sample=x_sample)
        else:
            act_d, st_p, new, xn = _stage_d(w_in, pool_w, pool_scale, state_pool, norm_mix, l, x=x)
        pool_p.append(st_p)
        pool_new.append(new)
        act_a, st_p, new = _stage_a(xn, w_in, conf_dw, conf_ln_g, conf_ln_b, state_conf_conv, l)
        conf_p.append(st_p)
        conf_new.append(new)
        act_b, v = _stage_b(xn, w_in, gmlp_ln_g, gmlp_ln_b, gmlp_ws, gmlp_b, l)
        v_s.append(v)
        act_c, st_p, st_s = _stage_c(xn, w_in, sconv_dw, state_sconv, l)
        sconv_p.append(st_p)
        sconv_s.append(st_s)
        merged = _stage_g(xn, (act_a, act_b, act_c, act_d), w_in, w_branch, l)
        if l == 0:
            x, xn = _stage_o(merged, w_o, norm_ffn, l, x_prompt=x_prompt, x_sample=x_sample)
        else:
            x, xn = _stage_o(merged, w_o, norm_ffn, l, x=x)
        a, st_p, new = _stage_u(xn, ffn_up, ffn_dw, state_ffn_conv, l)
        ffn_p.append(st_p)
        ffn_new.append(new)
        x = _stage_res(a, ffn_down, x, l, TN_D, "ffn_down", False)
    y_prompt, y_sample = _final_norm(x, norm_final)
    conf_s = _roll_state(state_conf_conv, conf_new)
    pool_s = _roll_state(state_pool, pool_new)
    ffn_s = _ffn_state(ffn_new)
    st = jnp.stack
    return (y_prompt, y_sample, st(conf_p), conf_s, st(sconv_p), st(sconv_s),
            st(pool_p), pool_s, st(ffn_p), ffn_s, st(v_s))
```
